```python
import jax, jax.numpy as jnp
from jax import lax
import numpy as np

D_MODEL = 2048
BATCH = 8
SEQ = 4096
DEPTH = 4

N_MEM = 256
N_A_LAYERS = DEPTH // 2
N_B_LAYERS = DEPTH - N_A_LAYERS
GLA_HEADS = 4
GLA_DK = D_MODEL // 8
GLA_DV = 3 * D_MODEL // 16
GLA_QK_WIDTH = GLA_HEADS * GLA_DK
GLA_V_WIDTH = GLA_HEADS * GLA_DV
GLA_GATE_RANK = 16
GLA_GATE_TAU = 16.0
GLA_CHUNK = 64
FOX_HEADS = 12
FOX_HEAD_DIM = D_MODEL // 16
FOX_WIDTH = FOX_HEADS * FOX_HEAD_DIM
FOX_BLOCK = 128
MEM_HEADS = 4
MEM_HEAD_DIM = D_MODEL // 16
MEM_WIDTH = MEM_HEADS * MEM_HEAD_DIM
A_MIX_WIDTH = GLA_V_WIDTH + MEM_WIDTH
B_MIX_WIDTH = FOX_WIDTH + MEM_WIDTH
A_IN_WIDTH = 2 * GLA_QK_WIDTH + GLA_V_WIDTH + GLA_GATE_RANK + GLA_V_WIDTH + MEM_WIDTH
B_IN_WIDTH = FOX_WIDTH + MEM_WIDTH
FFN_HIDDEN = 11 * D_MODEL // 4
CONV_WIDTH = 3
EPS = 1e-6

kernel_name = 'hybrid_gla_fox_yoco_block'


def rmsnorm(x, g):
    xf = x.astype(jnp.float32)
    y = xf * lax.rsqrt(jnp.mean(xf * xf, axis=-1, keepdims=True) + EPS)
    return (y * g.astype(jnp.float32)).astype(x.dtype)


def split_cols(z, widths):
    out, start = [], 0
    for w in widths:
        out.append(z[..., start:start + w])
        start += w
    return out


def split_heads(t, n_heads):
    b, s, _ = t.shape
    return t.reshape(b, s, n_heads, -1).transpose(0, 2, 1, 3)


def merge_heads(t):
    b, h, s, d = t.shape
    return t.transpose(0, 2, 1, 3).reshape(b, s, h * d)


def memory_kv(mem_n, w_kv):
    mk, mv = split_cols(mem_n @ w_kv, [MEM_WIDTH, MEM_WIDTH])
    return split_heads(mk, MEM_HEADS), split_heads(mv, MEM_HEADS)


def memory_attention(q, mem_k, mem_v):
    qh = split_heads(q, MEM_HEADS)
    s = jnp.einsum('bhqd,bhkd->bhqk', qh, mem_k).astype(jnp.float32) * (MEM_HEAD_DIM ** -0.5)
    p = jax.nn.softmax(s, axis=-1).astype(mem_v.dtype)
    return merge_heads(jnp.einsum('bhqk,bhkd->bhqd', p, mem_v))


def gla_chunked(q, k, v, g):
    q, k, v, g = (t.astype(jnp.float32) for t in (q, k, v, g))
    b_, h_, s_, dk = q.shape
    dv = v.shape[-1]
    nc = s_ // GLA_CHUNK

    def to_chunks(t):
        return jnp.moveaxis(t.reshape(b_, h_, nc, GLA_CHUNK, t.shape[-1]), 2, 0)

    tri = jnp.tril(jnp.ones((GLA_CHUNK, GLA_CHUNK), dtype=bool))[:, :, None]

    def step(state, inp):
        qc, kc, vc, gc = inp
        bcum = jnp.cumsum(gc, axis=2)
        o_inter = jnp.einsum('bhtk,bhkv->bhtv', qc * jnp.exp(bcum), state)
        rel = bcum[:, :, :, None, :] - bcum[:, :, None, :, :]
        decay = jnp.exp(jnp.where(tri, rel, -jnp.inf))
        att = jnp.einsum('bhtk,bhsk,bhtsk->bhts', qc, kc, decay)
        o_intra = jnp.einsum('bhts,bhsv->bhtv', att, vc)
        b_last = bcum[:, :, -1:, :]
        new_state = jnp.exp(b_last)[:, :, 0, :, None] * state + jnp.einsum(
            'bhsk,bhsv->bhkv', kc * jnp.exp(b_last - bcum), vc)
        return new_state, o_inter + o_intra

    state0 = jnp.zeros((b_, h_, dk, dv), jnp.float32)
    _, out = lax.scan(step, state0, (to_chunks(q), to_chunks(k), to_chunks(v), to_chunks(g)))
    return jnp.moveaxis(out, 0, 2).reshape(b_, h_, s_, dv)


def gla_mixer(h, mem_k, mem_v, w_in, w_gate_up, b_gate, gn_gain, w_out):
    z = h @ w_in
    q, k, v, glr, og, mq = split_cols(
        z, [GLA_QK_WIDTH, GLA_QK_WIDTH, GLA_V_WIDTH, GLA_GATE_RANK, GLA_V_WIDTH, MEM_WIDTH])
    g = jax.nn.log_sigmoid((glr @ w_gate_up + b_gate).astype(jnp.float32)) / GLA_GATE_TAU
    o = gla_chunked(split_heads(q, GLA_HEADS) * (GLA_DK ** -0.5), split_heads(k, GLA_HEADS),
                    split_heads(v, GLA_HEADS), split_heads(g, GLA_HEADS))
    o = o * lax.rsqrt(jnp.mean(o * o, axis=-1, keepdims=True) + EPS)
    o = o * gn_gain.astype(jnp.float32).reshape(GLA_HEADS, 1, GLA_DV)
    o = merge_heads(o).astype(h.dtype) * jax.nn.silu(og)
    m = memory_attention(mq, mem_k, mem_v)
    return jnp.concatenate([o, m], axis=-1) @ w_out


def fox_shared_kv(x, g_kv, w_kv, b_f):
    hs = rmsnorm(x, g_kv)
    k, v, fl = split_cols(hs @ w_kv, [FOX_WIDTH, FOX_WIDTH, FOX_HEADS])
    log_f = jax.nn.log_sigmoid((fl + b_f).astype(jnp.float32))
    c = jnp.cumsum(log_f, axis=1).transpose(0, 2, 1)
    return split_heads(k, FOX_HEADS), split_heads(v, FOX_HEADS), c


def fox_attention(q, k, v, c):
    b_, h_, s_, dh = q.shape
    nb = s_ // FOX_BLOCK
    qb = jnp.moveaxis(q.reshape(b_, h_, nb, FOX_BLOCK, dh), 2, 0)
    cb = jnp.moveaxis(c.reshape(b_, h_, nb, FOX_BLOCK), 2, 0)
    kpos = jnp.arange(s_)
    scale = FOX_HEAD_DIM ** -0.5

    def block(args):
        qi, ci, i = args
        s = jnp.einsum('bhqd,bhkd->bhqk', qi, k).astype(jnp.float32) * scale
        s = s + ci[..., None] - c[:, :, None, :]
        qpos = i * FOX_BLOCK + jnp.arange(FOX_BLOCK)
        s = jnp.where(kpos[None, :] <= qpos[:, None], s, -jnp.inf)
        p = jax.nn.softmax(s, axis=-1).astype(v.dtype)
        return jnp.einsum('bhqk,bhkd->bhqd', p, v)

    out = lax.map(block, (qb, cb, jnp.arange(nb)))
    return jnp.moveaxis(out, 0, 2).reshape(b_, h_, s_, dh)


def fox_mixer(h, fk, fv, fc, mem_k, mem_v, w_in, w_out):
    q, mq = split_cols(h @ w_in, [FOX_WIDTH, MEM_WIDTH])
    o = merge_heads(fox_attention(split_heads(q, FOX_HEADS), fk, fv, fc))
    m = memory_attention(mq, mem_k, mem_v)
    return jnp.concatenate([o, m], axis=-1) @ w_out


def causal_dwconv(u, w, b):
    s_ = u.shape[1]
    up = jnp.pad(u, ((0, 0), (CONV_WIDTH - 1, 0), (0, 0)))
    return w[0] * up[:, 0:s_] + w[1] * up[:, 1:s_ + 1] + w[2] * up[:, 2:s_ + 2] + b


def conv_ffn(h, w_up, conv_w, conv_b, w_down):
    u = causal_dwconv(h @ w_up, conv_w, conv_b)
    a, val = split_cols(u, [FFN_HIDDEN, FFN_HIDDEN])
    return (jax.nn.silu(a) * val) @ w_down


def setup_inputs(seed: int = 0) -> dict:
    key = jax.random.key(seed)
    ks = jax.random.split(key, 24)

    def nrm(k, shape, scale):
        return jax.random.normal(k, shape, jnp.float32) * scale

    out_scale = (2.0 * DEPTH) ** -0.5
    return {
        'x': nrm(ks[0], (BATCH, SEQ, D_MODEL), 1.0),
        'mem': nrm(ks[1], (BATCH, N_MEM, D_MODEL), 1.0),
        'norm_mix': 1.0 + nrm(ks[2], (DEPTH, D_MODEL), 0.02),
        'norm_ffn': 1.0 + nrm(ks[3], (DEPTH, D_MODEL), 0.02),
        'norm_mem': 1.0 + nrm(ks[4], (D_MODEL,), 0.02),
        'norm_final': 1.0 + nrm(ks[5], (D_MODEL,), 0.02),
        'mem_w_kv': nrm(ks[6], (DEPTH, D_MODEL, 2 * MEM_WIDTH), D_MODEL ** -0.5),
        'gla_w_in': nrm(ks[7], (N_A_LAYERS, D_MODEL, A_IN_WIDTH), D_MODEL ** -0.5),
        'gla_w_gate_up': nrm(ks[8], (N_A_LAYERS, GLA_GATE_RANK, GLA_QK_WIDTH), GLA_GATE_RANK ** -0.5),
        'gla_b_gate': 1.0 + nrm(ks[9], (N_A_LAYERS, GLA_QK_WIDTH), 0.5),
        'gla_norm': 1.0 + nrm(ks[10], (N_A_LAYERS, GLA_V_WIDTH), 0.02),
        'gla_w_out': nrm(ks[11], (N_A_LAYERS, A_MIX_WIDTH, D_MODEL), A_MIX_WIDTH ** -0.5 * out_scale),
        'fox_kv_norm': 1.0 + nrm(ks[12], (D_MODEL,), 0.02),
        'fox_w_kv': nrm(ks[13], (D_MODEL, 2 * FOX_WIDTH + FOX_HEADS), D_MODEL ** -0.5),
        'fox_b_f': 3.0 + nrm(ks[14], (FOX_HEADS,), 1.0),
        'fox_w_in': nrm(ks[15], (N_B_LAYERS, D_MODEL, B_IN_WIDTH), D_MODEL ** -0.5),
        'fox_w_out': nrm(ks[16], (N_B_LAYERS, B_MIX_WIDTH, D_MODEL), B_MIX_WIDTH ** -0.5 * out_scale),
        'ffn_w_up': nrm(ks[17], (DEPTH, D_MODEL, 2 * FFN_HIDDEN), D_MODEL ** -0.5),
        'ffn_conv_w': nrm(ks[18], (DEPTH, CONV_WIDTH, 2 * FFN_HIDDEN), CONV_WIDTH ** -0.5),
        'ffn_conv_b': nrm(ks[19], (DEPTH, 2 * FFN_HIDDEN), 0.01),
        'ffn_w_down': nrm(ks[20], (DEPTH, FFN_HIDDEN, D_MODEL), FFN_HIDDEN ** -0.5 * out_scale),
    }


def reference(x, mem, norm_mix, norm_ffn, norm_mem, norm_final, mem_w_kv, gla_w_in, gla_w_gate_up,
              gla_b_gate, gla_norm, gla_w_out, fox_kv_norm, fox_w_kv, fox_b_f, fox_w_in, fox_w_out,
              ffn_w_up, ffn_conv_w, ffn_conv_b, ffn_w_down):
    mem_n = rmsnorm(mem, norm_mem)
    fk = fv = fc = None
    for i in range(DEPTH):
        if i == N_A_LAYERS:
            fk, fv, fc = fox_shared_kv(x, fox_kv_norm, fox_w_kv, fox_b_f)
        mk, mv = memory_kv(mem_n, mem_w_kv[i])
        h = rmsnorm(x, norm_mix[i])
        if i < N_A_LAYERS:
            x = x + gla_mixer(h, mk, mv, gla_w_in[i], gla_w_gate_up[i], gla_b_gate[i],
                              gla_norm[i], gla_w_out[i])
        else:
            j = i - N_A_LAYERS
            x = x + fox_mixer(h, fk, fv, fc, mk, mv, fox_w_in[j], fox_w_out[j])
        h = rmsnorm(x, norm_ffn[i])
        x = x + conv_ffn(h, ffn_w_up[i], ffn_conv_w[i], ffn_conv_b[i], ffn_w_down[i])
    return rmsnorm(x, norm_final)
```

```python
import functools
import math

import jax
import jax.numpy as jnp
from jax import lax
from jax.experimental import pallas as pl
from jax.experimental.pallas import tpu as pltpu

GLA_HEADS = 4
GLA_GATE_TAU = 16.0
FOX_HEADS = 12
MEM_HEADS = 4
CONV_WIDTH = 3
EPS = 1e-6

LANES = 128
SUBLANES_F32 = 8
SUBLANES_BF16 = 16
VMEM_LIMIT = 56 * 1024 * 1024

BF16 = jnp.bfloat16
F32 = jnp.float32

NT_DIMS = (((1,), (1,)), ((), ()))
TN_DIMS = (((0,), (0,)), ((), ()))


def _params(*sem):
    return pltpu.CompilerParams(dimension_semantics=sem, vmem_limit_bytes=VMEM_LIMIT)


def _rms(x, gain):
    return x * lax.rsqrt(jnp.mean(x * x, axis=-1, keepdims=True) + EPS) * gain


def _dot(a, b):
    return jnp.dot(a, b, preferred_element_type=F32)


def _dot_nt(a, b):
    return lax.dot_general(a, b, NT_DIMS, preferred_element_type=F32)


def _dot_tn(a, b):
    return lax.dot_general(a, b, TN_DIMS, preferred_element_type=F32)


def _split2(x):
    hi = x.astype(BF16)
    lo = (x - hi.astype(F32)).astype(BF16)
    return hi, lo


def _log_sigmoid(x):
    return jnp.minimum(x, 0.0) - jnp.log1p(jnp.exp(-jnp.abs(x)))


def _pick_tile(n, preferred):
    for t in preferred:
        if n % t == 0:
            return t
    return n


def _norm_matmul_kernel(x_ref, g_ref, w_ref, o_ref, h_ref):
    @pl.when(pl.program_id(1) == 0)
    def _():
        h_ref[...] = _rms(x_ref[...], g_ref[...]).astype(BF16)

    o_ref[...] = _dot(h_ref[...], w_ref[...]).astype(o_ref.dtype)


def norm_matmul(x, gain, w, out_dtype, tn):
    m, d = x.shape
    n = w.shape[1]
    tm = _pick_tile(m, (1024, 512, 256))
    return pl.pallas_call(
        _norm_matmul_kernel,
        grid=(m // tm, n // tn),
        in_specs=[
            pl.BlockSpec((tm, d), lambda i, j: (i, 0)),
            pl.BlockSpec((1, d), lambda i, j: (0, 0)),
            pl.BlockSpec((d, tn), lambda i, j: (0, j)),
        ],
        out_specs=pl.BlockSpec((tm, tn), lambda i, j: (i, j)),
        out_shape=jax.ShapeDtypeStruct((m, n), out_dtype),
        scratch_shapes=[pltpu.VMEM((tm, d), BF16)],
        compiler_params=_params("parallel", "arbitrary"),
        name="norm_matmul",
    )(x, gain.reshape(1, d), w)


def _out_proj_kernel(a_ref, b_ref, wa_ref, wb_ref, x_ref, o_ref):
    o_ref[...] = x_ref[...] + _dot(a_ref[...], wa_ref[...]) + _dot(b_ref[...], wb_ref[...])


def out_proj(a, b, wa, wb, x):
    m, d = x.shape
    ka, kb = a.shape[1], b.shape[1]
    tm = _pick_tile(m, (1024, 512, 256))
    tn = _pick_tile(d, (1024, 512, 256))
    return pl.pallas_call(
        _out_proj_kernel,
        grid=(m // tm, d // tn),
        in_specs=[
            pl.BlockSpec((tm, ka), lambda i, j: (i, 0)),
            pl.BlockSpec((tm, kb), lambda i, j: (i, 0)),
            pl.BlockSpec((ka, tn), lambda i, j: (0, j)),
            pl.BlockSpec((kb, tn), lambda i, j: (0, j)),
            pl.BlockSpec((tm, tn), lambda i, j: (i, j)),
        ],
        out_specs=pl.BlockSpec((tm, tn), lambda i, j: (i, j)),
        out_shape=jax.ShapeDtypeStruct((m, d), F32),
        compiler_params=_params("parallel", "arbitrary"),
        name="out_proj",
    )(a, b, wa, wb, x)


HALO = SUBLANES_BF16


def _ffn_kernel(x_ref, xh_ref, g_ref, wa_ref, wv_ref, cwa_ref, cwv_ref, cba_ref, cbv_ref, wd_ref,
                gf_ref, o_ref, h_ref, acc_ref, *, tiles_per_seq, final_norm):
    i = pl.program_id(0)
    f = pl.program_id(1)
    tm = x_ref.shape[0]

    @pl.when(f == 0)
    def _():
        h_ref[0:tm, :] = _rms(x_ref[...], g_ref[...]).astype(BF16)
        prev = _rms(xh_ref[...], g_ref[...])
        prev = jnp.where(i % tiles_per_seq == 0, 0.0, prev)
        h_ref[tm:tm + HALO, :] = prev.astype(BF16)
        acc_ref[...] = jnp.zeros_like(acc_ref)

    h = h_ref[...]

    def conv(u, cw_ref, cb_ref):
        cw = cw_ref[...]
        y = cw[2:3, :] * u + cw[1:2, :] * pltpu.roll(u, 1, axis=0) + cw[0:1, :] * pltpu.roll(u, 2, axis=0)
        return y[0:tm, :] + cb_ref[...]

    ya = conv(_dot(h, wa_ref[...]), cwa_ref, cba_ref)
    yv = conv(_dot(h, wv_ref[...]), cwv_ref, cbv_ref)
    act = (ya * jax.nn.sigmoid(ya) * yv).astype(BF16)
    acc_ref[...] += _dot(act, wd_ref[...])

    @pl.when(f == pl.num_programs(1) - 1)
    def _():
        y = x_ref[...] + acc_ref[...]
        if final_norm:
            y = _rms(y, gf_ref[...])
        o_ref[...] = y


def conv_ffn(x, gain, w_up, conv_w, conv_b, w_down, final_gain, seq_len, final_norm):
    m, d = x.shape
    hidden = w_down.shape[0]
    tm = _pick_tile(seq_len, (512, 256))
    tf = _pick_tile(hidden, (512, 256, 128))
    nf = hidden // tf
    halo_blocks = tm // HALO
    kern = functools.partial(_ffn_kernel, tiles_per_seq=seq_len // tm, final_norm=final_norm)
    return pl.pallas_call(
        kern,
        grid=(m // tm, nf),
        in_specs=[
            pl.BlockSpec((tm, d), lambda i, f: (i, 0)),
            pl.BlockSpec((HALO, d), lambda i, f: (jnp.maximum(i * halo_blocks - 1, 0), 0)),
            pl.BlockSpec((1, d), lambda i, f: (0, 0)),
            pl.BlockSpec((d, tf), lambda i, f: (0, f)),
            pl.BlockSpec((d, tf), lambda i, f: (0, nf + f)),
            pl.BlockSpec((CONV_WIDTH, tf), lambda i, f: (0, f)),
            pl.BlockSpec((CONV_WIDTH, tf), lambda i, f: (0, nf + f)),
            pl.BlockSpec((1, tf), lambda i, f: (0, f)),
            pl.BlockSpec((1, tf), lambda i, f: (0, nf + f)),
            pl.BlockSpec((tf, d), lambda i, f: (f, 0)),
            pl.BlockSpec((1, d), lambda i, f: (0, 0)),
        ],
        out_specs=pl.BlockSpec((tm, d), lambda i, f: (i, 0)),
        out_shape=jax.ShapeDtypeStruct((m, d), F32),
        scratch_shapes=[pltpu.VMEM((tm + HALO, d), BF16), pltpu.VMEM((tm, d), F32)],
        compiler_params=_params("parallel", "arbitrary"),
        name="conv_ffn",
    )(x, x, gain.reshape(1, d), w_up, w_up, conv_w, conv_w, conv_b.reshape(1, -1),
      conv_b.reshape(1, -1), w_down, final_gain.reshape(1, d))


def _mem_attn_kernel(q_ref, mk_ref, mv_ref, o_ref, *, heads):
    dh = q_ref.shape[1] // heads
    scale = dh ** -0.5
    for hd in range(heads):
        cols = slice(hd * dh, (hd + 1) * dh)
        q = q_ref[:, cols].astype(BF16)
        s = _dot_nt(q, mk_ref[:, cols]) * scale
        s = s - jnp.max(s, axis=-1, keepdims=True)
        p = jnp.exp(s)
        p = p / jnp.sum(p, axis=-1, keepdims=True)
        o_ref[:, cols] = _dot(p.astype(BF16), mv_ref[:, cols]).astype(o_ref.dtype)


def mem_attention(z, q_col_block, mem_kv, layer, batch, seq_len, n_mem):
    m = z.shape[0]
    mem_width = MEM_HEADS * LANES
    t = _pick_tile(seq_len, (1024, 512, 256))
    tiles = seq_len // t
    kern = functools.partial(_mem_attn_kernel, heads=MEM_HEADS)
    return pl.pallas_call(
        kern,
        grid=(batch, tiles),
        in_specs=[
            pl.BlockSpec((t, mem_width), lambda b, s: (b * tiles + s, q_col_block)),
            pl.BlockSpec((n_mem, mem_width), lambda b, s: (b, 2 * layer)),
            pl.BlockSpec((n_mem, mem_width), lambda b, s: (b, 2 * layer + 1)),
        ],
        out_specs=pl.BlockSpec((t, mem_width), lambda b, s: (b * tiles + s, 0)),
        out_shape=jax.ShapeDtypeStruct((m, mem_width), BF16),
        compiler_params=_params("parallel", "arbitrary"),
        name="mem_attention",
    )(z, mem_kv, mem_kv)


GLA_BLOCK = 256


def _ref_rows(bc, half):
    rows, dk = bc.shape
    if half >= SUBLANES_F32:
        nb = rows // (2 * half)
        b3 = bc.reshape(nb, 2 * half, dk)
        return jnp.broadcast_to(b3[:, half:half + 1, :], b3.shape).reshape(rows, dk)
    b3 = bc.reshape(rows // SUBLANES_F32, SUBLANES_F32, dk)
    sub = lax.broadcasted_iota(jnp.int32, b3.shape, 1)
    out = None
    for mid in range(half, SUBLANES_F32, 2 * half):
        cand = jnp.broadcast_to(b3[:, mid:mid + 1, :], b3.shape)
        out = cand if out is None else jnp.where(sub >= mid - half, cand, out)
    return out.reshape(rows, dk)


def _gla_kernel(q_ref, k_ref, v_ref, glr_ref, og_ref, wg_ref, bg_ref, gn_ref, o_ref, state_ref, *, rank):
    t = pl.program_id(2)
    rows = q_ref.shape[0]
    dk = q_ref.shape[1]
    blk = GLA_BLOCK if rows % GLA_BLOCK == 0 else rows
    n_levels = int(math.log2(blk))

    @pl.when(t == 0)
    def _():
        state_ref[...] = jnp.zeros_like(state_ref)

    ti = lax.broadcasted_iota(jnp.int32, (blk, blk), 0)
    si = lax.broadcasted_iota(jnp.int32, (blk, blk), 1)
    tri = jnp.where(si <= ti, 1.0, 0.0).astype(BF16)
    x = ti ^ si
    top_bit = 31 - lax.clz(jnp.maximum(x, 1))
    pair_level = jnp.where(si < ti, top_bit, jnp.where(si == ti, -1, -2))
    row = lax.broadcasted_iota(jnp.int32, (blk, dk), 0)
    qscale = dk ** -0.5

    def step(i, carry):
        r0 = pl.multiple_of(i * blk, blk)
        rs = pl.ds(r0, blk)
        q = q_ref[rs, :] * qscale
        k = k_ref[rs, :]
        v = v_ref[rs, :].astype(BF16)
        glr = glr_ref[rs, 0:rank].astype(BF16)
        g = _log_sigmoid(_dot(glr, wg_ref[...]) + bg_ref[...]) * (1.0 / GLA_GATE_TAU)
        g_hi, g_lo = _split2(g)
        bc = _dot(tri, g_hi) + _dot(tri, g_lo)
        b_last = bc[blk - 1:blk, :]
        state = state_ref[...]

        o = _dot_nt((q * jnp.exp(bc)).astype(BF16), state.astype(BF16))

        att = jnp.zeros((blk, blk), F32)
        for p in range(n_levels):
            half = 1 << p
            e = jnp.exp(-jnp.abs(bc - _ref_rows(bc, half)))
            upper = ((row >> p) & 1) == 1
            ql = jnp.where(upper, q * e, 0.0).astype(BF16)
            kl = jnp.where(upper, 0.0, k * e).astype(BF16)
            att = jnp.where(pair_level == p, _dot_nt(ql, kl), att)
        att = jnp.where(pair_level == -1, _dot_nt(q.astype(BF16), k.astype(BF16)), att)
        o = o + _dot(att.astype(BF16), v)

        kd = (k * jnp.exp(b_last - bc)).astype(BF16)
        state_ref[...] = state * jnp.exp(b_last) + _dot_tn(v, kd)

        o = o * lax.rsqrt(jnp.mean(o * o, axis=-1, keepdims=True) + EPS) * gn_ref[...]
        og = og_ref[rs, :]
        o_ref[rs, :] = (o * (og * jax.nn.sigmoid(og))).astype(o_ref.dtype)
        return carry

    lax.fori_loop(0, rows // blk, step, 0)


def gla_attention(z, w_gate, b_gate, gn_gain, batch, seq_len, dk, dv, rank):
    m = z.shape[0]
    heads = GLA_HEADS
    t = _pick_tile(seq_len, (1024, 512, 256))
    tiles = seq_len // t
    v0, og0 = 0, (heads * dv) // dv
    q0 = (2 * heads * dv) // dk
    k0 = q0 + heads
    glr0 = (2 * heads * dv + 2 * heads * dk + MEM_HEADS * LANES) // LANES
    rows = lambda b, h, s: b * tiles + s
    kern = functools.partial(_gla_kernel, rank=rank)
    return pl.pallas_call(
        kern,
        grid=(batch, heads, tiles),
        in_specs=[
            pl.BlockSpec((t, dk), lambda b, h, s: (rows(b, h, s), q0 + h)),
            pl.BlockSpec((t, dk), lambda b, h, s: (rows(b, h, s), k0 + h)),
            pl.BlockSpec((t, dv), lambda b, h, s: (rows(b, h, s), v0 + h)),
            pl.BlockSpec((t, LANES), lambda b, h, s: (rows(b, h, s), glr0)),
            pl.BlockSpec((t, dv), lambda b, h, s: (rows(b, h, s), og0 + h)),
            pl.BlockSpec((None, rank, dk), lambda b, h, s: (h, 0, 0)),
            pl.BlockSpec((None, 1, dk), lambda b, h, s: (h, 0, 0)),
            pl.BlockSpec((None, 1, dv), lambda b, h, s: (h, 0, 0)),
        ],
        out_specs=pl.BlockSpec((t, dv), lambda b, h, s: (rows(b, h, s), h)),
        out_shape=jax.ShapeDtypeStruct((m, heads * dv), BF16),
        scratch_shapes=[pltpu.VMEM((dv, dk), F32)],
        compiler_params=_params("parallel", "parallel", "arbitrary"),
        name="gla_attention",
    )(z, z, z, z, z, w_gate, b_gate, gn_gain)


GATE_ROWS = 16


def _fox_gate_kernel(x_ref, g_ref, w_ref, b_ref, o_ref, carry_ref):
    s = pl.program_id(1)
    t = x_ref.shape[0]

    @pl.when(s == 0)
    def _():
        carry_ref[...] = jnp.zeros_like(carry_ref)

    h_hi, h_lo = _split2(_rms(x_ref[...], g_ref[...]))
    w = w_ref[...]
    log_f = _log_sigmoid(_dot(h_hi, w) + _dot(h_lo, w) + b_ref[...])
    ti = lax.broadcasted_iota(jnp.int32, (t, t), 0)
    si = lax.broadcasted_iota(jnp.int32, (t, t), 1)
    tri = jnp.where(si <= ti, 1.0, 0.0).astype(BF16)
    f_hi = log_f.astype(BF16)
    r1 = log_f - f_hi.astype(F32)
    f_mid = r1.astype(BF16)
    f_lo = (r1 - f_mid.astype(F32)).astype(BF16)
    c = carry_ref[...] + _dot(tri, f_hi) + _dot(tri, f_mid) + _dot(tri, f_lo)
    carry_ref[...] = c[t - 1:t, :]
    o_ref[...] = jnp.transpose(c)[0:GATE_ROWS, :]


def fox_gates(x, gain, w_f, b_f, batch, seq_len):
    m, d = x.shape
    t = _pick_tile(seq_len, (512, 256))
    tiles = seq_len // t
    return pl.pallas_call(
        _fox_gate_kernel,
        grid=(batch, tiles),
        in_specs=[
            pl.BlockSpec((t, d), lambda b, s: (b * tiles + s, 0)),
            pl.BlockSpec((1, d), lambda b, s: (0, 0)),
            pl.BlockSpec((d, LANES), lambda b, s: (0, 0)),
            pl.BlockSpec((1, LANES), lambda b, s: (0, 0)),
        ],
        out_specs=pl.BlockSpec((None, GATE_ROWS, t), lambda b, s: (b, 0, s)),
        out_shape=jax.ShapeDtypeStruct((batch, GATE_ROWS, seq_len), F32),
        scratch_shapes=[pltpu.VMEM((1, LANES), F32)],
        compiler_params=_params("parallel", "arbitrary"),
        name="fox_gates",
    )(x, gain.reshape(1, d), w_f, b_f)


def _fox_attn_kernel(q_ref, k_ref, v_ref, c_ref, o_ref):
    qi = pl.program_id(2)
    tq, dh = q_ref.shape
    q = (q_ref[...] * dh ** -0.5).astype(BF16)

    def scores(j):
        ks = pl.ds(pl.multiple_of(j * tq, tq), tq)
        return _dot_nt(q, k_ref[ks, :]) - c_ref[:, ks], ks

    def update(s, ks, carry):
        m_prev, l_prev, acc = carry
        m_new = jnp.maximum(m_prev, jnp.max(s, axis=-1, keepdims=True))
        alpha = jnp.exp(m_prev - m_new)
        p = jnp.exp(s - m_new)
        l_new = alpha * l_prev + jnp.sum(p, axis=-1, keepdims=True)
        acc = alpha * acc + _dot(p.astype(BF16), v_ref[ks, :])
        return m_new, l_new, acc

    def body(j, carry):
        s, ks = scores(j)
        return update(s, ks, carry)

    init = (jnp.full((tq, 1), -jnp.inf, F32), jnp.zeros((tq, 1), F32), jnp.zeros((tq, dh), F32))
    carry = lax.fori_loop(0, qi, body, init)
    s, ks = scores(qi)
    ti = lax.broadcasted_iota(jnp.int32, (tq, tq), 0)
    si = lax.broadcasted_iota(jnp.int32, (tq, tq), 1)
    s = jnp.where(si <= ti, s, -jnp.inf)
    _, l_fin, acc = update(s, ks, carry)
    o_ref[...] = (acc / l_fin).astype(o_ref.dtype)


def fox_attention(zq, kv, c, batch, seq_len):
    m = zq.shape[0]
    dh = LANES
    tq = _pick_tile(seq_len, (512, 256))
    tiles = seq_len // tq
    return pl.pallas_call(
        _fox_attn_kernel,
        grid=(batch, FOX_HEADS, tiles),
        in_specs=[
            pl.BlockSpec((tq, dh), lambda b, h, s: (b * tiles + s, h)),
            pl.BlockSpec((seq_len, dh), lambda b, h, s: (b, h)),
            pl.BlockSpec((seq_len, dh), lambda b, h, s: (b, FOX_HEADS + h)),
            pl.BlockSpec((None, None, 1, seq_len), lambda b, h, s: (b, h, 0, 0)),
        ],
        out_specs=pl.BlockSpec((tq, dh), lambda b, h, s: (b * tiles + s, h)),
        out_shape=jax.ShapeDtypeStruct((m, FOX_HEADS * dh), BF16),
        compiler_params=_params("parallel", "parallel", "arbitrary"),
        name="fox_attention",
    )(zq, kv, kv, c)


def kernel(x, mem, norm_mix, norm_ffn, norm_mem, norm_final, mem_w_kv, gla_w_in, gla_w_gate_up,
           gla_b_gate, gla_norm, gla_w_out, fox_kv_norm, fox_w_kv, fox_b_f, fox_w_in, fox_w_out,
           ffn_w_up, ffn_conv_w, ffn_conv_b, ffn_w_down):
    batch, seq_len, d = x.shape
    n_mem = mem.shape[1]
    depth = norm_mix.shape[0]
    n_gla = gla_w_in.shape[0]
    rank = gla_w_gate_up.shape[1]
    qk_width = gla_w_gate_up.shape[2]
    dk = qk_width // GLA_HEADS
    v_width = gla_norm.shape[1]
    dv = v_width // GLA_HEADS
    mem_width = mem_w_kv.shape[2] // 2
    fox_width = (fox_w_kv.shape[1] - FOX_HEADS) // 2
    assert mem_width == MEM_HEADS * LANES and fox_width == FOX_HEADS * LANES
    assert (2 * v_width) % dk == 0 and rank <= LANES

    xf = x.reshape(batch * seq_len, d)

    w_mem = jnp.transpose(mem_w_kv, (1, 0, 2)).reshape(d, depth * 2 * mem_width).astype(BF16)
    mem_kv = norm_matmul(mem.reshape(batch * n_mem, d), norm_mem, w_mem, BF16,
                         _pick_tile(w_mem.shape[1], (1024, 512)))

    fox_kv = fox_c = None
    for i in range(depth):
        if i == n_gla:
            fox_kv = norm_matmul(xf, fox_kv_norm, fox_w_kv[:, :2 * fox_width].astype(BF16), BF16,
                                 _pick_tile(2 * fox_width, (1024, 768, 512)))
            w_f = jnp.pad(fox_w_kv[:, 2 * fox_width:], ((0, 0), (0, LANES - FOX_HEADS))).astype(BF16)
            b_f = jnp.pad(fox_b_f, (0, LANES - FOX_HEADS)).reshape(1, LANES)
            fox_c = fox_gates(xf, fox_kv_norm, w_f, b_f, batch, seq_len)
            fox_c = fox_c.reshape(batch, GATE_ROWS, 1, seq_len)
        if i < n_gla:
            w = gla_w_in[i]
            o_q, o_k, o_v = 0, qk_width, 2 * qk_width
            o_glr = o_v + v_width
            o_og = o_glr + rank
            o_mq = o_og + v_width
            w_in = jnp.concatenate([
                w[:, o_v:o_glr], w[:, o_og:o_mq], w[:, o_q:o_k], w[:, o_k:o_v], w[:, o_mq:],
                jnp.pad(w[:, o_glr:o_og], ((0, 0), (0, LANES - rank)))], axis=1).astype(BF16)
            z = norm_matmul(xf, norm_mix[i], w_in, F32, _pick_tile(w_in.shape[1], (1152, 640, 384, 128)))
            a = gla_attention(z, gla_w_gate_up[i].reshape(rank, GLA_HEADS, dk).transpose(1, 0, 2).astype(BF16),
                              gla_b_gate[i].reshape(GLA_HEADS, 1, dk), gla_norm[i].reshape(GLA_HEADS, 1, dv),
                              batch, seq_len, dk, dv, rank)
            mq_block = (2 * v_width + 2 * qk_width) // mem_width
            w_out = gla_w_out[i].astype(BF16)
            split = v_width
        else:
            j = i - n_gla
            z = norm_matmul(xf, norm_mix[i], fox_w_in[j].astype(BF16), F32,
                            _pick_tile(fox_w_in.shape[2], (1024, 512)))
            a = fox_attention(z, fox_kv, fox_c, batch, seq_len)
            mq_block = fox_width // mem_width
            w_out = fox_w_out[j].astype(BF16)
            split = fox_width
        mo = mem_attention(z, mq_block, mem_kv, i, batch, seq_len, n_mem)
        xf = out_proj(a, mo, w_out[:split], w_out[split:], xf)
        xf = conv_ffn(xf, norm_ffn[i], ffn_w_up[i].astype(BF16), ffn_conv_w[i], ffn_conv_b[i],
                      ffn_w_down[i].astype(BF16), norm_final, seq_len, final_norm=(i == depth - 1))
    return xf.reshape(batch, seq_len, d)
```

```python
import functools
import math

import jax
import jax.numpy as jnp
from jax import lax
from jax.experimental import pallas as pl
from jax.experimental.pallas import tpu as pltpu

GLA_HEADS = 4
GLA_GATE_TAU = 16.0
FOX_HEADS = 12
MEM_HEADS = 4
CONV_WIDTH = 3
EPS = 1e-6

LANES = 128
SUBLANES_F32 = 8
SUBLANES_BF16 = 16
VMEM_LIMIT = 56 * 1024 * 1024

BF16 = jnp.bfloat16
F32 = jnp.float32

NT_DIMS = (((1,), (1,)), ((), ()))
TN_DIMS = (((0,), (0,)), ((), ()))


def _params(*sem):
    return pltpu.CompilerParams(dimension_semantics=sem, vmem_limit_bytes=VMEM_LIMIT)


def _rms(x, gain):
    return x * lax.rsqrt(jnp.mean(x * x, axis=-1, keepdims=True) + EPS) * gain


def _dot(a, b):
    return jnp.dot(a, b, preferred_element_type=F32)


def _dot_nt(a, b):
    return lax.dot_general(a, b, NT_DIMS, preferred_element_type=F32)


def _dot_tn(a, b):
    return lax.dot_general(a, b, TN_DIMS, preferred_element_type=F32)


def _split2(x):
    hi = x.astype(BF16)
    lo = (x - hi.astype(F32)).astype(BF16)
    return hi, lo


def _log_sigmoid(x):
    return jnp.minimum(x, 0.0) - jnp.log1p(jnp.exp(-jnp.abs(x)))


def _pick_tile(n, preferred):
    for t in preferred:
        if n % t == 0:
            return t
    return n


def _norm_matmul_kernel(x_ref, g_ref, w_ref, o_ref, h_ref):
    @pl.when(pl.program_id(1) == 0)
    def _():
        h_ref[...] = _rms(x_ref[...], g_ref[...]).astype(BF16)

    o_ref[...] = _dot(h_ref[...], w_ref[...]).astype(o_ref.dtype)


def norm_matmul(x, gain, w, out_dtype, tn):
    m, d = x.shape
    n = w.shape[1]
    tm = _pick_tile(m, (1024, 512, 256))
    return pl.pallas_call(
        _norm_matmul_kernel,
        grid=(m // tm, n // tn),
        in_specs=[
            pl.BlockSpec((tm, d), lambda i, j: (i, 0)),
            pl.BlockSpec((1, d), lambda i, j: (0, 0)),
            pl.BlockSpec((d, tn), lambda i, j: (0, j)),
        ],
        out_specs=pl.BlockSpec((tm, tn), lambda i, j: (i, j)),
        out_shape=jax.ShapeDtypeStruct((m, n), out_dtype),
        scratch_shapes=[pltpu.VMEM((tm, d), BF16)],
        compiler_params=_params("parallel", "arbitrary"),
        name="norm_matmul",
    )(x, gain.reshape(1, d), w)


def _out_proj_kernel(a_ref, b_ref, wa_ref, wb_ref, x_ref, o_ref):
    o_ref[...] = x_ref[...] + _dot(a_ref[...], wa_ref[...]) + _dot(b_ref[...], wb_ref[...])


def out_proj(a, b, wa, wb, x):
    m, d = x.shape
    ka, kb = a.shape[1], b.shape[1]
    tm = _pick_tile(m, (1024, 512, 256))
    tn = _pick_tile(d, (1024, 512, 256))
    return pl.pallas_call(
        _out_proj_kernel,
        grid=(m // tm, d // tn),
        in_specs=[
            pl.BlockSpec((tm, ka), lambda i, j: (i, 0)),
            pl.BlockSpec((tm, kb), lambda i, j: (i, 0)),
            pl.BlockSpec((ka, tn), lambda i, j: (0, j)),
            pl.BlockSpec((kb, tn), lambda i, j: (0, j)),
            pl.BlockSpec((tm, tn), lambda i, j: (i, j)),
        ],
        out_specs=pl.BlockSpec((tm, tn), lambda i, j: (i, j)),
        out_shape=jax.ShapeDtypeStruct((m, d), F32),
        compiler_params=_params("parallel", "arbitrary"),
        name="out_proj",
    )(a, b, wa, wb, x)


HALO = SUBLANES_BF16


def _ffn_kernel(x_ref, xh_ref, g_ref, wa_ref, wv_ref, cwa_ref, cwv_ref, cba_ref, cbv_ref, wd_ref,
                gf_ref, o_ref, h_ref, *, tiles_per_seq, final_norm):
    i = pl.program_id(0)
    f = pl.program_id(1)
    tm = x_ref.shape[0]

    @pl.when(f == 0)
    def _():
        x = x_ref[...]
        h_ref[0:tm, :] = _rms(x, g_ref[...]).astype(BF16)
        prev = _rms(xh_ref[...], g_ref[...])
        prev = jnp.where(i % tiles_per_seq == 0, 0.0, prev)
        h_ref[tm:tm + HALO, :] = prev.astype(BF16)
        o_ref[...] = x

    h = h_ref[...]

    def conv(u, cw_ref, cb_ref):
        cw = cw_ref[...]
        y = cw[2:3, :] * u + cw[1:2, :] * pltpu.roll(u, 1, axis=0) + cw[0:1, :] * pltpu.roll(u, 2, axis=0)
        return y[0:tm, :] + cb_ref[...]

    ya = conv(_dot(h, wa_ref[...]), cwa_ref, cba_ref)
    yv = conv(_dot(h, wv_ref[...]), cwv_ref, cbv_ref)
    act = (ya * jax.nn.sigmoid(ya) * yv).astype(BF16)
    o_ref[...] += _dot(act, wd_ref[...])

    if final_norm:
        @pl.when(f == pl.num_programs(1) - 1)
        def _():
            o_ref[...] = _rms(o_ref[...], gf_ref[...])


def conv_ffn(x, gain, w_up, conv_w, conv_b, w_down, final_gain, seq_len, final_norm):
    m, d = x.shape
    hidden = w_down.shape[0]
    tm = _pick_tile(seq_len, (1024, 512, 256))
    tf = _pick_tile(hidden, (512, 256, 128))
    nf = hidden // tf
    halo_blocks = tm // HALO
    kern = functools.partial(_ffn_kernel, tiles_per_seq=seq_len // tm, final_norm=final_norm)
    return pl.pallas_call(
        kern,
        grid=(m // tm, nf),
        in_specs=[
            pl.BlockSpec((tm, d), lambda i, f: (i, 0), pipeline_mode=pl.Buffered(1)),
            pl.BlockSpec((HALO, d), lambda i, f: (jnp.maximum(i * halo_blocks - 1, 0), 0)),
            pl.BlockSpec((1, d), lambda i, f: (0, 0)),
            pl.BlockSpec((d, tf), lambda i, f: (0, f)),
            pl.BlockSpec((d, tf), lambda i, f: (0, nf + f)),
            pl.BlockSpec((CONV_WIDTH, tf), lambda i, f: (0, f)),
            pl.BlockSpec((CONV_WIDTH, tf), lambda i, f: (0, nf + f)),
            pl.BlockSpec((1, tf), lambda i, f: (0, f)),
            pl.BlockSpec((1, tf), lambda i, f: (0, nf + f)),
            pl.BlockSpec((tf, d), lambda i, f: (f, 0)),
            pl.BlockSpec((1, d), lambda i, f: (0, 0)),
        ],
        out_specs=pl.BlockSpec((tm, d), lambda i, f: (i, 0)),
        out_shape=jax.ShapeDtypeStruct((m, d), F32),
        scratch_shapes=[pltpu.VMEM((tm + HALO, d), BF16)],
        compiler_params=_params("parallel", "arbitrary"),
        name="conv_ffn",
    )(x, x, gain.reshape(1, d), w_up, w_up, conv_w, conv_w, conv_b.reshape(1, -1),
      conv_b.reshape(1, -1), w_down, final_gain.reshape(1, d))


def _mem_attn_kernel(q_ref, mk_ref, mv_ref, o_ref, *, heads):
    dh = q_ref.shape[1] // heads
    scale = dh ** -0.5
    for hd in range(heads):
        cols = slice(hd * dh, (hd + 1) * dh)
        q = q_ref[:, cols].astype(BF16)
        s = _dot_nt(q, mk_ref[:, cols]) * scale
        s = s - jnp.max(s, axis=-1, keepdims=True)
        p = jnp.exp(s)
        p = p / jnp.sum(p, axis=-1, keepdims=True)
        o_ref[:, cols] = _dot(p.astype(BF16), mv_ref[:, cols]).astype(o_ref.dtype)


def mem_attention(z, q_col_block, mem_kv, layer, batch, seq_len, n_mem):
    m = z.shape[0]
    mem_width = MEM_HEADS * LANES
    t = _pick_tile(seq_len, (1024, 512, 256))
    tiles = seq_len // t
    kern = functools.partial(_mem_attn_kernel, heads=MEM_HEADS)
    return pl.pallas_call(
        kern,
        grid=(batch, tiles),
        in_specs=[
            pl.BlockSpec((t, mem_width), lambda b, s: (b * tiles + s, q_col_block)),
            pl.BlockSpec((n_mem, mem_width), lambda b, s: (b, 2 * layer)),
            pl.BlockSpec((n_mem, mem_width), lambda b, s: (b, 2 * layer + 1)),
        ],
        out_specs=pl.BlockSpec((t, mem_width), lambda b, s: (b * tiles + s, 0)),
        out_shape=jax.ShapeDtypeStruct((m, mem_width), BF16),
        compiler_params=_params("parallel", "arbitrary"),
        name="mem_attention",
    )(z, mem_kv, mem_kv)


GLA_BLOCK = 256


def _ref_rows(bc, half):
    rows, dk = bc.shape
    if half >= SUBLANES_F32:
        nb = rows // (2 * half)
        b3 = bc.reshape(nb, 2 * half, dk)
        return jnp.broadcast_to(b3[:, half:half + 1, :], b3.shape).reshape(rows, dk)
    b3 = bc.reshape(rows // SUBLANES_F32, SUBLANES_F32, dk)
    sub = lax.broadcasted_iota(jnp.int32, b3.shape, 1)
    out = None
    for mid in range(half, SUBLANES_F32, 2 * half):
        cand = jnp.broadcast_to(b3[:, mid:mid + 1, :], b3.shape)
        out = cand if out is None else jnp.where(sub >= mid - half, cand, out)
    return out.reshape(rows, dk)


def _gla_kernel(q_ref, k_ref, v_ref, glr_ref, og_ref, wg_ref, bg_ref, gn_ref, o_ref, state_ref, *, rank):
    t = pl.program_id(2)
    rows = q_ref.shape[0]
    dk = q_ref.shape[1]
    blk = GLA_BLOCK if rows % GLA_BLOCK == 0 else rows
    n_levels = int(math.log2(blk))

    @pl.when(t == 0)
    def _():
        state_ref[...] = jnp.zeros_like(state_ref)

    ti = lax.broadcasted_iota(jnp.int32, (blk, blk), 0)
    si = lax.broadcasted_iota(jnp.int32, (blk, blk), 1)
    tri = jnp.where(si <= ti, 1.0, 0.0).astype(BF16)
    x = ti ^ si
    top_bit = 31 - lax.clz(jnp.maximum(x, 1))
    pair_level = jnp.where(si < ti, top_bit, jnp.where(si == ti, -1, -2))
    row = lax.broadcasted_iota(jnp.int32, (blk, dk), 0)
    qscale = dk ** -0.5

    def step(i, carry):
        r0 = pl.multiple_of(i * blk, blk)
        rs = pl.ds(r0, blk)
        q = q_ref[rs, :].astype(F32) * qscale
        k = k_ref[rs, :].astype(F32)
        v = v_ref[rs, :].astype(BF16)
        glr = glr_ref[rs, 0:rank].astype(BF16)
        g = _log_sigmoid(_dot(glr, wg_ref[...]) + bg_ref[...]) * (1.0 / GLA_GATE_TAU)
        g_hi, g_lo = _split2(g)
        bc = _dot(tri, g_hi) + _dot(tri, g_lo)
        b_last = bc[blk - 1:blk, :]
        state = state_ref[...]

        o = _dot_nt((q * jnp.exp(bc)).astype(BF16), state.astype(BF16))

        att = jnp.zeros((blk, blk), F32)
        for p in range(n_levels):
            half = 1 << p
            e = jnp.exp(-jnp.abs(bc - _ref_rows(bc, half)))
            upper = ((row >> p) & 1) == 1
            ql = jnp.where(upper, q * e, 0.0).astype(BF16)
            kl = jnp.where(upper, 0.0, k * e).astype(BF16)
            att = jnp.where(pair_level == p, _dot_nt(ql, kl), att)
        att = jnp.where(pair_level == -1, _dot_nt(q.astype(BF16), k.astype(BF16)), att)
        o = o + _dot(att.astype(BF16), v)

        kd = (k * jnp.exp(b_last - bc)).astype(BF16)
        state_ref[...] = state * jnp.exp(b_last) + _dot_tn(v, kd)

        o = o * lax.rsqrt(jnp.mean(o * o, axis=-1, keepdims=True) + EPS) * gn_ref[...]
        og = og_ref[rs, :].astype(F32)
        o_ref[rs, :] = (o * (og * jax.nn.sigmoid(og))).astype(o_ref.dtype)
        return carry

    lax.fori_loop(0, rows // blk, step, 0)


def gla_attention(z, w_gate, b_gate, gn_gain, batch, seq_len, dk, dv, rank):
    m = z.shape[0]
    heads = GLA_HEADS
    t = _pick_tile(seq_len, (1024, 512, 256))
    tiles = seq_len // t
    v0, og0 = 0, (heads * dv) // dv
    q0 = (2 * heads * dv) // dk
    k0 = q0 + heads
    glr0 = (2 * heads * dv + 2 * heads * dk + MEM_HEADS * LANES) // LANES
    rows = lambda b, h, s: b * tiles + s
    kern = functools.partial(_gla_kernel, rank=rank)
    return pl.pallas_call(
        kern,
        grid=(batch, heads, tiles),
        in_specs=[
            pl.BlockSpec((t, dk), lambda b, h, s: (rows(b, h, s), q0 + h)),
            pl.BlockSpec((t, dk), lambda b, h, s: (rows(b, h, s), k0 + h)),
            pl.BlockSpec((t, dv), lambda b, h, s: (rows(b, h, s), v0 + h)),
            pl.BlockSpec((t, LANES), lambda b, h, s: (rows(b, h, s), glr0)),
            pl.BlockSpec((t, dv), lambda b, h, s: (rows(b, h, s), og0 + h)),
            pl.BlockSpec((None, rank, dk), lambda b, h, s: (h, 0, 0)),
            pl.BlockSpec((None, 1, dk), lambda b, h, s: (h, 0, 0)),
            pl.BlockSpec((None, 1, dv), lambda b, h, s: (h, 0, 0)),
        ],
        out_specs=pl.BlockSpec((t, dv), lambda b, h, s: (rows(b, h, s), h)),
        out_shape=jax.ShapeDtypeStruct((m, heads * dv), BF16),
        scratch_shapes=[pltpu.VMEM((dv, dk), F32)],
        compiler_params=_params("parallel", "parallel", "arbitrary"),
        name="gla_attention",
    )(z, z, z, z, z, w_gate, b_gate, gn_gain)


GATE_ROWS = 16


def _fox_gate_kernel(x_ref, g_ref, w_ref, b_ref, o_ref, carry_ref):
    s = pl.program_id(1)
    t = x_ref.shape[0]

    @pl.when(s == 0)
    def _():
        carry_ref[...] = jnp.zeros_like(carry_ref)

    h_hi, h_lo = _split2(_rms(x_ref[...], g_ref[...]))
    w = w_ref[...]
    log_f = _log_sigmoid(_dot(h_hi, w) + _dot(h_lo, w) + b_ref[...])
    ti = lax.broadcasted_iota(jnp.int32, (t, t), 0)
    si = lax.broadcasted_iota(jnp.int32, (t, t), 1)
    tri = jnp.where(si <= ti, 1.0, 0.0).astype(BF16)
    f_hi = log_f.astype(BF16)
    r1 = log_f - f_hi.astype(F32)
    f_mid = r1.astype(BF16)
    f_lo = (r1 - f_mid.astype(F32)).astype(BF16)
    c = carry_ref[...] + _dot(tri, f_hi) + _dot(tri, f_mid) + _dot(tri, f_lo)
    carry_ref[...] = c[t - 1:t, :]
    o_ref[...] = jnp.transpose(c)[0:GATE_ROWS, :]


def fox_gates(x, gain, w_f, b_f, batch, seq_len):
    m, d = x.shape
    t = _pick_tile(seq_len, (512, 256))
    tiles = seq_len // t
    return pl.pallas_call(
        _fox_gate_kernel,
        grid=(batch, tiles),
        in_specs=[
            pl.BlockSpec((t, d), lambda b, s: (b * tiles + s, 0)),
            pl.BlockSpec((1, d), lambda b, s: (0, 0)),
            pl.BlockSpec((d, LANES), lambda b, s: (0, 0)),
            pl.BlockSpec((1, LANES), lambda b, s: (0, 0)),
        ],
        out_specs=pl.BlockSpec((None, GATE_ROWS, t), lambda b, s: (b, 0, s)),
        out_shape=jax.ShapeDtypeStruct((batch, GATE_ROWS, seq_len), F32),
        scratch_shapes=[pltpu.VMEM((1, LANES), F32)],
        compiler_params=_params("parallel", "arbitrary"),
        name="fox_gates",
    )(x, gain.reshape(1, d), w_f, b_f)


FOX_HEADS_PER_STEP = 2
LOG2_E = math.log2(math.e)


def _fox_attn_kernel(q_ref, k_ref, v_ref, c_ref, o_ref, *, heads):
    qi = pl.program_id(2)
    tq = q_ref.shape[0]
    dh = q_ref.shape[1] // heads
    cols = [slice(hd * dh, (hd + 1) * dh) for hd in range(heads)]
    qs = [(q_ref[:, cols[hd]].astype(F32) * (dh ** -0.5 * LOG2_E)).astype(BF16) for hd in range(heads)]

    def scores(hd, ks):
        return _dot_nt(qs[hd], k_ref[ks, cols[hd]]) - c_ref[hd:hd + 1, ks] * LOG2_E

    def update(hd, s, ks, carry):
        m_prev, l_prev, acc = carry
        m_new = jnp.maximum(m_prev, jnp.max(s, axis=-1, keepdims=True))
        alpha = jnp.exp2(m_prev - m_new)
        p = jnp.exp2(s - m_new)
        l_new = alpha * l_prev + jnp.sum(p, axis=-1, keepdims=True)
        acc = alpha * acc + _dot(p.astype(BF16), v_ref[ks, cols[hd]])
        return m_new, l_new, acc

    def body(j, carries):
        ks = pl.ds(pl.multiple_of(j * tq, tq), tq)
        return tuple(update(hd, scores(hd, ks), ks, carries[hd]) for hd in range(heads))

    init = (jnp.full((tq, 1), -jnp.inf, F32), jnp.zeros((tq, 1), F32), jnp.zeros((tq, dh), F32))
    carries = lax.fori_loop(0, qi, body, (init,) * heads)
    ks = pl.ds(pl.multiple_of(qi * tq, tq), tq)
    ti = lax.broadcasted_iota(jnp.int32, (tq, tq), 0)
    si = lax.broadcasted_iota(jnp.int32, (tq, tq), 1)
    for hd in range(heads):
        s = jnp.where(si <= ti, scores(hd, ks), -jnp.inf)
        _, l_fin, acc = update(hd, s, ks, carries[hd])
        o_ref[:, cols[hd]] = (acc / l_fin).astype(o_ref.dtype)


def fox_attention(zq, kv, c, batch, seq_len):
    m = zq.shape[0]
    hp = FOX_HEADS_PER_STEP
    width = hp * LANES
    groups = FOX_HEADS // hp
    tq = _pick_tile(seq_len, (512, 256))
    tiles = seq_len // tq
    kern = functools.partial(_fox_attn_kernel, heads=hp)
    return pl.pallas_call(
        kern,
        grid=(batch, groups, tiles),
        in_specs=[
            pl.BlockSpec((tq, width), lambda b, h, s: (b * tiles + s, h)),
            pl.BlockSpec((seq_len, width), lambda b, h, s: (b, h)),
            pl.BlockSpec((seq_len, width), lambda b, h, s: (b, groups + h)),
            pl.BlockSpec((None, None, hp, seq_len), lambda b, h, s: (b, h, 0, 0)),
        ],
        out_specs=pl.BlockSpec((tq, width), lambda b, h, s: (b * tiles + s, h)),
        out_shape=jax.ShapeDtypeStruct((m, FOX_HEADS * LANES), BF16),
        compiler_params=_params("parallel", "parallel", "arbitrary"),
        name="fox_attention",
    )(zq, kv, kv, c)


def kernel(x, mem, norm_mix, norm_ffn, norm_mem, norm_final, mem_w_kv, gla_w_in, gla_w_gate_up,
           gla_b_gate, gla_norm, gla_w_out, fox_kv_norm, fox_w_kv, fox_b_f, fox_w_in, fox_w_out,
           ffn_w_up, ffn_conv_w, ffn_conv_b, ffn_w_down):
    batch, seq_len, d = x.shape
    n_mem = mem.shape[1]
    depth = norm_mix.shape[0]
    n_gla = gla_w_in.shape[0]
    rank = gla_w_gate_up.shape[1]
    qk_width = gla_w_gate_up.shape[2]
    dk = qk_width // GLA_HEADS
    v_width = gla_norm.shape[1]
    dv = v_width // GLA_HEADS
    mem_width = mem_w_kv.shape[2] // 2
    fox_width = (fox_w_kv.shape[1] - FOX_HEADS) // 2
    assert mem_width == MEM_HEADS * LANES and fox_width == FOX_HEADS * LANES
    assert (2 * v_width) % dk == 0 and rank <= LANES

    xf = x.reshape(batch * seq_len, d)

    w_mem = jnp.transpose(mem_w_kv, (1, 0, 2)).reshape(d, depth * 2 * mem_width).astype(BF16)
    mem_kv = norm_matmul(mem.reshape(batch * n_mem, d), norm_mem, w_mem, BF16,
                         _pick_tile(w_mem.shape[1], (1024, 512)))

    fox_kv = fox_c = None
    for i in range(depth):
        if i == n_gla:
            fox_kv = norm_matmul(xf, fox_kv_norm, fox_w_kv[:, :2 * fox_width].astype(BF16), BF16,
                                 _pick_tile(2 * fox_width, (1024, 768, 512)))
            w_f = jnp.pad(fox_w_kv[:, 2 * fox_width:], ((0, 0), (0, LANES - FOX_HEADS))).astype(BF16)
            b_f = jnp.pad(fox_b_f, (0, LANES - FOX_HEADS)).reshape(1, LANES)
            fox_c = fox_gates(xf, fox_kv_norm, w_f, b_f, batch, seq_len)
            fox_c = fox_c.reshape(batch, GATE_ROWS // FOX_HEADS_PER_STEP, FOX_HEADS_PER_STEP, seq_len)
        if i < n_gla:
            w = gla_w_in[i]
            o_q, o_k, o_v = 0, qk_width, 2 * qk_width
            o_glr = o_v + v_width
            o_og = o_glr + rank
            o_mq = o_og + v_width
            w_in = jnp.concatenate([
                w[:, o_v:o_glr], w[:, o_og:o_mq], w[:, o_q:o_k], w[:, o_k:o_v], w[:, o_mq:],
                jnp.pad(w[:, o_glr:o_og], ((0, 0), (0, LANES - rank)))], axis=1).astype(BF16)
            z = norm_matmul(xf, norm_mix[i], w_in, BF16, _pick_tile(w_in.shape[1], (1920, 1152, 640, 384, 128)))
            a = gla_attention(z, gla_w_gate_up[i].reshape(rank, GLA_HEADS, dk).transpose(1, 0, 2).astype(BF16),
                              gla_b_gate[i].reshape(GLA_HEADS, 1, dk), gla_norm[i].reshape(GLA_HEADS, 1, dv),
                              batch, seq_len, dk, dv, rank)
            mq_block = (2 * v_width + 2 * qk_width) // mem_width
            w_out = gla_w_out[i].astype(BF16)
            split = v_width
        else:
            j = i - n_gla
            z = norm_matmul(xf, norm_mix[i], fox_w_in[j].astype(BF16), BF16,
                            _pick_tile(fox_w_in.shape[2], (1024, 512)))
            a = fox_attention(z, fox_kv, fox_c, batch, seq_len)
            mq_block = fox_width // mem_width
            w_out = fox_w_out[j].astype(BF16)
            split = fox_width
        mo = mem_attention(z, mq_block, mem_kv, i, batch, seq_len, n_mem)
        xf = out_proj(a, mo, w_out[:split], w_out[split:], xf)
        xf = conv_ffn(xf, norm_ffn[i], ffn_w_up[i].astype(BF16), ffn_conv_w[i], ffn_conv_b[i],
                      ffn_w_down[i].astype(BF16), norm_final, seq_len, final_norm=(i == depth - 1))
    return xf.reshape(batch, seq_len, d)
```

```python
import functools
import math

import jax
import jax.numpy as jnp
from jax import lax
from jax.experimental import pallas as pl
from jax.experimental.pallas import tpu as pltpu

GLA_HEADS = 4
GLA_GATE_TAU = 16.0
FOX_HEADS = 12
MEM_HEADS = 4
CONV_WIDTH = 3
EPS = 1e-6
LOG2_E = math.log2(math.e)

LANES = 128
SUBLANES_F32 = 8
SUBLANES_BF16 = 16
VMEM_LIMIT = 56 * 1024 * 1024

BF16 = jnp.bfloat16
F32 = jnp.float32

NT_DIMS = (((1,), (1,)), ((), ()))
TN_DIMS = (((0,), (0,)), ((), ()))


def _params(*sem):
    return pltpu.CompilerParams(dimension_semantics=sem, vmem_limit_bytes=VMEM_LIMIT)


def _rms(x, gain):
    return x * lax.rsqrt(jnp.mean(x * x, axis=-1, keepdims=True) + EPS) * gain


def _dot(a, b):
    return jnp.dot(a, b, preferred_element_type=F32)


def _dot_nt(a, b):
    return lax.dot_general(a, b, NT_DIMS, preferred_element_type=F32)


def _dot_tn(a, b):
    return lax.dot_general(a, b, TN_DIMS, preferred_element_type=F32)


def _split2(x):
    hi = x.astype(BF16)
    lo = (x - hi.astype(F32)).astype(BF16)
    return hi, lo


def _log_sigmoid(x):
    return jnp.minimum(x, 0.0) - jnp.log1p(jnp.exp(-jnp.abs(x)))


def _pick_tile(n, preferred):
    for t in preferred:
        if n % t == 0:
            return t
    return n


def _norm_matmul_kernel(x_ref, g_ref, w_ref, o_ref, h_ref):
    @pl.when(pl.program_id(1) == 0)
    def _():
        h_ref[...] = _rms(x_ref[...], g_ref[...]).astype(BF16)

    o_ref[...] = _dot(h_ref[...], w_ref[...]).astype(o_ref.dtype)


def norm_matmul(x, gain, w, out_dtype, tn):
    m, d = x.shape
    n = w.shape[1]
    tm = _pick_tile(m, (1024, 512, 256))
    return pl.pallas_call(
        _norm_matmul_kernel,
        grid=(m // tm, n // tn),
        in_specs=[
            pl.BlockSpec((tm, d), lambda i, j: (i, 0)),
            pl.BlockSpec((1, d), lambda i, j: (0, 0)),
            pl.BlockSpec((d, tn), lambda i, j: (0, j)),
        ],
        out_specs=pl.BlockSpec((tm, tn), lambda i, j: (i, j)),
        out_shape=jax.ShapeDtypeStruct((m, n), out_dtype),
        scratch_shapes=[pltpu.VMEM((tm, d), BF16)],
        compiler_params=_params("parallel", "arbitrary"),
        name="norm_matmul",
    )(x, gain.reshape(1, d), w)


def _out_proj_kernel(a_ref, b_ref, wa_ref, wb_ref, x_ref, o_ref):
    o_ref[...] = x_ref[...] + _dot(a_ref[...], wa_ref[...]) + _dot(b_ref[...], wb_ref[...])


def out_proj(a, b, wa, wb, x):
    m, d = x.shape
    ka, kb = a.shape[1], b.shape[1]
    tm = _pick_tile(m, (1024, 512, 256))
    tn = _pick_tile(d, (1024, 512, 256))
    return pl.pallas_call(
        _out_proj_kernel,
        grid=(m // tm, d // tn),
        in_specs=[
            pl.BlockSpec((tm, ka), lambda i, j: (i, 0)),
            pl.BlockSpec((tm, kb), lambda i, j: (i, 0)),
            pl.BlockSpec((ka, tn), lambda i, j: (0, j)),
            pl.BlockSpec((kb, tn), lambda i, j: (0, j)),
            pl.BlockSpec((tm, tn), lambda i, j: (i, j)),
        ],
        out_specs=pl.BlockSpec((tm, tn), lambda i, j: (i, j)),
        out_shape=jax.ShapeDtypeStruct((m, d), F32),
        compiler_params=_params("parallel", "arbitrary"),
        name="out_proj",
    )(a, b, wa, wb, x)


HALO = SUBLANES_BF16
FFN_ROW_CHUNK = 512


def _ffn_kernel(x_ref, xh_ref, g_ref, wa_ref, wv_ref, cwa_ref, cwv_ref, cba_ref, cbv_ref, wd_ref,
                gf_ref, o_ref, h_ref, *, tiles_per_seq, final_norm):
    i = pl.program_id(0)
    f = pl.program_id(1)
    tm = x_ref.shape[0]

    @pl.when(f == 0)
    def _():
        x = x_ref[...]
        prev = _rms(xh_ref[...], g_ref[...])
        prev = jnp.where(i % tiles_per_seq == 0, 0.0, prev)
        h_ref[0:HALO, :] = prev.astype(BF16)
        h_ref[HALO:HALO + tm, :] = _rms(x, g_ref[...]).astype(BF16)
        o_ref[...] = x

    def conv(u, cw_ref, cb_ref):
        cw = cw_ref[...]
        y = cw[2:3, :] * u + cw[1:2, :] * pltpu.roll(u, 1, axis=0) + cw[0:1, :] * pltpu.roll(u, 2, axis=0)
        return y[HALO:, :] + cb_ref[...]

    rc = FFN_ROW_CHUNK if tm % FFN_ROW_CHUNK == 0 else tm

    def up(lo):
        h = h_ref[lo:lo + rc + HALO, :]
        return _dot(h, wa_ref[...]), _dot(h, wv_ref[...])

    starts = list(range(0, tm, rc))
    u_next = up(starts[0])
    for n, lo in enumerate(starts):
        ua, uv = u_next
        if n + 1 < len(starts):
            u_next = up(starts[n + 1])
        ya = conv(ua, cwa_ref, cba_ref)
        yv = conv(uv, cwv_ref, cbv_ref)
        act = (ya * jax.nn.sigmoid(ya) * yv).astype(BF16)
        o_ref[lo:lo + rc, :] += _dot(act, wd_ref[...])

    if final_norm:
        @pl.when(f == pl.num_programs(1) - 1)
        def _():
            o_ref[...] = _rms(o_ref[...], gf_ref[...])


def conv_ffn(x, gain, w_up, conv_w, conv_b, w_down, final_gain, seq_len, final_norm):
    m, d = x.shape
    hidden = w_down.shape[0]
    tm = _pick_tile(seq_len, (1024, 512, 256))
    tf = _pick_tile(hidden, (512, 256, 128))
    nf = hidden // tf
    halo_blocks = tm // HALO
    kern = functools.partial(_ffn_kernel, tiles_per_seq=seq_len // tm, final_norm=final_norm)
    return pl.pallas_call(
        kern,
        grid=(m // tm, nf),
        in_specs=[
            pl.BlockSpec((tm, d), lambda i, f: (i, 0), pipeline_mode=pl.Buffered(1)),
            pl.BlockSpec((HALO, d), lambda i, f: (jnp.maximum(i * halo_blocks - 1, 0), 0)),
            pl.BlockSpec((1, d), lambda i, f: (0, 0)),
            pl.BlockSpec((d, tf), lambda i, f: (0, f)),
            pl.BlockSpec((d, tf), lambda i, f: (0, nf + f)),
            pl.BlockSpec((CONV_WIDTH, tf), lambda i, f: (0, f)),
            pl.BlockSpec((CONV_WIDTH, tf), lambda i, f: (0, nf + f)),
            pl.BlockSpec((1, tf), lambda i, f: (0, f)),
            pl.BlockSpec((1, tf), lambda i, f: (0, nf + f)),
            pl.BlockSpec((tf, d), lambda i, f: (f, 0)),
            pl.BlockSpec((1, d), lambda i, f: (0, 0)),
        ],
        out_specs=pl.BlockSpec((tm, d), lambda i, f: (i, 0)),
        out_shape=jax.ShapeDtypeStruct((m, d), F32),
        scratch_shapes=[pltpu.VMEM((tm + HALO, d), BF16)],
        compiler_params=_params("parallel", "arbitrary"),
        name="conv_ffn",
    )(x, x, gain.reshape(1, d), w_up, w_up, conv_w, conv_w, conv_b.reshape(1, -1),
      conv_b.reshape(1, -1), w_down, final_gain.reshape(1, d))


def _mem_attn_kernel(q_ref, mk_ref, mv_ref, o_ref, *, heads):
    dh = q_ref.shape[1] // heads
    scale = dh ** -0.5
    for hd in range(heads):
        cols = slice(hd * dh, (hd + 1) * dh)
        q = q_ref[:, cols].astype(BF16)
        s = _dot_nt(q, mk_ref[:, cols]) * scale
        s = s - jnp.max(s, axis=-1, keepdims=True)
        p = jnp.exp(s)
        p = p / jnp.sum(p, axis=-1, keepdims=True)
        o_ref[:, cols] = _dot(p.astype(BF16), mv_ref[:, cols]).astype(o_ref.dtype)


def mem_attention(z, q_col_block, mem_kv, layer, batch, seq_len, n_mem):
    m = z.shape[0]
    mem_width = MEM_HEADS * LANES
    t = _pick_tile(seq_len, (1024, 512, 256))
    tiles = seq_len // t
    kern = functools.partial(_mem_attn_kernel, heads=MEM_HEADS)
    return pl.pallas_call(
        kern,
        grid=(batch, tiles),
        in_specs=[
            pl.BlockSpec((t, mem_width), lambda b, s: (b * tiles + s, q_col_block)),
            pl.BlockSpec((n_mem, mem_width), lambda b, s: (b, 2 * layer)),
            pl.BlockSpec((n_mem, mem_width), lambda b, s: (b, 2 * layer + 1)),
        ],
        out_specs=pl.BlockSpec((t, mem_width), lambda b, s: (b * tiles + s, 0)),
        out_shape=jax.ShapeDtypeStruct((m, mem_width), BF16),
        compiler_params=_params("parallel", "arbitrary"),
        name="mem_attention",
    )(z, mem_kv, mem_kv)


GLA_BLOCK = 256


def _ref_rows(bc, half):
    rows, dk = bc.shape
    if half >= SUBLANES_F32:
        nb = rows // (2 * half)
        b3 = bc.reshape(nb, 2 * half, dk)
        return jnp.broadcast_to(b3[:, half:half + 1, :], b3.shape).reshape(rows, dk)
    b3 = bc.reshape(rows // SUBLANES_F32, SUBLANES_F32, dk)
    sub = lax.broadcasted_iota(jnp.int32, b3.shape, 1)
    out = None
    for mid in range(half, SUBLANES_F32, 2 * half):
        cand = jnp.broadcast_to(b3[:, mid:mid + 1, :], b3.shape)
        out = cand if out is None else jnp.where(sub >= mid - half, cand, out)
    return out.reshape(rows, dk)


def _gla_kernel(q_ref, k_ref, v_ref, glr_ref, og_ref, wg_ref, bg_ref, gn_ref, o_ref, state_ref, *, rank):
    t = pl.program_id(2)
    rows = q_ref.shape[0]
    dk = q_ref.shape[1]
    blk = GLA_BLOCK if rows % GLA_BLOCK == 0 else rows
    n_levels = int(math.log2(blk))

    @pl.when(t == 0)
    def _():
        state_ref[...] = jnp.zeros_like(state_ref)

    ti = lax.broadcasted_iota(jnp.int32, (blk, blk), 0)
    si = lax.broadcasted_iota(jnp.int32, (blk, blk), 1)
    tri = jnp.where(si <= ti, 1.0, 0.0).astype(BF16)
    x = ti ^ si
    top_bit = 31 - lax.clz(jnp.maximum(x, 1))
    pair_level = jnp.where(si < ti, top_bit, jnp.where(si == ti, -1, -2))
    qscale = dk ** -0.5

    def step(i, carry):
        r0 = pl.multiple_of(i * blk, blk)
        rs = pl.ds(r0, blk)
        q = q_ref[rs, :].astype(F32) * qscale
        k = k_ref[rs, :].astype(F32)
        v = v_ref[rs, :].astype(BF16)
        glr = glr_ref[rs, 0:rank].astype(BF16)
        g = _log_sigmoid(_dot(glr, wg_ref[...]) + bg_ref[...]) * (LOG2_E / GLA_GATE_TAU)
        g_hi, g_lo = _split2(g)
        bc = _dot(tri, g_hi) + _dot(tri, g_lo)
        b_last = bc[blk - 1:blk, :]
        state = state_ref[...]

        o = _dot_nt((q * jnp.exp2(bc)).astype(BF16), state.astype(BF16))

        att = jnp.zeros((blk, blk), F32)
        for p in range(n_levels):
            e = jnp.exp2(-jnp.abs(bc - _ref_rows(bc, 1 << p)))
            att = jnp.where(pair_level == p, _dot_nt((q * e).astype(BF16), (k * e).astype(BF16)), att)
        att = jnp.where(pair_level == -1, _dot_nt(q.astype(BF16), k.astype(BF16)), att)
        o = o + _dot(att.astype(BF16), v)

        kd = (k * jnp.exp2(b_last - bc)).astype(BF16)
        state_ref[...] = state * jnp.exp2(b_last) + _dot_tn(v, kd)

        o = o * lax.rsqrt(jnp.mean(o * o, axis=-1, keepdims=True) + EPS) * gn_ref[...]
        og = og_ref[rs, :].astype(F32)
        o_ref[rs, :] = (o * (og * jax.nn.sigmoid(og))).astype(o_ref.dtype)
        return carry

    lax.fori_loop(0, rows // blk, step, 0)


def gla_attention(z, w_gate, b_gate, gn_gain, batch, seq_len, dk, dv, rank):
    m = z.shape[0]
    heads = GLA_HEADS
    t = _pick_tile(seq_len, (1024, 512, 256))
    tiles = seq_len // t
    v0, og0 = 0, (heads * dv) // dv
    q0 = (2 * heads * dv) // dk
    k0 = q0 + heads
    glr0 = (2 * heads * dv + 2 * heads * dk + MEM_HEADS * LANES) // LANES
    rows = lambda b, h, s: b * tiles + s
    kern = functools.partial(_gla_kernel, rank=rank)
    return pl.pallas_call(
        kern,
        grid=(batch, heads, tiles),
        in_specs=[
            pl.BlockSpec((t, dk), lambda b, h, s: (rows(b, h, s), q0 + h)),
            pl.BlockSpec((t, dk), lambda b, h, s: (rows(b, h, s), k0 + h)),
            pl.BlockSpec((t, dv), lambda b, h, s: (rows(b, h, s), v0 + h)),
            pl.BlockSpec((t, LANES), lambda b, h, s: (rows(b, h, s), glr0)),
            pl.BlockSpec((t, dv), lambda b, h, s: (rows(b, h, s), og0 + h)),
            pl.BlockSpec((None, rank, dk), lambda b, h, s: (h, 0, 0)),
            pl.BlockSpec((None, 1, dk), lambda b, h, s: (h, 0, 0)),
            pl.BlockSpec((None, 1, dv), lambda b, h, s: (h, 0, 0)),
        ],
        out_specs=pl.BlockSpec((t, dv), lambda b, h, s: (rows(b, h, s), h)),
        out_shape=jax.ShapeDtypeStruct((m, heads * dv), BF16),
        scratch_shapes=[pltpu.VMEM((dv, dk), F32)],
        compiler_params=_params("parallel", "parallel", "arbitrary"),
        name="gla_attention",
    )(z, z, z, z, z, w_gate, b_gate, gn_gain)


BIAS_PIECES = 3


def _split3(x):
    hi = x.astype(BF16)
    r1 = x - hi.astype(F32)
    mid = r1.astype(BF16)
    lo = (r1 - mid.astype(F32)).astype(BF16)
    return hi, mid, lo


def _fox_gate_kernel(x_ref, g_ref, w_ref, b_ref, o_ref, carry_ref, *, heads):
    s = pl.program_id(1)
    t = x_ref.shape[0]

    @pl.when(s == 0)
    def _():
        carry_ref[...] = jnp.zeros_like(carry_ref)

    h_hi, h_lo = _split2(_rms(x_ref[...], g_ref[...]))
    w = w_ref[...]
    log_f = _log_sigmoid(_dot(h_hi, w) + _dot(h_lo, w) + b_ref[...])
    ti = lax.broadcasted_iota(jnp.int32, (t, t), 0)
    si = lax.broadcasted_iota(jnp.int32, (t, t), 1)
    tri = jnp.where(si <= ti, 1.0, 0.0).astype(BF16)
    f_hi, f_mid, f_lo = _split3(log_f)
    c = carry_ref[...] + _dot(tri, f_hi) + _dot(tri, f_mid) + _dot(tri, f_lo)
    carry_ref[...] = c[t - 1:t, :]
    lane = lax.broadcasted_iota(jnp.int32, (t, LANES), 1)
    for hd in range(heads):
        pieces = _split3(jnp.broadcast_to(c[:, hd:hd + 1] * (-LOG2_E), (t, LANES)))
        slab = jnp.zeros((t, LANES), F32)
        for n, piece in enumerate(pieces):
            slab = jnp.where(lane == n, piece.astype(F32), slab)
        o_ref[:, hd * LANES:(hd + 1) * LANES] = slab.astype(BF16)


def fox_gates(x, gain, w_f, b_f, batch, seq_len):
    m, d = x.shape
    t = _pick_tile(seq_len, (512, 256))
    tiles = seq_len // t
    kern = functools.partial(_fox_gate_kernel, heads=FOX_HEADS)
    return pl.pallas_call(
        kern,
        grid=(batch, tiles),
        in_specs=[
            pl.BlockSpec((t, d), lambda b, s: (b * tiles + s, 0)),
            pl.BlockSpec((1, d), lambda b, s: (0, 0)),
            pl.BlockSpec((d, LANES), lambda b, s: (0, 0)),
            pl.BlockSpec((1, LANES), lambda b, s: (0, 0)),
        ],
        out_specs=pl.BlockSpec((t, FOX_HEADS * LANES), lambda b, s: (b * tiles + s, 0)),
        out_shape=jax.ShapeDtypeStruct((m, FOX_HEADS * LANES), BF16),
        scratch_shapes=[pltpu.VMEM((1, LANES), F32)],
        compiler_params=_params("parallel", "arbitrary"),
        name="fox_gates",
    )(x, gain.reshape(1, d), w_f, b_f)


FOX_HEADS_PER_STEP = 2


def _fox_attn_kernel(q_ref, k_ref, kb_ref, v_ref, o_ref, m_ref, acc_ref, *, heads):
    qi = pl.program_id(2)
    tq = q_ref.shape[0]
    dh = q_ref.shape[1] // heads
    cols = [slice(hd * dh, (hd + 1) * dh) for hd in range(heads)]
    lane = lax.broadcasted_iota(jnp.int32, (tq, dh), 1)
    bias_taps = jnp.where(lane < BIAS_PIECES, 1.0, 0.0).astype(BF16)
    ones_col = jnp.where(lane == 0, 1.0, 0.0).astype(BF16)
    qs = [jnp.concatenate([(q_ref[:, cols[hd]].astype(F32) * (dh ** -0.5 * LOG2_E)).astype(BF16), bias_taps],
                          axis=1) for hd in range(heads)]

    def scores(hd, ks):
        return _dot_nt(qs[hd], jnp.concatenate([k_ref[ks, cols[hd]], kb_ref[ks, cols[hd]]], axis=1))

    def update(hd, s, ks):
        m_prev = m_ref[hd]
        m_new = jnp.maximum(m_prev, jnp.max(s, axis=-1, keepdims=True))
        alpha = jnp.exp2(m_prev - m_new)
        p = jnp.concatenate([jnp.exp2((s[:, c:c + LANES] - m_new).astype(BF16))
                             for c in range(0, s.shape[1], LANES)], axis=1)
        pv = _dot(p, jnp.concatenate([v_ref[ks, cols[hd]], ones_col], axis=1))
        acc = acc_ref[hd]
        acc_ref[hd] = jnp.concatenate([alpha * acc[:, c:c + LANES] for c in range(0, acc.shape[1], LANES)],
                                      axis=1) + pv
        m_ref[hd] = m_new

    def body(j, carry):
        ks = pl.ds(pl.multiple_of(j * tq, tq), tq)
        ss = [scores(hd, ks) for hd in range(heads)]
        for hd in range(heads):
            update(hd, ss[hd], ks)
        return carry

    m_ref[...] = jnp.full(m_ref.shape, -jnp.inf, F32)
    acc_ref[...] = jnp.zeros_like(acc_ref)
    lax.fori_loop(0, qi, body, 0)
    ks = pl.ds(pl.multiple_of(qi * tq, tq), tq)
    ti = lax.broadcasted_iota(jnp.int32, (tq, tq), 0)
    si = lax.broadcasted_iota(jnp.int32, (tq, tq), 1)
    ss = [jnp.where(si <= ti, scores(hd, ks), -jnp.inf) for hd in range(heads)]
    for hd in range(heads):
        update(hd, ss[hd], ks)
        acc = acc_ref[hd]
        o_ref[:, cols[hd]] = (acc[:, 0:dh] / acc[:, dh:dh + 1]).astype(o_ref.dtype)


def fox_attention(zq, kv, kbias, batch, seq_len):
    m = zq.shape[0]
    hp = FOX_HEADS_PER_STEP
    width = hp * LANES
    groups = FOX_HEADS // hp
    tq = _pick_tile(seq_len, (512, 256))
    tiles = seq_len // tq
    kern = functools.partial(_fox_attn_kernel, heads=hp)
    return pl.pallas_call(
        kern,
        grid=(batch, groups, tiles),
        in_specs=[
            pl.BlockSpec((tq, width), lambda b, h, s: (b * tiles + s, h)),
            pl.BlockSpec((seq_len, width), lambda b, h, s: (b, h)),
            pl.BlockSpec((seq_len, width), lambda b, h, s: (b, h)),
            pl.BlockSpec((seq_len, width), lambda b, h, s: (b, groups + h)),
        ],
        out_specs=pl.BlockSpec((tq, width), lambda b, h, s: (b * tiles + s, h)),
        out_shape=jax.ShapeDtypeStruct((m, FOX_HEADS * LANES), BF16),
        scratch_shapes=[pltpu.VMEM((hp, tq, LANES), F32), pltpu.VMEM((hp, tq, 2 * LANES), F32)],
        compiler_params=_params("parallel", "parallel", "arbitrary"),
        name="fox_attention",
    )(zq, kv, kbias, kv)


def kernel(x, mem, norm_mix, norm_ffn, norm_mem, norm_final, mem_w_kv, gla_w_in, gla_w_gate_up,
           gla_b_gate, gla_norm, gla_w_out, fox_kv_norm, fox_w_kv, fox_b_f, fox_w_in, fox_w_out,
           ffn_w_up, ffn_conv_w, ffn_conv_b, ffn_w_down):
    batch, seq_len, d = x.shape
    n_mem = mem.shape[1]
    depth = norm_mix.shape[0]
    n_gla = gla_w_in.shape[0]
    rank = gla_w_gate_up.shape[1]
    qk_width = gla_w_gate_up.shape[2]
    dk = qk_width // GLA_HEADS
    v_width = gla_norm.shape[1]
    dv = v_width // GLA_HEADS
    mem_width = mem_w_kv.shape[2] // 2
    fox_width = (fox_w_kv.shape[1] - FOX_HEADS) // 2
    assert mem_width == MEM_HEADS * LANES and fox_width == FOX_HEADS * LANES
    assert (2 * v_width) % dk == 0 and rank <= LANES

    xf = x.reshape(batch * seq_len, d)

    w_mem = jnp.transpose(mem_w_kv, (1, 0, 2)).reshape(d, depth * 2 * mem_width).astype(BF16)
    mem_kv = norm_matmul(mem.reshape(batch * n_mem, d), norm_mem, w_mem, BF16,
                         _pick_tile(w_mem.shape[1], (1024, 512)))

    fox_kv = fox_c = None
    for i in range(depth):
        if i == n_gla:
            fox_kv = norm_matmul(xf, fox_kv_norm, fox_w_kv[:, :2 * fox_width].astype(BF16), BF16,
                                 _pick_tile(2 * fox_width, (1024, 768, 512)))
            w_f = jnp.pad(fox_w_kv[:, 2 * fox_width:], ((0, 0), (0, LANES - FOX_HEADS))).astype(BF16)
            b_f = jnp.pad(fox_b_f, (0, LANES - FOX_HEADS)).reshape(1, LANES)
            fox_c = fox_gates(xf, fox_kv_norm, w_f, b_f, batch, seq_len)
        if i < n_gla:
            w = gla_w_in[i]
            o_q, o_k, o_v = 0, qk_width, 2 * qk_width
            o_glr = o_v + v_width
            o_og = o_glr + rank
            o_mq = o_og + v_width
            w_in = jnp.concatenate([
                w[:, o_v:o_glr], w[:, o_og:o_mq], w[:, o_q:o_k], w[:, o_k:o_v], w[:, o_mq:],
                jnp.pad(w[:, o_glr:o_og], ((0, 0), (0, LANES - rank)))], axis=1).astype(BF16)
            z = norm_matmul(xf, norm_mix[i], w_in, BF16, _pick_tile(w_in.shape[1], (1920, 1152, 640, 384, 128)))
            a = gla_attention(z, gla_w_gate_up[i].reshape(rank, GLA_HEADS, dk).transpose(1, 0, 2).astype(BF16),
                              gla_b_gate[i].reshape(GLA_HEADS, 1, dk), gla_norm[i].reshape(GLA_HEADS, 1, dv),
                              batch, seq_len, dk, dv, rank)
            mq_block = (2 * v_width + 2 * qk_width) // mem_width
            w_out = gla_w_out[i].astype(BF16)
            split = v_width
        else:
            j = i - n_gla
            z = norm_matmul(xf, norm_mix[i], fox_w_in[j].astype(BF16), BF16,
                            _pick_tile(fox_w_in.shape[2], (1024, 512)))
            a = fox_attention(z, fox_kv, fox_c, batch, seq_len)
            mq_block = fox_width // mem_width
            w_out = fox_w_out[j].astype(BF16)
            split = fox_width
        mo = mem_attention(z, mq_block, mem_kv, i, batch, seq_len, n_mem)
        xf = out_proj(a, mo, w_out[:split], w_out[split:], xf)
        xf = conv_ffn(xf, norm_ffn[i], ffn_w_up[i].astype(BF16), ffn_conv_w[i], ffn_conv_b[i],
                      ffn_w_down[i].astype(BF16), norm_final, seq_len, final_norm=(i == depth - 1))
    return xf.reshape(batch, seq_len, d)
```

```python
import functools
import math

import jax
import jax.numpy as jnp
from jax import lax
from jax.experimental import pallas as pl
from jax.experimental.pallas import tpu as pltpu

GLA_HEADS = 4
GLA_GATE_TAU = 16.0
FOX_HEADS = 12
MEM_HEADS = 4
CONV_WIDTH = 3
EPS = 1e-6
LOG2_E = math.log2(math.e)

LANES = 128
SUBLANES_F32 = 8
SUBLANES_BF16 = 16
VMEM_LIMIT = 56 * 1024 * 1024

BF16 = jnp.bfloat16
F32 = jnp.float32

NT_DIMS = (((1,), (1,)), ((), ()))
TN_DIMS = (((0,), (0,)), ((), ()))


def _params(*sem):
    return pltpu.CompilerParams(dimension_semantics=sem, vmem_limit_bytes=VMEM_LIMIT)


def _rms(x, gain):
    return x * lax.rsqrt(jnp.mean(x * x, axis=-1, keepdims=True) + EPS) * gain


def _dot(a, b):
    return jnp.dot(a, b, preferred_element_type=F32)


def _dot_nt(a, b):
    return lax.dot_general(a, b, NT_DIMS, preferred_element_type=F32)


def _dot_tn(a, b):
    return lax.dot_general(a, b, TN_DIMS, preferred_element_type=F32)


def _split2(x):
    hi = x.astype(BF16)
    lo = (x - hi.astype(F32)).astype(BF16)
    return hi, lo


def _neg_abs(x):
    bits = lax.bitcast_convert_type(x, jnp.uint32) | jnp.uint32(0x80000000)
    return lax.bitcast_convert_type(bits, F32)


def _log_sigmoid(x):
    return jnp.minimum(x, 0.0) - jnp.log1p(jnp.exp(-jnp.abs(x)))


def _pick_tile(n, preferred):
    for t in preferred:
        if n % t == 0:
            return t
    return n


def _norm_matmul_kernel(x_ref, g_ref, w_ref, o_ref, h_ref):
    @pl.when(pl.program_id(1) == 0)
    def _():
        h_ref[...] = _rms(x_ref[...], g_ref[...]).astype(BF16)

    o_ref[...] = _dot(h_ref[...], w_ref[...]).astype(o_ref.dtype)


def norm_matmul(x, gain, w, out_dtype, tn):
    m, d = x.shape
    n = w.shape[1]
    tm = _pick_tile(m, (1024, 512, 256))
    return pl.pallas_call(
        _norm_matmul_kernel,
        grid=(m // tm, n // tn),
        in_specs=[
            pl.BlockSpec((tm, d), lambda i, j: (i, 0)),
            pl.BlockSpec((1, d), lambda i, j: (0, 0)),
            pl.BlockSpec((d, tn), lambda i, j: (0, j)),
        ],
        out_specs=pl.BlockSpec((tm, tn), lambda i, j: (i, j)),
        out_shape=jax.ShapeDtypeStruct((m, n), out_dtype),
        scratch_shapes=[pltpu.VMEM((tm, d), BF16)],
        compiler_params=_params("parallel", "arbitrary"),
        name="norm_matmul",
    )(x, gain.reshape(1, d), w)


def _out_proj_kernel(a_ref, b_ref, wa_ref, wb_ref, x_ref, o_ref):
    o_ref[...] = x_ref[...] + _dot(a_ref[...], wa_ref[...]) + _dot(b_ref[...], wb_ref[...])


def out_proj(a, b, wa, wb, x):
    m, d = x.shape
    ka, kb = a.shape[1], b.shape[1]
    tm = _pick_tile(m, (1024, 512, 256))
    tn = _pick_tile(d, (1024, 512, 256))
    return pl.pallas_call(
        _out_proj_kernel,
        grid=(m // tm, d // tn),
        in_specs=[
            pl.BlockSpec((tm, ka), lambda i, j: (i, 0)),
            pl.BlockSpec((tm, kb), lambda i, j: (i, 0)),
            pl.BlockSpec((ka, tn), lambda i, j: (0, j)),
            pl.BlockSpec((kb, tn), lambda i, j: (0, j)),
            pl.BlockSpec((tm, tn), lambda i, j: (i, j)),
        ],
        out_specs=pl.BlockSpec((tm, tn), lambda i, j: (i, j)),
        out_shape=jax.ShapeDtypeStruct((m, d), F32),
        compiler_params=_params("parallel", "arbitrary"),
        name="out_proj",
    )(a, b, wa, wb, x)


HALO = SUBLANES_BF16
FFN_CHUNK_PARTS = (1, 1)


def _ffn_kernel(x_ref, xh_ref, g_ref, wa_ref, wv_ref, cwa_ref, cwv_ref, cba_ref, cbv_ref, wd_ref,
                gf_ref, o_ref, h_ref, *, tiles_per_seq, final_norm):
    i = pl.program_id(0)
    f = pl.program_id(1)
    tm = x_ref.shape[0]

    @pl.when(f == 0)
    def _():
        x = x_ref[...]
        prev = _rms(xh_ref[...], g_ref[...])
        prev = jnp.where(i % tiles_per_seq == 0, 0.0, prev)
        h_ref[0:HALO, :] = prev.astype(BF16)
        h_ref[HALO:HALO + tm, :] = _rms(x, g_ref[...]).astype(BF16)
        o_ref[...] = x

    def conv(u, cw_ref, cb_ref):
        cw = cw_ref[...]
        y = cw[2:3, :] * u + cw[1:2, :] * pltpu.roll(u, 1, axis=0) + cw[0:1, :] * pltpu.roll(u, 2, axis=0)
        return y[HALO:, :] + cb_ref[...]

    def up(lo, rc):
        h = h_ref[lo:lo + rc + HALO, :]
        return _dot(h, wa_ref[...]), _dot(h, wv_ref[...])

    sizes = [c * tm // sum(FFN_CHUNK_PARTS) for c in FFN_CHUNK_PARTS]
    starts = [sum(sizes[:n]) for n in range(len(sizes))]
    u_next = up(starts[0], sizes[0])
    for n, (lo, rc) in enumerate(zip(starts, sizes)):
        ua, uv = u_next
        if n + 1 < len(starts):
            u_next = up(starts[n + 1], sizes[n + 1])
        ya = conv(ua, cwa_ref, cba_ref)
        yv = conv(uv, cwv_ref, cbv_ref)
        act = (ya * jax.nn.sigmoid(ya) * yv).astype(BF16)
        o_ref[lo:lo + rc, :] += _dot(act, wd_ref[...])

    if final_norm:
        @pl.when(f == pl.num_programs(1) - 1)
        def _():
            o_ref[...] = _rms(o_ref[...], gf_ref[...])


def conv_ffn(x, gain, w_up, conv_w, conv_b, w_down, final_gain, seq_len, final_norm):
    m, d = x.shape
    hidden = w_down.shape[0]
    tm = _pick_tile(seq_len, (1024, 512, 256))
    tf = _pick_tile(hidden, (512, 256, 128))
    nf = hidden // tf
    halo_blocks = tm // HALO
    assert tm % (sum(FFN_CHUNK_PARTS) * SUBLANES_BF16) == 0
    w_tiles = w_up.reshape(d, 2, nf, tf).transpose(1, 2, 0, 3)
    kern = functools.partial(_ffn_kernel, tiles_per_seq=seq_len // tm, final_norm=final_norm)
    return pl.pallas_call(
        kern,
        grid=(m // tm, nf),
        in_specs=[
            pl.BlockSpec((tm, d), lambda i, f: (i, 0)),
            pl.BlockSpec((HALO, d), lambda i, f: (jnp.maximum(i * halo_blocks - 1, 0), 0)),
            pl.BlockSpec((1, d), lambda i, f: (0, 0)),
            pl.BlockSpec((None, None, d, tf), lambda i, f: (0, f, 0, 0)),
            pl.BlockSpec((None, None, d, tf), lambda i, f: (1, f, 0, 0)),
            pl.BlockSpec((CONV_WIDTH, tf), lambda i, f: (0, f)),
            pl.BlockSpec((CONV_WIDTH, tf), lambda i, f: (0, nf + f)),
            pl.BlockSpec((1, tf), lambda i, f: (0, f)),
            pl.BlockSpec((1, tf), lambda i, f: (0, nf + f)),
            pl.BlockSpec((tf, d), lambda i, f: (f, 0)),
            pl.BlockSpec((1, d), lambda i, f: (0, 0)),
        ],
        out_specs=pl.BlockSpec((tm, d), lambda i, f: (i, 0)),
        out_shape=jax.ShapeDtypeStruct((m, d), F32),
        scratch_shapes=[pltpu.VMEM((tm + HALO, d), BF16)],
        compiler_params=_params("parallel", "arbitrary"),
        name="conv_ffn",
    )(x, x, gain.reshape(1, d), w_tiles, w_tiles, conv_w, conv_w, conv_b.reshape(1, -1),
      conv_b.reshape(1, -1), w_down, final_gain.reshape(1, d))


def _mem_attn_kernel(q_ref, mk_ref, mv_ref, o_ref, *, heads):
    dh = q_ref.shape[1] // heads
    scale = dh ** -0.5
    for hd in range(heads):
        cols = slice(hd * dh, (hd + 1) * dh)
        q = q_ref[:, cols].astype(BF16)
        s = _dot_nt(q, mk_ref[:, cols]) * scale
        s = s - jnp.max(s, axis=-1, keepdims=True)
        p = jnp.exp(s)
        p = p / jnp.sum(p, axis=-1, keepdims=True)
        o_ref[:, cols] = _dot(p.astype(BF16), mv_ref[:, cols]).astype(o_ref.dtype)


def mem_attention(z, q_col_block, mem_kv, layer, batch, seq_len, n_mem):
    m = z.shape[0]
    mem_width = MEM_HEADS * LANES
    t = _pick_tile(seq_len, (1024, 512, 256))
    tiles = seq_len // t
    kern = functools.partial(_mem_attn_kernel, heads=MEM_HEADS)
    return pl.pallas_call(
        kern,
        grid=(batch, tiles),
        in_specs=[
            pl.BlockSpec((t, mem_width), lambda b, s: (b * tiles + s, q_col_block)),
            pl.BlockSpec((n_mem, mem_width), lambda b, s: (b, 2 * layer)),
            pl.BlockSpec((n_mem, mem_width), lambda b, s: (b, 2 * layer + 1)),
        ],
        out_specs=pl.BlockSpec((t, mem_width), lambda b, s: (b * tiles + s, 0)),
        out_shape=jax.ShapeDtypeStruct((m, mem_width), BF16),
        compiler_params=_params("parallel", "arbitrary"),
        name="mem_attention",
    )(z, mem_kv, mem_kv)


GLA_BLOCK = 256


def _ref_rows(bc, half):
    rows, dk = bc.shape
    if half >= SUBLANES_F32:
        nb = rows // (2 * half)
        b3 = bc.reshape(nb, 2 * half, dk)
        return jnp.broadcast_to(b3[:, half:half + 1, :], b3.shape).reshape(rows, dk)
    b3 = bc.reshape(rows // SUBLANES_F32, SUBLANES_F32, dk)
    sub = lax.broadcasted_iota(jnp.int32, b3.shape, 1)
    out = None
    for mid in range(half, SUBLANES_F32, 2 * half):
        cand = jnp.broadcast_to(b3[:, mid:mid + 1, :], b3.shape)
        out = cand if out is None else jnp.where(sub >= mid - half, cand, out)
    return out.reshape(rows, dk)


def _gla_kernel(q_ref, k_ref, v_ref, glr_ref, og_ref, wg_ref, bg_ref, gn_ref, o_ref, state_ref, *, rank):
    t = pl.program_id(2)
    rows = q_ref.shape[0]
    dk = q_ref.shape[1]
    blk = GLA_BLOCK if rows % GLA_BLOCK == 0 else rows
    n_levels = int(math.log2(blk))

    @pl.when(t == 0)
    def _():
        state_ref[...] = jnp.zeros_like(state_ref)

    ti = lax.broadcasted_iota(jnp.int32, (blk, blk), 0)
    si = lax.broadcasted_iota(jnp.int32, (blk, blk), 1)
    tri = jnp.where(si <= ti, 1.0, 0.0).astype(BF16)
    x = ti ^ si
    top_bit = 31 - lax.clz(jnp.maximum(x, 1))
    pair_level = jnp.where(si < ti, top_bit, jnp.where(si == ti, -1, -2))
    qscale = dk ** -0.5

    def step(i, carry):
        r0 = pl.multiple_of(i * blk, blk)
        rs = pl.ds(r0, blk)
        q = q_ref[rs, :].astype(F32) * qscale
        k = k_ref[rs, :].astype(F32)
        v = v_ref[rs, :].astype(BF16)
        glr = glr_ref[rs, 0:rank].astype(BF16)
        g = _log_sigmoid(_dot(glr, wg_ref[...]) + bg_ref[...]) * (LOG2_E / GLA_GATE_TAU)
        g_hi, g_lo = _split2(g)
        bc = _dot(tri, g_hi) + _dot(tri, g_lo)
        b_last = bc[blk - 1:blk, :]
        state = state_ref[...]

        o = _dot_nt((q * jnp.exp2(bc)).astype(BF16), state.astype(BF16))

        att = jnp.zeros((blk, blk), F32)
        q_b, k_b = q.astype(BF16), k.astype(BF16)
        for p in range(n_levels):
            e = jnp.exp2(_neg_abs(bc - _ref_rows(bc, 1 << p)).astype(BF16))
            att = jnp.where(pair_level == p, _dot_nt(q_b * e, k_b * e), att)
        att = jnp.where(pair_level == -1, _dot_nt(q_b, k_b), att)
        o = o + _dot(att.astype(BF16), v)

        kd = (k * jnp.exp2(b_last - bc)).astype(BF16)
        state_ref[...] = state * jnp.exp2(b_last) + _dot_tn(v, kd)

        o = o * lax.rsqrt(jnp.mean(o * o, axis=-1, keepdims=True) + EPS) * gn_ref[...]
        og = og_ref[rs, :].astype(F32)
        o_ref[rs, :] = (o * (og * jax.nn.sigmoid(og))).astype(o_ref.dtype)
        return carry

    lax.fori_loop(0, rows // blk, step, 0)


def gla_attention(z, w_gate, b_gate, gn_gain, batch, seq_len, dk, dv, rank):
    m = z.shape[0]
    heads = GLA_HEADS
    t = _pick_tile(seq_len, (1024, 512, 256))
    tiles = seq_len // t
    v0, og0 = 0, (heads * dv) // dv
    q0 = (2 * heads * dv) // dk
    k0 = q0 + heads
    glr0 = (2 * heads * dv + 2 * heads * dk + MEM_HEADS * LANES) // LANES
    rows = lambda b, h, s: b * tiles + s
    kern = functools.partial(_gla_kernel, rank=rank)
    return pl.pallas_call(
        kern,
        grid=(batch, heads, tiles),
        in_specs=[
            pl.BlockSpec((t, dk), lambda b, h, s: (rows(b, h, s), q0 + h)),
            pl.BlockSpec((t, dk), lambda b, h, s: (rows(b, h, s), k0 + h)),
            pl.BlockSpec((t, dv), lambda b, h, s: (rows(b, h, s), v0 + h)),
            pl.BlockSpec((t, LANES), lambda b, h, s: (rows(b, h, s), glr0)),
            pl.BlockSpec((t, dv), lambda b, h, s: (rows(b, h, s), og0 + h)),
            pl.BlockSpec((None, rank, dk), lambda b, h, s: (h, 0, 0)),
            pl.BlockSpec((None, 1, dk), lambda b, h, s: (h, 0, 0)),
            pl.BlockSpec((None, 1, dv), lambda b, h, s: (h, 0, 0)),
        ],
        out_specs=pl.BlockSpec((t, dv), lambda b, h, s: (rows(b, h, s), h)),
        out_shape=jax.ShapeDtypeStruct((m, heads * dv), BF16),
        scratch_shapes=[pltpu.VMEM((dv, dk), F32)],
        compiler_params=_params("parallel", "parallel", "arbitrary"),
        name="gla_attention",
    )(z, z, z, z, z, w_gate, b_gate, gn_gain)


BIAS_PIECES = 3


def _split3(x):
    hi = x.astype(BF16)
    r1 = x - hi.astype(F32)
    mid = r1.astype(BF16)
    lo = (r1 - mid.astype(F32)).astype(BF16)
    return hi, mid, lo


def _fox_gate_kernel(x_ref, g_ref, w_ref, b_ref, o_ref, carry_ref, *, heads):
    s = pl.program_id(1)
    t = x_ref.shape[0]

    @pl.when(s == 0)
    def _():
        carry_ref[...] = jnp.zeros_like(carry_ref)

    h_hi, h_lo = _split2(_rms(x_ref[...], g_ref[...]))
    w = w_ref[...]
    log_f = _log_sigmoid(_dot(h_hi, w) + _dot(h_lo, w) + b_ref[...])
    ti = lax.broadcasted_iota(jnp.int32, (t, t), 0)
    si = lax.broadcasted_iota(jnp.int32, (t, t), 1)
    tri = jnp.where(si <= ti, 1.0, 0.0).astype(BF16)
    f_hi, f_mid, f_lo = _split3(log_f)
    c = carry_ref[...] + _dot(tri, f_hi) + _dot(tri, f_mid) + _dot(tri, f_lo)
    carry_ref[...] = c[t - 1:t, :]
    lane = lax.broadcasted_iota(jnp.int32, (t, LANES), 1)
    for hd in range(heads):
        pieces = _split3(jnp.broadcast_to(c[:, hd:hd + 1] * (-LOG2_E), (t, LANES)))
        slab = jnp.zeros((t, LANES), F32)
        for n, piece in enumerate(pieces):
            slab = jnp.where(lane == n, piece.astype(F32), slab)
        o_ref[:, hd * LANES:(hd + 1) * LANES] = slab.astype(BF16)


def fox_gates(x, gain, w_f, b_f, batch, seq_len):
    m, d = x.shape
    t = _pick_tile(seq_len, (512, 256))
    tiles = seq_len // t
    kern = functools.partial(_fox_gate_kernel, heads=FOX_HEADS)
    return pl.pallas_call(
        kern,
        grid=(batch, tiles),
        in_specs=[
            pl.BlockSpec((t, d), lambda b, s: (b * tiles + s, 0)),
            pl.BlockSpec((1, d), lambda b, s: (0, 0)),
            pl.BlockSpec((d, LANES), lambda b, s: (0, 0)),
            pl.BlockSpec((1, LANES), lambda b, s: (0, 0)),
        ],
        out_specs=pl.BlockSpec((t, FOX_HEADS * LANES), lambda b, s: (b * tiles + s, 0)),
        out_shape=jax.ShapeDtypeStruct((m, FOX_HEADS * LANES), BF16),
        scratch_shapes=[pltpu.VMEM((1, LANES), F32)],
        compiler_params=_params("parallel", "arbitrary"),
        name="fox_gates",
    )(x, gain.reshape(1, d), w_f, b_f)


FOX_HEADS_PER_STEP = 2


def _fox_attn_kernel(q_ref, k_ref, kb_ref, v_ref, o_ref, m_ref, acc_ref, s0_ref, s1_ref, *, heads):
    qi = pl.program_id(2)
    tq = q_ref.shape[0]
    tk = tq // 2
    dh = q_ref.shape[1] // heads
    cols = [slice(hd * dh, (hd + 1) * dh) for hd in range(heads)]
    lane = lax.broadcasted_iota(jnp.int32, (tq, dh), 1)
    bias_taps = jnp.where(lane < BIAS_PIECES, 1.0, 0.0).astype(BF16)
    ones_col = jnp.where(lax.broadcasted_iota(jnp.int32, (tk, dh), 1) == 0, 1.0, 0.0).astype(BF16)
    qs = [jnp.concatenate([(q_ref[:, cols[hd]].astype(F32) * (dh ** -0.5 * LOG2_E)).astype(BF16), bias_taps],
                          axis=1) for hd in range(heads)]

    def key_rows(j):
        return pl.ds(pl.multiple_of(j * tk, tk), tk)

    def issue_scores(s_ref, j, rows):
        ks = key_rows(j)
        for hd in range(heads):
            keys = jnp.concatenate([k_ref[ks, cols[hd]], kb_ref[ks, cols[hd]]], axis=1)
            s_ref[hd, rows, :] = _dot_nt(qs[hd][rows], keys)

    def update(hd, s, j, rows):
        m_prev = m_ref[hd, rows, :]
        m_new = jnp.maximum(m_prev, jnp.max(s, axis=-1, keepdims=True))
        alpha = jnp.exp2(m_prev - m_new)
        p = jnp.concatenate([jnp.exp2((s[:, c:c + LANES] - m_new).astype(BF16))
                             for c in range(0, tk, LANES)], axis=1)
        pv = _dot(p, jnp.concatenate([v_ref[key_rows(j), cols[hd]], ones_col], axis=1))
        acc = acc_ref[hd, rows, :]
        acc_ref[hd, rows, :] = jnp.concatenate([alpha * acc[:, c:c + LANES] for c in range(0, 2 * dh, LANES)],
                                               axis=1) + pv
        m_ref[hd, rows, :] = m_new

    every = slice(0, tq)
    lower = slice(tk, tq)

    m_ref[...] = jnp.full(m_ref.shape, -jnp.inf, F32)
    acc_ref[...] = jnp.zeros_like(acc_ref)
    issue_scores(s0_ref, 0, every)

    def pair(jj, carry):
        j = 2 * jj
        issue_scores(s1_ref, j + 1, every)
        for hd in range(heads):
            update(hd, s0_ref[hd], j, every)
        issue_scores(s0_ref, j + 2, every)
        for hd in range(heads):
            update(hd, s1_ref[hd], j + 1, every)
        return carry

    lax.fori_loop(0, qi, pair, 0)

    issue_scores(s1_ref, 2 * qi + 1, lower)
    ti = lax.broadcasted_iota(jnp.int32, (tq, tk), 0)
    si = lax.broadcasted_iota(jnp.int32, (tq, tk), 1)
    for hd in range(heads):
        update(hd, jnp.where(si <= ti, s0_ref[hd], -jnp.inf), 2 * qi, every)
    tl = lax.broadcasted_iota(jnp.int32, (tk, tk), 0)
    sl = lax.broadcasted_iota(jnp.int32, (tk, tk), 1)
    for hd in range(heads):
        update(hd, jnp.where(sl <= tl, s1_ref[hd, lower, :], -jnp.inf), 2 * qi + 1, lower)
    for hd in range(heads):
        acc = acc_ref[hd]
        o_ref[:, cols[hd]] = (acc[:, 0:dh] / acc[:, dh:dh + 1]).astype(o_ref.dtype)


def fox_attention(zq, kv, kbias, batch, seq_len):
    m = zq.shape[0]
    hp = FOX_HEADS_PER_STEP
    width = hp * LANES
    groups = FOX_HEADS // hp
    tq = _pick_tile(seq_len, (1024, 512, 256))
    tiles = seq_len // tq
    kern = functools.partial(_fox_attn_kernel, heads=hp)
    return pl.pallas_call(
        kern,
        grid=(batch, groups, tiles),
        in_specs=[
            pl.BlockSpec((tq, width), lambda b, h, s: (b * tiles + s, h)),
            pl.BlockSpec((seq_len, width), lambda b, h, s: (b, h)),
            pl.BlockSpec((seq_len, width), lambda b, h, s: (b, h)),
            pl.BlockSpec((seq_len, width), lambda b, h, s: (b, groups + h)),
        ],
        out_specs=pl.BlockSpec((tq, width), lambda b, h, s: (b * tiles + s, h)),
        out_shape=jax.ShapeDtypeStruct((m, FOX_HEADS * LANES), BF16),
        scratch_shapes=[pltpu.VMEM((hp, tq, LANES), F32), pltpu.VMEM((hp, tq, 2 * LANES), F32),
                        pltpu.VMEM((hp, tq, tq // 2), F32), pltpu.VMEM((hp, tq, tq // 2), F32)],
        compiler_params=_params("parallel", "parallel", "arbitrary"),
        name="fox_attention",
    )(zq, kv, kbias, kv)


def kernel(x, mem, norm_mix, norm_ffn, norm_mem, norm_final, mem_w_kv, gla_w_in, gla_w_gate_up,
           gla_b_gate, gla_norm, gla_w_out, fox_kv_norm, fox_w_kv, fox_b_f, fox_w_in, fox_w_out,
           ffn_w_up, ffn_conv_w, ffn_conv_b, ffn_w_down):
    batch, seq_len, d = x.shape
    n_mem = mem.shape[1]
    depth = norm_mix.shape[0]
    n_gla = gla_w_in.shape[0]
    rank = gla_w_gate_up.shape[1]
    qk_width = gla_w_gate_up.shape[2]
    dk = qk_width // GLA_HEADS
    v_width = gla_norm.shape[1]
    dv = v_width // GLA_HEADS
    mem_width = mem_w_kv.shape[2] // 2
    fox_width = (fox_w_kv.shape[1] - FOX_HEADS) // 2
    assert mem_width == MEM_HEADS * LANES and fox_width == FOX_HEADS * LANES
    assert (2 * v_width) % dk == 0 and rank <= LANES

    xf = x.reshape(batch * seq_len, d)

    w_mem = jnp.transpose(mem_w_kv, (1, 0, 2)).reshape(d, depth * 2 * mem_width).astype(BF16)
    mem_kv = norm_matmul(mem.reshape(batch * n_mem, d), norm_mem, w_mem, BF16,
                         _pick_tile(w_mem.shape[1], (1024, 512)))

    fox_kv = fox_c = None
    for i in range(depth):
        if i == n_gla:
            fox_kv = norm_matmul(xf, fox_kv_norm, fox_w_kv[:, :2 * fox_width].astype(BF16), BF16,
                                 _pick_tile(2 * fox_width, (1024, 768, 512)))
            w_f = jnp.pad(fox_w_kv[:, 2 * fox_width:], ((0, 0), (0, LANES - FOX_HEADS))).astype(BF16)
            b_f = jnp.pad(fox_b_f, (0, LANES - FOX_HEADS)).reshape(1, LANES)
            fox_c = fox_gates(xf, fox_kv_norm, w_f, b_f, batch, seq_len)
        if i < n_gla:
            w = gla_w_in[i]
            o_q, o_k, o_v = 0, qk_width, 2 * qk_width
            o_glr = o_v + v_width
            o_og = o_glr + rank
            o_mq = o_og + v_width
            w_in = jnp.concatenate([
                w[:, o_v:o_glr], w[:, o_og:o_mq], w[:, o_q:o_k], w[:, o_k:o_v], w[:, o_mq:],
                jnp.pad(w[:, o_glr:o_og], ((0, 0), (0, LANES - rank)))], axis=1).astype(BF16)
            z = norm_matmul(xf, norm_mix[i], w_in, BF16, _pick_tile(w_in.shape[1], (1920, 1152, 640, 384, 128)))
            a = gla_attention(z, gla_w_gate_up[i].reshape(rank, GLA_HEADS, dk).transpose(1, 0, 2).astype(BF16),
                              gla_b_gate[i].reshape(GLA_HEADS, 1, dk), gla_norm[i].reshape(GLA_HEADS, 1, dv),
                              batch, seq_len, dk, dv, rank)
            mq_block = (2 * v_width + 2 * qk_width) // mem_width
            w_out = gla_w_out[i].astype(BF16)
            split = v_width
        else:
            j = i - n_gla
            z = norm_matmul(xf, norm_mix[i], fox_w_in[j].astype(BF16), BF16,
                            _pick_tile(fox_w_in.shape[2], (1024, 512)))
            a = fox_attention(z, fox_kv, fox_c, batch, seq_len)
            mq_block = fox_width // mem_width
            w_out = fox_w_out[j].astype(BF16)
            split = fox_width
        mo = mem_attention(z, mq_block, mem_kv, i, batch, seq_len, n_mem)
        xf = out_proj(a, mo, w_out[:split], w_out[split:], xf)
        xf = conv_ffn(xf, norm_ffn[i], ffn_w_up[i].astype(BF16), ffn_conv_w[i], ffn_conv_b[i],
                      ffn_w_down[i].astype(BF16), norm_final, seq_len, final_norm=(i == depth - 1))
    return xf.reshape(batch, seq_len, d)
```

```python
import functools
import math

import jax
import jax.numpy as jnp
from jax import lax
from jax.experimental import pallas as pl
from jax.experimental.pallas import tpu as pltpu

GLA_HEADS = 4
GLA_GATE_TAU = 16.0
FOX_HEADS = 12
MEM_HEADS = 4
CONV_WIDTH = 3
EPS = 1e-6
LOG2_E = math.log2(math.e)

LANES = 128
SUBLANES_F32 = 8
SUBLANES_BF16 = 16
VMEM_LIMIT = 56 * 1024 * 1024

BF16 = jnp.bfloat16
F32 = jnp.float32

NT_DIMS = (((1,), (1,)), ((), ()))
TN_DIMS = (((0,), (0,)), ((), ()))


def _params(*sem):
    return pltpu.CompilerParams(dimension_semantics=sem, vmem_limit_bytes=VMEM_LIMIT)


def _rms(x, gain):
    return x * lax.rsqrt(jnp.mean(x * x, axis=-1, keepdims=True) + EPS) * gain


def _dot(a, b):
    return jnp.dot(a, b, preferred_element_type=F32)


def _dot_nt(a, b):
    return lax.dot_general(a, b, NT_DIMS, preferred_element_type=F32)


def _dot_tn(a, b):
    return lax.dot_general(a, b, TN_DIMS, preferred_element_type=F32)


def _split2(x):
    hi = x.astype(BF16)
    lo = (x - hi.astype(F32)).astype(BF16)
    return hi, lo


def _neg_abs(x):
    bits = lax.bitcast_convert_type(x, jnp.uint32) | jnp.uint32(0x80000000)
    return lax.bitcast_convert_type(bits, F32)


def _log_sigmoid(x):
    return jnp.minimum(x, 0.0) - jnp.log1p(jnp.exp(-jnp.abs(x)))


def _pick_tile(n, preferred):
    for t in preferred:
        if n % t == 0:
            return t
    return n


def _norm_matmul_kernel(x_ref, g_ref, w_ref, o_ref, *, tn):
    h = _rms(x_ref[...], g_ref[...]).astype(BF16)
    for c in range(0, o_ref.shape[1], tn):
        o_ref[:, c:c + tn] = _dot(h, w_ref[:, c:c + tn]).astype(o_ref.dtype)


def norm_matmul(x, gain, w, out_dtype, tn):
    m, d = x.shape
    n = w.shape[1]
    out_bytes = jnp.dtype(out_dtype).itemsize

    def vmem_need(tm):
        return 2 * tm * d * 4 + d * n * 2 + 2 * tm * n * out_bytes + tm * d * 2 + tm * tn * 4

    tm = next(t for t in (1024, 512, 256, 128) if m % t == 0 and vmem_need(t) <= 0.9 * VMEM_LIMIT)
    return pl.pallas_call(
        functools.partial(_norm_matmul_kernel, tn=tn),
        grid=(m // tm,),
        in_specs=[
            pl.BlockSpec((tm, d), lambda i: (i, 0)),
            pl.BlockSpec((1, d), lambda i: (0, 0)),
            pl.BlockSpec((d, n), lambda i: (0, 0), pipeline_mode=pl.Buffered(1)),
        ],
        out_specs=pl.BlockSpec((tm, n), lambda i: (i, 0)),
        out_shape=jax.ShapeDtypeStruct((m, n), out_dtype),
        compiler_params=_params("parallel"),
        name="norm_matmul",
    )(x, gain.reshape(1, d), w)


def _out_proj_kernel(a_ref, b_ref, wa_ref, wb_ref, x_ref, o_ref):
    o_ref[...] = x_ref[...] + _dot(a_ref[...], wa_ref[...]) + _dot(b_ref[...], wb_ref[...])


def out_proj(a, b, wa, wb, x):
    m, d = x.shape
    ka, kb = a.shape[1], b.shape[1]
    tm = _pick_tile(m, (1024, 512, 256))
    resident = pl.Buffered(1)
    return pl.pallas_call(
        _out_proj_kernel,
        grid=(m // tm,),
        in_specs=[
            pl.BlockSpec((tm, ka), lambda i: (i, 0)),
            pl.BlockSpec((tm, kb), lambda i: (i, 0)),
            pl.BlockSpec((ka, d), lambda i: (0, 0), pipeline_mode=resident),
            pl.BlockSpec((kb, d), lambda i: (0, 0), pipeline_mode=resident),
            pl.BlockSpec((tm, d), lambda i: (i, 0)),
        ],
        out_specs=pl.BlockSpec((tm, d), lambda i: (i, 0)),
        out_shape=jax.ShapeDtypeStruct((m, d), F32),
        compiler_params=_params("parallel"),
        name="out_proj",
    )(a, b, wa, wb, x)


HALO = SUBLANES_BF16
FFN_CHUNK_PARTS = (1, 1)


def _ffn_kernel(x_ref, xh_ref, g_ref, wa_ref, wv_ref, cwa_ref, cwv_ref, cba_ref, cbv_ref, wd_ref,
                gf_ref, o_ref, h_ref, *, tiles_per_seq, final_norm):
    i = pl.program_id(0)
    f = pl.program_id(1)
    tm = x_ref.shape[0]

    @pl.when(f == 0)
    def _():
        x = x_ref[...]
        prev = _rms(xh_ref[...], g_ref[...])
        prev = jnp.where(i % tiles_per_seq == 0, 0.0, prev)
        h_ref[0:HALO, :] = prev.astype(BF16)
        h_ref[HALO:HALO + tm, :] = _rms(x, g_ref[...]).astype(BF16)
        o_ref[...] = x

    def conv(u, cw_ref, cb_ref):
        cw = cw_ref[...]
        y = cw[2:3, :] * u + cw[1:2, :] * pltpu.roll(u, 1, axis=0) + cw[0:1, :] * pltpu.roll(u, 2, axis=0)
        return y[HALO:, :] + cb_ref[...]

    def up(lo, rc):
        h = h_ref[lo:lo + rc + HALO, :]
        return _dot(h, wa_ref[...]), _dot(h, wv_ref[...])

    sizes = [c * tm // sum(FFN_CHUNK_PARTS) for c in FFN_CHUNK_PARTS]
    starts = [sum(sizes[:n]) for n in range(len(sizes))]
    u_next = up(starts[0], sizes[0])
    for n, (lo, rc) in enumerate(zip(starts, sizes)):
        ua, uv = u_next
        if n + 1 < len(starts):
            u_next = up(starts[n + 1], sizes[n + 1])
        ya = conv(ua, cwa_ref, cba_ref)
        yv = conv(uv, cwv_ref, cbv_ref)
        act = (ya * jax.nn.sigmoid(ya) * yv).astype(BF16)
        o_ref[lo:lo + rc, :] += _dot(act, wd_ref[...])

    if final_norm:
        @pl.when(f == pl.num_programs(1) - 1)
        def _():
            o_ref[...] = _rms(o_ref[...], gf_ref[...])


def conv_ffn(x, gain, w_up, conv_w, conv_b, w_down, final_gain, seq_len, final_norm):
    m, d = x.shape
    hidden = w_down.shape[0]
    tm = _pick_tile(seq_len, (1024, 512, 256))
    tf = _pick_tile(hidden, (512, 256, 128))
    nf = hidden // tf
    halo_blocks = tm // HALO
    assert tm % (sum(FFN_CHUNK_PARTS) * SUBLANES_BF16) == 0
    w_tiles = w_up.reshape(d, 2, nf, tf).transpose(1, 2, 0, 3)
    kern = functools.partial(_ffn_kernel, tiles_per_seq=seq_len // tm, final_norm=final_norm)
    return pl.pallas_call(
        kern,
        grid=(m // tm, nf),
        in_specs=[
            pl.BlockSpec((tm, d), lambda i, f: (i, 0)),
            pl.BlockSpec((HALO, d), lambda i, f: (jnp.maximum(i * halo_blocks - 1, 0), 0)),
            pl.BlockSpec((1, d), lambda i, f: (0, 0)),
            pl.BlockSpec((None, None, d, tf), lambda i, f: (0, f, 0, 0)),
            pl.BlockSpec((None, None, d, tf), lambda i, f: (1, f, 0, 0)),
            pl.BlockSpec((CONV_WIDTH, tf), lambda i, f: (0, f)),
            pl.BlockSpec((CONV_WIDTH, tf), lambda i, f: (0, nf + f)),
            pl.BlockSpec((1, tf), lambda i, f: (0, f)),
            pl.BlockSpec((1, tf), lambda i, f: (0, nf + f)),
            pl.BlockSpec((tf, d), lambda i, f: (f, 0)),
            pl.BlockSpec((1, d), lambda i, f: (0, 0)),
        ],
        out_specs=pl.BlockSpec((tm, d), lambda i, f: (i, 0)),
        out_shape=jax.ShapeDtypeStruct((m, d), F32),
        scratch_shapes=[pltpu.VMEM((tm + HALO, d), BF16)],
        compiler_params=_params("parallel", "arbitrary"),
        name="conv_ffn",
    )(x, x, gain.reshape(1, d), w_tiles, w_tiles, conv_w, conv_w, conv_b.reshape(1, -1),
      conv_b.reshape(1, -1), w_down, final_gain.reshape(1, d))


def _mem_attn_kernel(q_ref, mk_ref, mv_ref, o_ref, *, heads):
    dh = q_ref.shape[1] // heads
    scale = dh ** -0.5
    for hd in range(heads):
        cols = slice(hd * dh, (hd + 1) * dh)
        q = q_ref[:, cols].astype(BF16)
        s = _dot_nt(q, mk_ref[:, cols]) * scale
        s = s - jnp.max(s, axis=-1, keepdims=True)
        p = jnp.exp(s)
        p = p / jnp.sum(p, axis=-1, keepdims=True)
        o_ref[:, cols] = _dot(p.astype(BF16), mv_ref[:, cols]).astype(o_ref.dtype)


def mem_attention(z, q_col_block, mem_kv, layer, batch, seq_len, n_mem):
    m = z.shape[0]
    mem_width = MEM_HEADS * LANES
    t = _pick_tile(seq_len, (1024, 512, 256))
    tiles = seq_len // t
    kern = functools.partial(_mem_attn_kernel, heads=MEM_HEADS)
    return pl.pallas_call(
        kern,
        grid=(batch, tiles),
        in_specs=[
            pl.BlockSpec((t, mem_width), lambda b, s: (b * tiles + s, q_col_block)),
            pl.BlockSpec((n_mem, mem_width), lambda b, s: (b, 2 * layer)),
            pl.BlockSpec((n_mem, mem_width), lambda b, s: (b, 2 * layer + 1)),
        ],
        out_specs=pl.BlockSpec((t, mem_width), lambda b, s: (b * tiles + s, 0)),
        out_shape=jax.ShapeDtypeStruct((m, mem_width), BF16),
        compiler_params=_params("parallel", "arbitrary"),
        name="mem_attention",
    )(z, mem_kv, mem_kv)


GLA_BLOCK = 256


def _ref_rows(bc, half):
    rows, dk = bc.shape
    if half >= SUBLANES_F32:
        nb = rows // (2 * half)
        b3 = bc.reshape(nb, 2 * half, dk)
        return jnp.broadcast_to(b3[:, half:half + 1, :], b3.shape).reshape(rows, dk)
    b3 = bc.reshape(rows // SUBLANES_F32, SUBLANES_F32, dk)
    sub = lax.broadcasted_iota(jnp.int32, b3.shape, 1)
    out = None
    for mid in range(half, SUBLANES_F32, 2 * half):
        cand = jnp.broadcast_to(b3[:, mid:mid + 1, :], b3.shape)
        out = cand if out is None else jnp.where(sub >= mid - half, cand, out)
    return out.reshape(rows, dk)


def _gla_kernel(q_ref, k_ref, v_ref, glr_ref, og_ref, wg_ref, bg_ref, gn_ref, o_ref, state_ref, *, rank):
    t = pl.program_id(2)
    rows = q_ref.shape[0]
    dk = q_ref.shape[1]
    blk = GLA_BLOCK if rows % GLA_BLOCK == 0 else rows
    n_levels = int(math.log2(blk))

    @pl.when(t == 0)
    def _():
        state_ref[...] = jnp.zeros_like(state_ref)

    ti = lax.broadcasted_iota(jnp.int32, (blk, blk), 0)
    si = lax.broadcasted_iota(jnp.int32, (blk, blk), 1)
    tri = jnp.where(si <= ti, 1.0, 0.0).astype(BF16)
    x = ti ^ si
    top_bit = 31 - lax.clz(jnp.maximum(x, 1))
    pair_level = jnp.where(si < ti, top_bit, jnp.where(si == ti, -1, -2))
    qscale = dk ** -0.5

    def step(i, carry):
        r0 = pl.multiple_of(i * blk, blk)
        rs = pl.ds(r0, blk)
        q = q_ref[rs, :].astype(F32) * qscale
        k = k_ref[rs, :].astype(F32)
        v = v_ref[rs, :].astype(BF16)
        glr = glr_ref[rs, 0:rank].astype(BF16)
        g = _log_sigmoid(_dot(glr, wg_ref[...]) + bg_ref[...]) * (LOG2_E / GLA_GATE_TAU)
        g_hi, g_lo = _split2(g)
        bc = _dot(tri, g_hi) + _dot(tri, g_lo)
        b_last = bc[blk - 1:blk, :]
        state = state_ref[...]

        o = _dot_nt((q * jnp.exp2(bc)).astype(BF16), state.astype(BF16))

        att = jnp.zeros((blk, blk), F32)
        q_b, k_b = q.astype(BF16), k.astype(BF16)
        for p in range(n_levels):
            e = jnp.exp2(_neg_abs(bc - _ref_rows(bc, 1 << p)).astype(BF16))
            att = jnp.where(pair_level == p, _dot_nt(q_b * e, k_b * e), att)
        att = jnp.where(pair_level == -1, _dot_nt(q_b, k_b), att)
        o = o + _dot(att.astype(BF16), v)

        kd = (k * jnp.exp2(b_last - bc)).astype(BF16)
        state_ref[...] = state * jnp.exp2(b_last) + _dot_tn(v, kd)

        o = o * lax.rsqrt(jnp.mean(o * o, axis=-1, keepdims=True) + EPS) * gn_ref[...]
        og = og_ref[rs, :].astype(F32)
        o_ref[rs, :] = (o * (og * jax.nn.sigmoid(og))).astype(o_ref.dtype)
        return carry

    lax.fori_loop(0, rows // blk, step, 0)


def gla_attention(z, w_gate, b_gate, gn_gain, batch, seq_len, dk, dv, rank):
    m = z.shape[0]
    heads = GLA_HEADS
    t = _pick_tile(seq_len, (1024, 512, 256))
    tiles = seq_len // t
    v0, og0 = 0, (heads * dv) // dv
    q0 = (2 * heads * dv) // dk
    k0 = q0 + heads
    glr0 = (2 * heads * dv + 2 * heads * dk + MEM_HEADS * LANES) // LANES
    rows = lambda b, h, s: b * tiles + s
    kern = functools.partial(_gla_kernel, rank=rank)
    return pl.pallas_call(
        kern,
        grid=(batch, heads, tiles),
        in_specs=[
            pl.BlockSpec((t, dk), lambda b, h, s: (rows(b, h, s), q0 + h)),
            pl.BlockSpec((t, dk), lambda b, h, s: (rows(b, h, s), k0 + h)),
            pl.BlockSpec((t, dv), lambda b, h, s: (rows(b, h, s), v0 + h)),
            pl.BlockSpec((t, LANES), lambda b, h, s: (rows(b, h, s), glr0)),
            pl.BlockSpec((t, dv), lambda b, h, s: (rows(b, h, s), og0 + h)),
            pl.BlockSpec((None, rank, dk), lambda b, h, s: (h, 0, 0)),
            pl.BlockSpec((None, 1, dk), lambda b, h, s: (h, 0, 0)),
            pl.BlockSpec((None, 1, dv), lambda b, h, s: (h, 0, 0)),
        ],
        out_specs=pl.BlockSpec((t, dv), lambda b, h, s: (rows(b, h, s), h)),
        out_shape=jax.ShapeDtypeStruct((m, heads * dv), BF16),
        scratch_shapes=[pltpu.VMEM((dv, dk), F32)],
        compiler_params=_params("parallel", "parallel", "arbitrary"),
        name="gla_attention",
    )(z, z, z, z, z, w_gate, b_gate, gn_gain)


BIAS_PIECES = 3


def _split3(x):
    hi = x.astype(BF16)
    r1 = x - hi.astype(F32)
    mid = r1.astype(BF16)
    lo = (r1 - mid.astype(F32)).astype(BF16)
    return hi, mid, lo


def _fox_gate_kernel(x_ref, g_ref, w_ref, b_ref, o_ref, carry_ref, *, heads):
    s = pl.program_id(1)
    t = x_ref.shape[0]

    @pl.when(s == 0)
    def _():
        carry_ref[...] = jnp.zeros_like(carry_ref)

    h_hi, h_lo = _split2(_rms(x_ref[...], g_ref[...]))
    w = w_ref[...]
    log_f = _log_sigmoid(_dot(h_hi, w) + _dot(h_lo, w) + b_ref[...])
    ti = lax.broadcasted_iota(jnp.int32, (t, t), 0)
    si = lax.broadcasted_iota(jnp.int32, (t, t), 1)
    tri = jnp.where(si <= ti, 1.0, 0.0).astype(BF16)
    f_hi, f_mid, f_lo = _split3(log_f)
    c = carry_ref[...] + _dot(tri, f_hi) + _dot(tri, f_mid) + _dot(tri, f_lo)
    carry_ref[...] = c[t - 1:t, :]
    lane = lax.broadcasted_iota(jnp.int32, (t, LANES), 1)
    for hd in range(heads):
        pieces = _split3(jnp.broadcast_to(c[:, hd:hd + 1] * (-LOG2_E), (t, LANES)))
        slab = jnp.zeros((t, LANES), F32)
        for n, piece in enumerate(pieces):
            slab = jnp.where(lane == n, piece.astype(F32), slab)
        o_ref[:, hd * LANES:(hd + 1) * LANES] = slab.astype(BF16)


def fox_gates(x, gain, w_f, b_f, batch, seq_len):
    m, d = x.shape
    t = _pick_tile(seq_len, (512, 256))
    tiles = seq_len // t
    kern = functools.partial(_fox_gate_kernel, heads=FOX_HEADS)
    return pl.pallas_call(
        kern,
        grid=(batch, tiles),
        in_specs=[
            pl.BlockSpec((t, d), lambda b, s: (b * tiles + s, 0)),
            pl.BlockSpec((1, d), lambda b, s: (0, 0)),
            pl.BlockSpec((d, LANES), lambda b, s: (0, 0)),
            pl.BlockSpec((1, LANES), lambda b, s: (0, 0)),
        ],
        out_specs=pl.BlockSpec((t, FOX_HEADS * LANES), lambda b, s: (b * tiles + s, 0)),
        out_shape=jax.ShapeDtypeStruct((m, FOX_HEADS * LANES), BF16),
        scratch_shapes=[pltpu.VMEM((1, LANES), F32)],
        compiler_params=_params("parallel", "arbitrary"),
        name="fox_gates",
    )(x, gain.reshape(1, d), w_f, b_f)


FOX_HEADS_PER_STEP = 2


def _fox_attn_kernel(q_ref, k_ref, kb_ref, v_ref, o_ref, m_ref, acc_ref, s0_ref, s1_ref, *, heads):
    qi = pl.program_id(2)
    tq = q_ref.shape[0]
    tk = tq // 2
    dh = q_ref.shape[1] // heads
    cols = [slice(hd * dh, (hd + 1) * dh) for hd in range(heads)]
    lane = lax.broadcasted_iota(jnp.int32, (tq, dh), 1)
    bias_taps = jnp.where(lane < BIAS_PIECES, 1.0, 0.0).astype(BF16)
    ones_col = jnp.where(lax.broadcasted_iota(jnp.int32, (tk, dh), 1) == 0, 1.0, 0.0).astype(BF16)
    qs = [jnp.concatenate([(q_ref[:, cols[hd]].astype(F32) * (dh ** -0.5 * LOG2_E)).astype(BF16), bias_taps],
                          axis=1) for hd in range(heads)]

    def key_rows(j):
        return pl.ds(pl.multiple_of(j * tk, tk), tk)

    def issue_scores(s_ref, j, rows):
        ks = key_rows(j)
        for hd in range(heads):
            keys = jnp.concatenate([k_ref[ks, cols[hd]], kb_ref[ks, cols[hd]]], axis=1)
            s_ref[hd, rows, :] = _dot_nt(qs[hd][rows], keys)

    def update(hd, s, j, rows):
        m_prev = m_ref[hd, rows, :]
        m_new = jnp.maximum(m_prev, jnp.max(s, axis=-1, keepdims=True))
        alpha = jnp.exp2(m_prev - m_new)
        p = jnp.concatenate([jnp.exp2((s[:, c:c + LANES] - m_new).astype(BF16))
                             for c in range(0, tk, LANES)], axis=1)
        pv = _dot(p, jnp.concatenate([v_ref[key_rows(j), cols[hd]], ones_col], axis=1))
        acc = acc_ref[hd, rows, :]
        acc_ref[hd, rows, :] = jnp.concatenate([alpha * acc[:, c:c + LANES] for c in range(0, 2 * dh, LANES)],
                                               axis=1) + pv
        m_ref[hd, rows, :] = m_new

    every = slice(0, tq)
    lower = slice(tk, tq)

    m_ref[...] = jnp.full(m_ref.shape, -jnp.inf, F32)
    acc_ref[...] = jnp.zeros_like(acc_ref)
    issue_scores(s0_ref, 0, every)

    def pair(jj, carry):
        j = 2 * jj
        issue_scores(s1_ref, j + 1, every)
        for hd in range(heads):
            update(hd, s0_ref[hd], j, every)
        issue_scores(s0_ref, j + 2, every)
        for hd in range(heads):
            update(hd, s1_ref[hd], j + 1, every)
        return carry

    lax.fori_loop(0, qi, pair, 0)

    issue_scores(s1_ref, 2 * qi + 1, lower)
    ti = lax.broadcasted_iota(jnp.int32, (tq, tk), 0)
    si = lax.broadcasted_iota(jnp.int32, (tq, tk), 1)
    for hd in range(heads):
        update(hd, jnp.where(si <= ti, s0_ref[hd], -jnp.inf), 2 * qi, every)
    tl = lax.broadcasted_iota(jnp.int32, (tk, tk), 0)
    sl = lax.broadcasted_iota(jnp.int32, (tk, tk), 1)
    for hd in range(heads):
        update(hd, jnp.where(sl <= tl, s1_ref[hd, lower, :], -jnp.inf), 2 * qi + 1, lower)
    for hd in range(heads):
        acc = acc_ref[hd]
        o_ref[:, cols[hd]] = (acc[:, 0:dh] / acc[:, dh:dh + 1]).astype(o_ref.dtype)


def fox_attention(zq, kv, kbias, batch, seq_len):
    m = zq.shape[0]
    hp = FOX_HEADS_PER_STEP
    width = hp * LANES
    groups = FOX_HEADS // hp
    tq = _pick_tile(seq_len, (1024, 512, 256))
    tiles = seq_len // tq
    kern = functools.partial(_fox_attn_kernel, heads=hp)
    return pl.pallas_call(
        kern,
        grid=(batch, groups, tiles),
        in_specs=[
            pl.BlockSpec((tq, width), lambda b, h, s: (b * tiles + s, h)),
            pl.BlockSpec((seq_len, width), lambda b, h, s: (b, h)),
            pl.BlockSpec((seq_len, width), lambda b, h, s: (b, h)),
            pl.BlockSpec((seq_len, width), lambda b, h, s: (b, groups + h)),
        ],
        out_specs=pl.BlockSpec((tq, width), lambda b, h, s: (b * tiles + s, h)),
        out_shape=jax.ShapeDtypeStruct((m, FOX_HEADS * LANES), BF16),
        scratch_shapes=[pltpu.VMEM((hp, tq, LANES), F32), pltpu.VMEM((hp, tq, 2 * LANES), F32),
                        pltpu.VMEM((hp, tq, tq // 2), F32), pltpu.VMEM((hp, tq, tq // 2), F32)],
        compiler_params=_params("parallel", "parallel", "arbitrary"),
        name="fox_attention",
    )(zq, kv, kbias, kv)


def kernel(x, mem, norm_mix, norm_ffn, norm_mem, norm_final, mem_w_kv, gla_w_in, gla_w_gate_up,
           gla_b_gate, gla_norm, gla_w_out, fox_kv_norm, fox_w_kv, fox_b_f, fox_w_in, fox_w_out,
           ffn_w_up, ffn_conv_w, ffn_conv_b, ffn_w_down):
    batch, seq_len, d = x.shape
    n_mem = mem.shape[1]
    depth = norm_mix.shape[0]
    n_gla = gla_w_in.shape[0]
    rank = gla_w_gate_up.shape[1]
    qk_width = gla_w_gate_up.shape[2]
    dk = qk_width // GLA_HEADS
    v_width = gla_norm.shape[1]
    dv = v_width // GLA_HEADS
    mem_width = mem_w_kv.shape[2] // 2
    fox_width = (fox_w_kv.shape[1] - FOX_HEADS) // 2
    assert mem_width == MEM_HEADS * LANES and fox_width == FOX_HEADS * LANES
    assert (2 * v_width) % dk == 0 and rank <= LANES

    xf = x.reshape(batch * seq_len, d)

    w_mem = jnp.transpose(mem_w_kv, (1, 0, 2)).reshape(d, depth * 2 * mem_width).astype(BF16)
    mem_kv = norm_matmul(mem.reshape(batch * n_mem, d), norm_mem, w_mem, BF16,
                         _pick_tile(w_mem.shape[1], (1024, 512)))

    fox_kv = fox_c = None
    for i in range(depth):
        if i == n_gla:
            fox_kv = norm_matmul(xf, fox_kv_norm, fox_w_kv[:, :2 * fox_width].astype(BF16), BF16,
                                 _pick_tile(2 * fox_width, (1024, 768, 512)))
            w_f = jnp.pad(fox_w_kv[:, 2 * fox_width:], ((0, 0), (0, LANES - FOX_HEADS))).astype(BF16)
            b_f = jnp.pad(fox_b_f, (0, LANES - FOX_HEADS)).reshape(1, LANES)
            fox_c = fox_gates(xf, fox_kv_norm, w_f, b_f, batch, seq_len)
        if i < n_gla:
            w = gla_w_in[i]
            o_q, o_k, o_v = 0, qk_width, 2 * qk_width
            o_glr = o_v + v_width
            o_og = o_glr + rank
            o_mq = o_og + v_width
            w_in = jnp.concatenate([
                w[:, o_v:o_glr], w[:, o_og:o_mq], w[:, o_q:o_k], w[:, o_k:o_v], w[:, o_mq:],
                jnp.pad(w[:, o_glr:o_og], ((0, 0), (0, LANES - rank)))], axis=1).astype(BF16)
            z = norm_matmul(xf, norm_mix[i], w_in, BF16, _pick_tile(w_in.shape[1], (1920, 1152, 640, 384, 128)))
            a = gla_attention(z, gla_w_gate_up[i].reshape(rank, GLA_HEADS, dk).transpose(1, 0, 2).astype(BF16),
                              gla_b_gate[i].reshape(GLA_HEADS, 1, dk), gla_norm[i].reshape(GLA_HEADS, 1, dv),
                              batch, seq_len, dk, dv, rank)
            mq_block = (2 * v_width + 2 * qk_width) // mem_width
            w_out = gla_w_out[i].astype(BF16)
            split = v_width
        else:
            j = i - n_gla
            z = norm_matmul(xf, norm_mix[i], fox_w_in[j].astype(BF16), BF16,
                            _pick_tile(fox_w_in.shape[2], (1024, 512)))
            a = fox_attention(z, fox_kv, fox_c, batch, seq_len)
            mq_block = fox_width // mem_width
            w_out = fox_w_out[j].astype(BF16)
            split = fox_width
        mo = mem_attention(z, mq_block, mem_kv, i, batch, seq_len, n_mem)
        xf = out_proj(a, mo, w_out[:split], w_out[split:], xf)
        xf = conv_ffn(xf, norm_ffn[i], ffn_w_up[i].astype(BF16), ffn_conv_w[i], ffn_conv_b[i],
                      ffn_w_down[i].astype(BF16), norm_final, seq_len, final_norm=(i == depth - 1))
    return xf.reshape(batch, seq_len, d)
```

```python
import functools
import math

import jax
import jax.numpy as jnp
from jax import lax
from jax.experimental import pallas as pl
from jax.experimental.pallas import tpu as pltpu

GLA_HEADS = 4
GLA_GATE_TAU = 16.0
FOX_HEADS = 12
MEM_HEADS = 4
CONV_WIDTH = 3
EPS = 1e-6
LOG2_E = math.log2(math.e)

LANES = 128
SUBLANES_F32 = 8
SUBLANES_BF16 = 16
VMEM_LIMIT = 56 * 1024 * 1024

BF16 = jnp.bfloat16
F32 = jnp.float32

NT_DIMS = (((1,), (1,)), ((), ()))
TN_DIMS = (((0,), (0,)), ((), ()))


def _params(*sem):
    return pltpu.CompilerParams(dimension_semantics=sem, vmem_limit_bytes=VMEM_LIMIT)


def _rms(x, gain):
    return x * lax.rsqrt(jnp.mean(x * x, axis=-1, keepdims=True) + EPS) * gain


def _dot(a, b):
    return jnp.dot(a, b, preferred_element_type=F32)


def _dot_nt(a, b):
    return lax.dot_general(a, b, NT_DIMS, preferred_element_type=F32)


def _dot_tn(a, b):
    return lax.dot_general(a, b, TN_DIMS, preferred_element_type=F32)


def _split2(x):
    hi = x.astype(BF16)
    lo = (x - hi.astype(F32)).astype(BF16)
    return hi, lo


def _neg_abs(x):
    bits = lax.bitcast_convert_type(x, jnp.uint32) | jnp.uint32(0x80000000)
    return lax.bitcast_convert_type(bits, F32)


def _log_sigmoid(x):
    return jnp.minimum(x, 0.0) - jnp.log1p(jnp.exp(-jnp.abs(x)))


def _pick_tile(n, preferred):
    for t in preferred:
        if n % t == 0:
            return t
    return n


def _norm_matmul_kernel(x_ref, g_ref, w_ref, o_ref, *, tn):
    h = _rms(x_ref[...], g_ref[...]).astype(BF16)
    for c in range(0, o_ref.shape[1], tn):
        o_ref[:, c:c + tn] = _dot(h, w_ref[:, c:c + tn]).astype(o_ref.dtype)


def norm_matmul(x, gain, w, out_dtype, tn):
    m, d = x.shape
    n = w.shape[1]
    out_bytes = jnp.dtype(out_dtype).itemsize

    def vmem_need(tm):
        return 2 * tm * d * 4 + d * n * 2 + 2 * tm * n * out_bytes + tm * d * 2 + tm * tn * 4

    tm = next(t for t in (1024, 512, 256, 128) if m % t == 0 and vmem_need(t) <= 0.9 * VMEM_LIMIT)
    return pl.pallas_call(
        functools.partial(_norm_matmul_kernel, tn=tn),
        grid=(m // tm,),
        in_specs=[
            pl.BlockSpec((tm, d), lambda i: (i, 0)),
            pl.BlockSpec((1, d), lambda i: (0, 0)),
            pl.BlockSpec((d, n), lambda i: (0, 0), pipeline_mode=pl.Buffered(1)),
        ],
        out_specs=pl.BlockSpec((tm, n), lambda i: (i, 0)),
        out_shape=jax.ShapeDtypeStruct((m, n), out_dtype),
        compiler_params=_params("parallel"),
        name="norm_matmul",
    )(x, gain.reshape(1, d), w)


def _out_proj_kernel(a_ref, b_ref, w_ref, x_ref, o_ref):
    ka = a_ref.shape[1]
    o_ref[...] = x_ref[...] + _dot(a_ref[...], w_ref[0:ka, :]) + _dot(b_ref[...], w_ref[ka:, :])


def out_proj(a, b, w, x):
    m, d = x.shape
    ka, kb = a.shape[1], b.shape[1]
    tm = _pick_tile(m, (1024, 512, 256))
    return pl.pallas_call(
        _out_proj_kernel,
        grid=(m // tm,),
        in_specs=[
            pl.BlockSpec((tm, ka), lambda i: (i, 0)),
            pl.BlockSpec((tm, kb), lambda i: (i, 0)),
            pl.BlockSpec((ka + kb, d), lambda i: (0, 0), pipeline_mode=pl.Buffered(1)),
            pl.BlockSpec((tm, d), lambda i: (i, 0)),
        ],
        out_specs=pl.BlockSpec((tm, d), lambda i: (i, 0)),
        out_shape=jax.ShapeDtypeStruct((m, d), F32),
        compiler_params=_params("parallel"),
        name="out_proj",
    )(a, b, w, x)


HALO = SUBLANES_BF16
FFN_CHUNK_PARTS = (1, 1)


def _ffn_kernel(x_ref, xh_ref, g_ref, wa_ref, wv_ref, cwa_ref, cwv_ref, cba_ref, cbv_ref, wd_ref,
                gf_ref, o_ref, h_ref, *, tiles_per_seq, final_norm):
    i = pl.program_id(0)
    f = pl.program_id(1)
    tm = x_ref.shape[0]

    @pl.when(f == 0)
    def _():
        x = x_ref[...]
        prev = _rms(xh_ref[...], g_ref[...])
        prev = jnp.where(i % tiles_per_seq == 0, 0.0, prev)
        h_ref[0:HALO, :] = prev.astype(BF16)
        h_ref[HALO:HALO + tm, :] = _rms(x, g_ref[...]).astype(BF16)
        o_ref[...] = x

    def conv(u, cw_ref, cb_ref):
        cw = cw_ref[...]
        y = cw[2:3, :] * u + cw[1:2, :] * pltpu.roll(u, 1, axis=0) + cw[0:1, :] * pltpu.roll(u, 2, axis=0)
        return y[HALO:, :] + cb_ref[...]

    def up(lo, rc):
        h = h_ref[lo:lo + rc + HALO, :]
        return _dot(h, wa_ref[...]), _dot(h, wv_ref[...])

    sizes = [c * tm // sum(FFN_CHUNK_PARTS) for c in FFN_CHUNK_PARTS]
    starts = [sum(sizes[:n]) for n in range(len(sizes))]
    u_next = up(starts[0], sizes[0])
    for n, (lo, rc) in enumerate(zip(starts, sizes)):
        ua, uv = u_next
        if n + 1 < len(starts):
            u_next = up(starts[n + 1], sizes[n + 1])
        ya = conv(ua, cwa_ref, cba_ref)
        yv = conv(uv, cwv_ref, cbv_ref)
        act = (ya * jax.nn.sigmoid(ya) * yv).astype(BF16)
        o_ref[lo:lo + rc, :] += _dot(act, wd_ref[...])

    if final_norm:
        @pl.when(f == pl.num_programs(1) - 1)
        def _():
            o_ref[...] = _rms(o_ref[...], gf_ref[...])


def conv_ffn(x, gain, w_up, conv_w, conv_b, w_down, final_gain, seq_len, final_norm):
    m, d = x.shape
    hidden = w_down.shape[0]
    tm = _pick_tile(seq_len, (1024, 512, 256))
    tf = _pick_tile(hidden, (512, 256, 128))
    nf = hidden // tf
    halo_blocks = tm // HALO
    assert tm % (sum(FFN_CHUNK_PARTS) * SUBLANES_BF16) == 0
    w_tiles = w_up.reshape(d, 2, nf, tf).transpose(1, 2, 0, 3)
    kern = functools.partial(_ffn_kernel, tiles_per_seq=seq_len // tm, final_norm=final_norm)
    return pl.pallas_call(
        kern,
        grid=(m // tm, nf),
        in_specs=[
            pl.BlockSpec((tm, d), lambda i, f: (i, 0)),
            pl.BlockSpec((HALO, d), lambda i, f: (jnp.maximum(i * halo_blocks - 1, 0), 0)),
            pl.BlockSpec((1, d), lambda i, f: (0, 0)),
            pl.BlockSpec((None, None, d, tf), lambda i, f: (0, f, 0, 0)),
            pl.BlockSpec((None, None, d, tf), lambda i, f: (1, f, 0, 0)),
            pl.BlockSpec((CONV_WIDTH, tf), lambda i, f: (0, f)),
            pl.BlockSpec((CONV_WIDTH, tf), lambda i, f: (0, nf + f)),
            pl.BlockSpec((1, tf), lambda i, f: (0, f)),
            pl.BlockSpec((1, tf), lambda i, f: (0, nf + f)),
            pl.BlockSpec((tf, d), lambda i, f: (f, 0)),
            pl.BlockSpec((1, d), lambda i, f: (0, 0)),
        ],
        out_specs=pl.BlockSpec((tm, d), lambda i, f: (i, 0)),
        out_shape=jax.ShapeDtypeStruct((m, d), F32),
        scratch_shapes=[pltpu.VMEM((tm + HALO, d), BF16)],
        compiler_params=_params("parallel", "arbitrary"),
        name="conv_ffn",
    )(x, x, gain.reshape(1, d), w_tiles, w_tiles, conv_w, conv_w, conv_b.reshape(1, -1),
      conv_b.reshape(1, -1), w_down, final_gain.reshape(1, d))


def _mem_attn_kernel(q_ref, mk_ref, mv_ref, o_ref, *, heads):
    dh = q_ref.shape[1] // heads
    scale = dh ** -0.5
    for hd in range(heads):
        cols = slice(hd * dh, (hd + 1) * dh)
        q = q_ref[:, cols].astype(BF16)
        s = _dot_nt(q, mk_ref[:, cols]) * scale
        s = s - jnp.max(s, axis=-1, keepdims=True)
        p = jnp.exp(s)
        p = p / jnp.sum(p, axis=-1, keepdims=True)
        o_ref[:, cols] = _dot(p.astype(BF16), mv_ref[:, cols]).astype(o_ref.dtype)


def mem_attention(z, q_col_block, mem_kv, layer, batch, seq_len, n_mem):
    m = z.shape[0]
    mem_width = MEM_HEADS * LANES
    t = _pick_tile(seq_len, (1024, 512, 256))
    tiles = seq_len // t
    kern = functools.partial(_mem_attn_kernel, heads=MEM_HEADS)
    return pl.pallas_call(
        kern,
        grid=(batch, tiles),
        in_specs=[
            pl.BlockSpec((t, mem_width), lambda b, s: (b * tiles + s, q_col_block)),
            pl.BlockSpec((n_mem, mem_width), lambda b, s: (b, 2 * layer)),
            pl.BlockSpec((n_mem, mem_width), lambda b, s: (b, 2 * layer + 1)),
        ],
        out_specs=pl.BlockSpec((t, mem_width), lambda b, s: (b * tiles + s, 0)),
        out_shape=jax.ShapeDtypeStruct((m, mem_width), BF16),
        compiler_params=_params("parallel", "arbitrary"),
        name="mem_attention",
    )(z, mem_kv, mem_kv)


GLA_BLOCK = 256


def _ref_rows(bc, half):
    rows, dk = bc.shape
    if half >= SUBLANES_F32:
        nb = rows // (2 * half)
        b3 = bc.reshape(nb, 2 * half, dk)
        return jnp.broadcast_to(b3[:, half:half + 1, :], b3.shape).reshape(rows, dk)
    b3 = bc.reshape(rows // SUBLANES_F32, SUBLANES_F32, dk)
    sub = lax.broadcasted_iota(jnp.int32, b3.shape, 1)
    out = None
    for mid in range(half, SUBLANES_F32, 2 * half):
        cand = jnp.broadcast_to(b3[:, mid:mid + 1, :], b3.shape)
        out = cand if out is None else jnp.where(sub >= mid - half, cand, out)
    return out.reshape(rows, dk)


GLA_HEADS_PER_STEP = 2


def _gla_kernel(q_ref, k_ref, v_ref, glr_ref, og_ref, wg_ref, bg_ref, gn_ref, o_ref, state_ref, *, rank, heads):
    t = pl.program_id(2)
    rows = q_ref.shape[0]
    dk = q_ref.shape[1] // heads
    dv = v_ref.shape[1] // heads
    blk = GLA_BLOCK if rows % GLA_BLOCK == 0 else rows
    n_levels = int(math.log2(blk))

    @pl.when(t == 0)
    def _():
        state_ref[...] = jnp.zeros_like(state_ref)

    ti = lax.broadcasted_iota(jnp.int32, (blk, blk), 0)
    si = lax.broadcasted_iota(jnp.int32, (blk, blk), 1)
    tri = jnp.where(si <= ti, 1.0, 0.0).astype(BF16)
    x = ti ^ si
    top_bit = 31 - lax.clz(jnp.maximum(x, 1))
    pair_level = jnp.where(si < ti, top_bit, jnp.where(si == ti, -1, -2))
    qscale = dk ** -0.5

    def head_step(hd, rs, glr):
        kc = slice(hd * dk, (hd + 1) * dk)
        vc = slice(hd * dv, (hd + 1) * dv)
        q = q_ref[rs, kc].astype(F32) * qscale
        k = k_ref[rs, kc].astype(F32)
        v = v_ref[rs, vc].astype(BF16)
        g = _log_sigmoid(_dot(glr, wg_ref[hd]) + bg_ref[hd]) * (LOG2_E / GLA_GATE_TAU)
        g_hi, g_lo = _split2(g)
        bc = _dot(tri, g_hi) + _dot(tri, g_lo)
        b_last = bc[blk - 1:blk, :]
        state = state_ref[hd]

        o = _dot_nt((q * jnp.exp2(bc)).astype(BF16), state.astype(BF16))

        att = jnp.zeros((blk, blk), F32)
        q_b, k_b = q.astype(BF16), k.astype(BF16)
        for p in range(n_levels):
            e = jnp.exp2(_neg_abs(bc - _ref_rows(bc, 1 << p)).astype(BF16))
            att = jnp.where(pair_level == p, _dot_nt(q_b * e, k_b * e), att)
        att = jnp.where(pair_level == -1, _dot_nt(q_b, k_b), att)
        o = o + _dot(att.astype(BF16), v)

        kd = (k * jnp.exp2(b_last - bc)).astype(BF16)
        state_ref[hd] = state * jnp.exp2(b_last) + _dot_tn(v, kd)

        o = o * lax.rsqrt(jnp.mean(o * o, axis=-1, keepdims=True) + EPS) * gn_ref[hd]
        og = og_ref[rs, vc].astype(F32)
        o_ref[rs, vc] = (o * (og * jax.nn.sigmoid(og))).astype(o_ref.dtype)

    def step(i, carry):
        rs = pl.ds(pl.multiple_of(i * blk, blk), blk)
        glr = glr_ref[rs, 0:rank].astype(BF16)
        for hd in range(heads):
            head_step(hd, rs, glr)
        return carry

    lax.fori_loop(0, rows // blk, step, 0)


def gla_attention(z, w_gate, b_gate, gn_gain, batch, seq_len, dk, dv, rank):
    m = z.shape[0]
    heads = GLA_HEADS
    hp = GLA_HEADS_PER_STEP
    groups = heads // hp
    t = _pick_tile(seq_len, (1024, 512, 256))
    tiles = seq_len // t
    v0, og0 = 0, groups
    q0 = (2 * heads * dv) // (hp * dk)
    k0 = q0 + groups
    glr0 = (2 * heads * dv + 2 * heads * dk + MEM_HEADS * LANES) // LANES
    assert (2 * heads * dv) % (hp * dk) == 0
    rows = lambda b, h, s: b * tiles + s
    kern = functools.partial(_gla_kernel, rank=rank, heads=hp)
    return pl.pallas_call(
        kern,
        grid=(batch, groups, tiles),
        in_specs=[
            pl.BlockSpec((t, hp * dk), lambda b, h, s: (rows(b, h, s), q0 + h)),
            pl.BlockSpec((t, hp * dk), lambda b, h, s: (rows(b, h, s), k0 + h)),
            pl.BlockSpec((t, hp * dv), lambda b, h, s: (rows(b, h, s), v0 + h)),
            pl.BlockSpec((t, LANES), lambda b, h, s: (rows(b, h, s), glr0)),
            pl.BlockSpec((t, hp * dv), lambda b, h, s: (rows(b, h, s), og0 + h)),
            pl.BlockSpec((hp, rank, dk), lambda b, h, s: (h, 0, 0)),
            pl.BlockSpec((hp, 1, dk), lambda b, h, s: (h, 0, 0)),
            pl.BlockSpec((hp, 1, dv), lambda b, h, s: (h, 0, 0)),
        ],
        out_specs=pl.BlockSpec((t, hp * dv), lambda b, h, s: (rows(b, h, s), h)),
        out_shape=jax.ShapeDtypeStruct((m, heads * dv), BF16),
        scratch_shapes=[pltpu.VMEM((hp, dv, dk), F32)],
        compiler_params=_params("parallel", "parallel", "arbitrary"),
        name="gla_attention",
    )(z, z, z, z, z, w_gate, b_gate, gn_gain)


BIAS_PIECES = 3


def _split3(x):
    hi = x.astype(BF16)
    r1 = x - hi.astype(F32)
    mid = r1.astype(BF16)
    lo = (r1 - mid.astype(F32)).astype(BF16)
    return hi, mid, lo


def _fox_gate_kernel(x_ref, g_ref, w_ref, b_ref, route_ref, o_ref, carry_ref):
    s = pl.program_id(1)
    t = x_ref.shape[0]

    @pl.when(s == 0)
    def _():
        carry_ref[...] = jnp.zeros_like(carry_ref)

    h = _rms(x_ref[...], g_ref[...]).astype(BF16)
    log_f = _log_sigmoid(_dot(h, w_ref[...]) + b_ref[...])
    ti = lax.broadcasted_iota(jnp.int32, (t, t), 0)
    si = lax.broadcasted_iota(jnp.int32, (t, t), 1)
    tri = jnp.where(si <= ti, 1.0, 0.0).astype(BF16)
    f_hi, f_mid, f_lo = _split3(log_f)
    c = carry_ref[...] + _dot(tri, f_hi) + _dot(tri, f_mid) + _dot(tri, f_lo)
    carry_ref[...] = c[t - 1:t, :]
    pieces = jnp.concatenate(_split3(c * (-LOG2_E)), axis=1)
    o_ref[...] = _dot(pieces, route_ref[...]).astype(BF16)


def fox_gates(x, gain, w_f, b_f, batch, seq_len):
    m, d = x.shape
    t = _pick_tile(seq_len, (512, 256))
    tiles = seq_len // t
    src = lax.broadcasted_iota(jnp.int32, (BIAS_PIECES * LANES, FOX_HEADS * LANES), 0)
    dst = lax.broadcasted_iota(jnp.int32, (BIAS_PIECES * LANES, FOX_HEADS * LANES), 1)
    route = ((src // LANES == dst % LANES) & (src % LANES == dst // LANES)).astype(BF16)
    return pl.pallas_call(
        _fox_gate_kernel,
        grid=(batch, tiles),
        in_specs=[
            pl.BlockSpec((t, d), lambda b, s: (b * tiles + s, 0)),
            pl.BlockSpec((1, d), lambda b, s: (0, 0)),
            pl.BlockSpec((d, LANES), lambda b, s: (0, 0)),
            pl.BlockSpec((1, LANES), lambda b, s: (0, 0)),
            pl.BlockSpec(route.shape, lambda b, s: (0, 0)),
        ],
        out_specs=pl.BlockSpec((t, FOX_HEADS * LANES), lambda b, s: (b * tiles + s, 0)),
        out_shape=jax.ShapeDtypeStruct((m, FOX_HEADS * LANES), BF16),
        scratch_shapes=[pltpu.VMEM((1, LANES), F32)],
        compiler_params=_params("parallel", "arbitrary"),
        name="fox_gates",
    )(x, gain.reshape(1, d), w_f, b_f, route)


FOX_HEADS_PER_STEP = 2


def _fox_attn_kernel(q_ref, k_ref, kb_ref, v_ref, o_ref, m_ref, acc_ref, s0_ref, s1_ref, *, heads):
    qi = pl.program_id(2)
    tq = q_ref.shape[0]
    tk = tq // 2
    dh = q_ref.shape[1] // heads
    cols = [slice(hd * dh, (hd + 1) * dh) for hd in range(heads)]
    lane = lax.broadcasted_iota(jnp.int32, (tq, dh), 1)
    bias_taps = jnp.where(lane < BIAS_PIECES, 1.0, 0.0).astype(BF16)
    ones_col = jnp.where(lax.broadcasted_iota(jnp.int32, (tk, dh), 1) == 0, 1.0, 0.0).astype(BF16)
    qs = [jnp.concatenate([(q_ref[:, cols[hd]].astype(F32) * (dh ** -0.5 * LOG2_E)).astype(BF16), bias_taps],
                          axis=1) for hd in range(heads)]

    def key_rows(j):
        return pl.ds(pl.multiple_of(j * tk, tk), tk)

    def issue_scores(s_ref, j, rows):
        ks = key_rows(j)
        for hd in range(heads):
            keys = jnp.concatenate([k_ref[ks, cols[hd]], kb_ref[ks, cols[hd]]], axis=1)
            s_ref[hd, rows, :] = _dot_nt(qs[hd][rows], keys)

    def update(hd, s, j, rows):
        m_prev = m_ref[hd, rows, :]
        m_new = jnp.maximum(m_prev, jnp.max(s, axis=-1, keepdims=True))
        alpha = jnp.exp2(m_prev - m_new)
        p = jnp.concatenate([jnp.exp2((s[:, c:c + LANES] - m_new).astype(BF16))
                             for c in range(0, tk, LANES)], axis=1)
        pv = _dot(p, jnp.concatenate([v_ref[key_rows(j), cols[hd]], ones_col], axis=1))
        acc = acc_ref[hd, rows, :]
        acc_ref[hd, rows, :] = jnp.concatenate([alpha * acc[:, c:c + LANES] for c in range(0, 2 * dh, LANES)],
                                               axis=1) + pv
        m_ref[hd, rows, :] = m_new

    every = slice(0, tq)
    lower = slice(tk, tq)

    m_ref[...] = jnp.full(m_ref.shape, -jnp.inf, F32)
    acc_ref[...] = jnp.zeros_like(acc_ref)
    issue_scores(s0_ref, 0, every)

    def pair(jj, carry):
        j = 2 * jj
        issue_scores(s1_ref, j + 1, every)
        for hd in range(heads):
            update(hd, s0_ref[hd], j, every)
        issue_scores(s0_ref, j + 2, every)
        for hd in range(heads):
            update(hd, s1_ref[hd], j + 1, every)
        return carry

    lax.fori_loop(0, qi, pair, 0)

    issue_scores(s1_ref, 2 * qi + 1, lower)
    ti = lax.broadcasted_iota(jnp.int32, (tq, tk), 0)
    si = lax.broadcasted_iota(jnp.int32, (tq, tk), 1)
    for hd in range(heads):
        update(hd, jnp.where(si <= ti, s0_ref[hd], -jnp.inf), 2 * qi, every)
    tl = lax.broadcasted_iota(jnp.int32, (tk, tk), 0)
    sl = lax.broadcasted_iota(jnp.int32, (tk, tk), 1)
    for hd in range(heads):
        update(hd, jnp.where(sl <= tl, s1_ref[hd, lower, :], -jnp.inf), 2 * qi + 1, lower)
    for hd in range(heads):
        acc = acc_ref[hd]
        o_ref[:, cols[hd]] = (acc[:, 0:dh] / acc[:, dh:dh + 1]).astype(o_ref.dtype)


def fox_attention(zq, kv, kbias, batch, seq_len):
    m = zq.shape[0]
    hp = FOX_HEADS_PER_STEP
    width = hp * LANES
    groups = FOX_HEADS // hp
    tq = _pick_tile(seq_len, (1024, 512, 256))
    tiles = seq_len // tq
    kern = functools.partial(_fox_attn_kernel, heads=hp)
    return pl.pallas_call(
        kern,
        grid=(batch, groups, tiles),
        in_specs=[
            pl.BlockSpec((tq, width), lambda b, h, s: (b * tiles + s, h)),
            pl.BlockSpec((seq_len, width), lambda b, h, s: (b, h)),
            pl.BlockSpec((seq_len, width), lambda b, h, s: (b, h)),
            pl.BlockSpec((seq_len, width), lambda b, h, s: (b, groups + h)),
        ],
        out_specs=pl.BlockSpec((tq, width), lambda b, h, s: (b * tiles + s, h)),
        out_shape=jax.ShapeDtypeStruct((m, FOX_HEADS * LANES), BF16),
        scratch_shapes=[pltpu.VMEM((hp, tq, LANES), F32), pltpu.VMEM((hp, tq, 2 * LANES), F32),
                        pltpu.VMEM((hp, tq, tq // 2), F32), pltpu.VMEM((hp, tq, tq // 2), F32)],
        compiler_params=_params("parallel", "parallel", "arbitrary"),
        name="fox_attention",
    )(zq, kv, kbias, kv)


def kernel(x, mem, norm_mix, norm_ffn, norm_mem, norm_final, mem_w_kv, gla_w_in, gla_w_gate_up,
           gla_b_gate, gla_norm, gla_w_out, fox_kv_norm, fox_w_kv, fox_b_f, fox_w_in, fox_w_out,
           ffn_w_up, ffn_conv_w, ffn_conv_b, ffn_w_down):
    batch, seq_len, d = x.shape
    n_mem = mem.shape[1]
    depth = norm_mix.shape[0]
    n_gla = gla_w_in.shape[0]
    rank = gla_w_gate_up.shape[1]
    qk_width = gla_w_gate_up.shape[2]
    dk = qk_width // GLA_HEADS
    v_width = gla_norm.shape[1]
    dv = v_width // GLA_HEADS
    mem_width = mem_w_kv.shape[2] // 2
    fox_width = (fox_w_kv.shape[1] - FOX_HEADS) // 2
    assert mem_width == MEM_HEADS * LANES and fox_width == FOX_HEADS * LANES
    assert (2 * v_width) % dk == 0 and rank <= LANES

    xf = x.reshape(batch * seq_len, d)

    w_mem = jnp.transpose(mem_w_kv, (1, 0, 2)).reshape(d, depth * 2 * mem_width).astype(BF16)
    mem_kv = norm_matmul(mem.reshape(batch * n_mem, d), norm_mem, w_mem, BF16,
                         _pick_tile(w_mem.shape[1], (1024, 512)))

    fox_kv = fox_c = None
    for i in range(depth):
        if i == n_gla:
            fox_kv = norm_matmul(xf, fox_kv_norm, fox_w_kv[:, :2 * fox_width].astype(BF16), BF16,
                                 _pick_tile(2 * fox_width, (1024, 768, 512)))
            w_f = jnp.pad(fox_w_kv[:, 2 * fox_width:], ((0, 0), (0, LANES - FOX_HEADS))).astype(BF16)
            b_f = jnp.pad(fox_b_f, (0, LANES - FOX_HEADS)).reshape(1, LANES)
            fox_c = fox_gates(xf, fox_kv_norm, w_f, b_f, batch, seq_len)
        if i < n_gla:
            w = gla_w_in[i]
            o_q, o_k, o_v = 0, qk_width, 2 * qk_width
            o_glr = o_v + v_width
            o_og = o_glr + rank
            o_mq = o_og + v_width
            w_in = jnp.concatenate([
                w[:, o_v:o_glr], w[:, o_og:o_mq], w[:, o_q:o_k], w[:, o_k:o_v], w[:, o_mq:],
                jnp.pad(w[:, o_glr:o_og], ((0, 0), (0, LANES - rank)))], axis=1).astype(BF16)
            z = norm_matmul(xf, norm_mix[i], w_in, BF16, _pick_tile(w_in.shape[1], (1920, 1152, 640, 384, 128)))
            a = gla_attention(z, gla_w_gate_up[i].reshape(rank, GLA_HEADS, dk).transpose(1, 0, 2).astype(BF16),
                              gla_b_gate[i].reshape(GLA_HEADS, 1, dk), gla_norm[i].reshape(GLA_HEADS, 1, dv),
                              batch, seq_len, dk, dv, rank)
            mq_block = (2 * v_width + 2 * qk_width) // mem_width
            w_out = gla_w_out[i].astype(BF16)
        else:
            j = i - n_gla
            z = norm_matmul(xf, norm_mix[i], fox_w_in[j].astype(BF16), BF16,
                            _pick_tile(fox_w_in.shape[2], (1024, 512)))
            a = fox_attention(z, fox_kv, fox_c, batch, seq_len)
            mq_block = fox_width // mem_width
            w_out = fox_w_out[j].astype(BF16)
        mo = mem_attention(z, mq_block, mem_kv, i, batch, seq_len, n_mem)
        xf = out_proj(a, mo, w_out, xf)
        xf = conv_ffn(xf, norm_ffn[i], ffn_w_up[i].astype(BF16), ffn_conv_w[i], ffn_conv_b[i],
                      ffn_w_down[i].astype(BF16), norm_final, seq_len, final_norm=(i == depth - 1))
    return xf.reshape(batch, seq_len, d)
```

```python
import functools
import math

import jax
import jax.numpy as jnp
from jax import lax
from jax.experimental import pallas as pl
from jax.experimental.pallas import tpu as pltpu

GLA_HEADS = 4
GLA_GATE_TAU = 16.0
FOX_HEADS = 12
MEM_HEADS = 4
CONV_WIDTH = 3
EPS = 1e-6
LOG2_E = math.log2(math.e)

LANES = 128
SUBLANES_F32 = 8
SUBLANES_BF16 = 16
VMEM_LIMIT = 56 * 1024 * 1024

BF16 = jnp.bfloat16
F32 = jnp.float32

NT_DIMS = (((1,), (1,)), ((), ()))
TN_DIMS = (((0,), (0,)), ((), ()))


def _params(*sem):
    return pltpu.CompilerParams(dimension_semantics=sem, vmem_limit_bytes=VMEM_LIMIT)


def _rms(x, gain):
    return x * lax.rsqrt(jnp.mean(x * x, axis=-1, keepdims=True) + EPS) * gain


def _dot(a, b):
    return jnp.dot(a, b, preferred_element_type=F32)


def _dot_nt(a, b):
    return lax.dot_general(a, b, NT_DIMS, preferred_element_type=F32)


def _dot_tn(a, b):
    return lax.dot_general(a, b, TN_DIMS, preferred_element_type=F32)


def _split2(x):
    hi = x.astype(BF16)
    lo = (x - hi.astype(F32)).astype(BF16)
    return hi, lo


def _neg_abs(x):
    bits = lax.bitcast_convert_type(x, jnp.uint32) | jnp.uint32(0x80000000)
    return lax.bitcast_convert_type(bits, F32)


def _log_sigmoid(x):
    return jnp.minimum(x, 0.0) - jnp.log1p(jnp.exp(-jnp.abs(x)))


def _pick_tile(n, preferred):
    for t in preferred:
        if n % t == 0:
            return t
    return n


def _norm_matmul_kernel(x_ref, g_ref, w_ref, o_ref, *, tn):
    h = _rms(x_ref[...], g_ref[...]).astype(BF16)
    for c in range(0, o_ref.shape[1], tn):
        o_ref[:, c:c + tn] = _dot(h, w_ref[:, c:c + tn]).astype(o_ref.dtype)


def norm_matmul(x, gain, w, out_dtype, tn):
    m, d = x.shape
    n = w.shape[1]
    out_bytes = jnp.dtype(out_dtype).itemsize

    def vmem_need(tm):
        return 2 * tm * d * 4 + d * n * 2 + 2 * tm * n * out_bytes + tm * d * 2 + tm * tn * 4

    tm = next(t for t in (1024, 512, 256, 128) if m % t == 0 and vmem_need(t) <= 0.9 * VMEM_LIMIT)
    return pl.pallas_call(
        functools.partial(_norm_matmul_kernel, tn=tn),
        grid=(m // tm,),
        in_specs=[
            pl.BlockSpec((tm, d), lambda i: (i, 0)),
            pl.BlockSpec((1, d), lambda i: (0, 0)),
            pl.BlockSpec((d, n), lambda i: (0, 0), pipeline_mode=pl.Buffered(1)),
        ],
        out_specs=pl.BlockSpec((tm, n), lambda i: (i, 0)),
        out_shape=jax.ShapeDtypeStruct((m, n), out_dtype),
        compiler_params=_params("parallel"),
        name="norm_matmul",
    )(x, gain.reshape(1, d), w)


def _out_proj_kernel(a_ref, b_ref, w_ref, x_ref, o_ref):
    ka = a_ref.shape[1]
    o_ref[...] = x_ref[...] + _dot(a_ref[...], w_ref[0:ka, :]) + _dot(b_ref[...], w_ref[ka:, :])


def out_proj(a, b, w, x):
    m, d = x.shape
    ka, kb = a.shape[1], b.shape[1]
    tm = _pick_tile(m, (1024, 512, 256))
    return pl.pallas_call(
        _out_proj_kernel,
        grid=(m // tm,),
        in_specs=[
            pl.BlockSpec((tm, ka), lambda i: (i, 0)),
            pl.BlockSpec((tm, kb), lambda i: (i, 0)),
            pl.BlockSpec((ka + kb, d), lambda i: (0, 0), pipeline_mode=pl.Buffered(1)),
            pl.BlockSpec((tm, d), lambda i: (i, 0)),
        ],
        out_specs=pl.BlockSpec((tm, d), lambda i: (i, 0)),
        out_shape=jax.ShapeDtypeStruct((m, d), F32),
        compiler_params=_params("parallel"),
        name="out_proj",
    )(a, b, w, x)


HALO = SUBLANES_BF16
FFN_CHUNK_PARTS = (1, 1)


def _ffn_kernel(x_ref, xh_ref, g_ref, w_ref, taps_ref, wd_ref, gf_ref, o_ref, h_ref, *, tiles_per_seq,
                final_norm):
    i = pl.program_id(0)
    f = pl.program_id(1)
    tm = x_ref.shape[0]
    tf = wd_ref.shape[0]
    hidden = taps_ref.shape[1] // 2

    @pl.when(f == 0)
    def _():
        x = x_ref[...]
        prev = _rms(xh_ref[...], g_ref[...])
        prev = jnp.where(i % tiles_per_seq == 0, 0.0, prev)
        h_ref[0:HALO, :] = prev.astype(BF16)
        h_ref[HALO:HALO + tm, :] = _rms(x, g_ref[...]).astype(BF16)
        o_ref[...] = x

    def conv(u, first_col):
        taps = taps_ref[:, pl.ds(pl.multiple_of(first_col, LANES), tf)]
        y = taps[2:3, :] * u + taps[1:2, :] * pltpu.roll(u, 1, axis=0) + taps[0:1, :] * pltpu.roll(u, 2, axis=0)
        return y[HALO:, :] + taps[3:4, :]

    def up(lo, rc):
        h = h_ref[lo:lo + rc + HALO, :]
        return _dot(h, w_ref[0]), _dot(h, w_ref[1])

    sizes = [c * tm // sum(FFN_CHUNK_PARTS) for c in FFN_CHUNK_PARTS]
    starts = [sum(sizes[:n]) for n in range(len(sizes))]
    u_next = up(starts[0], sizes[0])
    for n, (lo, rc) in enumerate(zip(starts, sizes)):
        ua, uv = u_next
        if n + 1 < len(starts):
            u_next = up(starts[n + 1], sizes[n + 1])
        ya = conv(ua, f * tf)
        yv = conv(uv, hidden + f * tf)
        act = (ya * jax.nn.sigmoid(ya) * yv).astype(BF16)
        o_ref[lo:lo + rc, :] += _dot(act, wd_ref[...])

    if final_norm:
        @pl.when(f == pl.num_programs(1) - 1)
        def _():
            o_ref[...] = _rms(o_ref[...], gf_ref[...])


def conv_ffn(x, gain, w_up, conv_w, conv_b, w_down, final_gain, seq_len, final_norm):
    m, d = x.shape
    hidden = w_down.shape[0]
    tm = _pick_tile(seq_len, (1024, 512, 256))
    tf = _pick_tile(hidden, (512, 256, 128))
    nf = hidden // tf
    halo_blocks = tm // HALO
    assert tm % (sum(FFN_CHUNK_PARTS) * SUBLANES_BF16) == 0
    w_tiles = w_up.reshape(d, 2, nf, tf).transpose(2, 1, 0, 3)
    taps = jnp.concatenate([conv_w, conv_b.reshape(1, -1)], axis=0)
    kern = functools.partial(_ffn_kernel, tiles_per_seq=seq_len // tm, final_norm=final_norm)
    return pl.pallas_call(
        kern,
        grid=(m // tm, nf),
        in_specs=[
            pl.BlockSpec((tm, d), lambda i, f: (i, 0)),
            pl.BlockSpec((HALO, d), lambda i, f: (jnp.maximum(i * halo_blocks - 1, 0), 0)),
            pl.BlockSpec((1, d), lambda i, f: (0, 0)),
            pl.BlockSpec((None, 2, d, tf), lambda i, f: (f, 0, 0, 0)),
            pl.BlockSpec(taps.shape, lambda i, f: (0, 0)),
            pl.BlockSpec((tf, d), lambda i, f: (f, 0)),
            pl.BlockSpec((1, d), lambda i, f: (0, 0)),
        ],
        out_specs=pl.BlockSpec((tm, d), lambda i, f: (i, 0)),
        out_shape=jax.ShapeDtypeStruct((m, d), F32),
        scratch_shapes=[pltpu.VMEM((tm + HALO, d), BF16)],
        compiler_params=_params("parallel", "arbitrary"),
        name="conv_ffn",
    )(x, x, gain.reshape(1, d), w_tiles, taps, w_down, final_gain.reshape(1, d))


def _mem_attn_kernel(q_ref, mk_ref, mv_ref, o_ref, *, heads):
    dh = q_ref.shape[1] // heads
    scale = dh ** -0.5
    for hd in range(heads):
        cols = slice(hd * dh, (hd + 1) * dh)
        q = q_ref[:, cols].astype(BF16)
        s = _dot_nt(q, mk_ref[:, cols]) * scale
        s = s - jnp.max(s, axis=-1, keepdims=True)
        p = jnp.exp(s)
        p = p / jnp.sum(p, axis=-1, keepdims=True)
        o_ref[:, cols] = _dot(p.astype(BF16), mv_ref[:, cols]).astype(o_ref.dtype)


def mem_attention(z, q_col_block, mem_kv, layer, batch, seq_len, n_mem):
    m = z.shape[0]
    mem_width = MEM_HEADS * LANES
    t = _pick_tile(seq_len, (1024, 512, 256))
    tiles = seq_len // t
    kern = functools.partial(_mem_attn_kernel, heads=MEM_HEADS)
    return pl.pallas_call(
        kern,
        grid=(batch, tiles),
        in_specs=[
            pl.BlockSpec((t, mem_width), lambda b, s: (b * tiles + s, q_col_block)),
            pl.BlockSpec((n_mem, mem_width), lambda b, s: (b, 2 * layer)),
            pl.BlockSpec((n_mem, mem_width), lambda b, s: (b, 2 * layer + 1)),
        ],
        out_specs=pl.BlockSpec((t, mem_width), lambda b, s: (b * tiles + s, 0)),
        out_shape=jax.ShapeDtypeStruct((m, mem_width), BF16),
        compiler_params=_params("parallel", "arbitrary"),
        name="mem_attention",
    )(z, mem_kv, mem_kv)


GLA_BLOCK = 256


def _ref_rows(bc, half):
    rows, dk = bc.shape
    if half >= SUBLANES_F32:
        nb = rows // (2 * half)
        b3 = bc.reshape(nb, 2 * half, dk)
        return jnp.broadcast_to(b3[:, half:half + 1, :], b3.shape).reshape(rows, dk)
    b3 = bc.reshape(rows // SUBLANES_F32, SUBLANES_F32, dk)
    sub = lax.broadcasted_iota(jnp.int32, b3.shape, 1)
    out = None
    for mid in range(half, SUBLANES_F32, 2 * half):
        cand = jnp.broadcast_to(b3[:, mid:mid + 1, :], b3.shape)
        out = cand if out is None else jnp.where(sub >= mid - half, cand, out)
    return out.reshape(rows, dk)


GLA_HEADS_PER_STEP = 2


def _gla_kernel(q_ref, k_ref, v_ref, glr_ref, og_ref, wg_ref, bg_ref, gn_ref, o_ref, state_ref, *, rank, heads):
    t = pl.program_id(2)
    rows = q_ref.shape[0]
    dk = q_ref.shape[1] // heads
    dv = v_ref.shape[1] // heads
    blk = GLA_BLOCK if rows % GLA_BLOCK == 0 else rows
    n_levels = int(math.log2(blk))

    @pl.when(t == 0)
    def _():
        state_ref[...] = jnp.zeros_like(state_ref)

    ti = lax.broadcasted_iota(jnp.int32, (blk, blk), 0)
    si = lax.broadcasted_iota(jnp.int32, (blk, blk), 1)
    tri = jnp.where(si <= ti, 1.0, 0.0).astype(BF16)
    x = ti ^ si
    top_bit = 31 - lax.clz(jnp.maximum(x, 1))
    pair_level = jnp.where(si < ti, top_bit, jnp.where(si == ti, -1, -2))
    qscale = dk ** -0.5

    def head_step(hd, rs, glr):
        kc = slice(hd * dk, (hd + 1) * dk)
        vc = slice(hd * dv, (hd + 1) * dv)
        q = q_ref[rs, kc].astype(F32) * qscale
        k = k_ref[rs, kc].astype(F32)
        v = v_ref[rs, vc].astype(BF16)
        g = _log_sigmoid(_dot(glr, wg_ref[hd]) + bg_ref[hd]) * (LOG2_E / GLA_GATE_TAU)
        g_hi, g_lo = _split2(g)
        bc = _dot(tri, g_hi) + _dot(tri, g_lo)
        b_last = bc[blk - 1:blk, :]
        state = state_ref[hd]

        o = _dot_nt((q * jnp.exp2(bc)).astype(BF16), state.astype(BF16))

        att = jnp.zeros((blk, blk), F32)
        q_b, k_b = q.astype(BF16), k.astype(BF16)
        for p in range(n_levels):
            e = jnp.exp2(_neg_abs(bc - _ref_rows(bc, 1 << p)).astype(BF16))
            att = jnp.where(pair_level == p, _dot_nt(q_b * e, k_b * e), att)
        att = jnp.where(pair_level == -1, _dot_nt(q_b, k_b), att)
        o = o + _dot(att.astype(BF16), v)

        kd = (k * jnp.exp2(b_last - bc)).astype(BF16)
        state_ref[hd] = state * jnp.exp2(b_last) + _dot_tn(v, kd)

        o = o * lax.rsqrt(jnp.mean(o * o, axis=-1, keepdims=True) + EPS) * gn_ref[hd]
        og = og_ref[rs, vc].astype(F32)
        o_ref[rs, vc] = (o * (og * jax.nn.sigmoid(og))).astype(o_ref.dtype)

    def step(i, carry):
        rs = pl.ds(pl.multiple_of(i * blk, blk), blk)
        glr = glr_ref[rs, 0:rank].astype(BF16)
        for hd in range(heads):
            head_step(hd, rs, glr)
        return carry

    lax.fori_loop(0, rows // blk, step, 0)


def gla_attention(z, w_gate, b_gate, gn_gain, batch, seq_len, dk, dv, rank):
    m = z.shape[0]
    heads = GLA_HEADS
    hp = GLA_HEADS_PER_STEP
    groups = heads // hp
    t = _pick_tile(seq_len, (1024, 512, 256))
    tiles = seq_len // t
    v0, og0 = 0, groups
    q0 = (2 * heads * dv) // (hp * dk)
    k0 = q0 + groups
    glr0 = (2 * heads * dv + 2 * heads * dk + MEM_HEADS * LANES) // LANES
    assert (2 * heads * dv) % (hp * dk) == 0
    rows = lambda b, h, s: b * tiles + s
    kern = functools.partial(_gla_kernel, rank=rank, heads=hp)
    return pl.pallas_call(
        kern,
        grid=(batch, groups, tiles),
        in_specs=[
            pl.BlockSpec((t, hp * dk), lambda b, h, s: (rows(b, h, s), q0 + h)),
            pl.BlockSpec((t, hp * dk), lambda b, h, s: (rows(b, h, s), k0 + h)),
            pl.BlockSpec((t, hp * dv), lambda b, h, s: (rows(b, h, s), v0 + h)),
            pl.BlockSpec((t, LANES), lambda b, h, s: (rows(b, h, s), glr0)),
            pl.BlockSpec((t, hp * dv), lambda b, h, s: (rows(b, h, s), og0 + h)),
            pl.BlockSpec((hp, rank, dk), lambda b, h, s: (h, 0, 0)),
            pl.BlockSpec((hp, 1, dk), lambda b, h, s: (h, 0, 0)),
            pl.BlockSpec((hp, 1, dv), lambda b, h, s: (h, 0, 0)),
        ],
        out_specs=pl.BlockSpec((t, hp * dv), lambda b, h, s: (rows(b, h, s), h)),
        out_shape=jax.ShapeDtypeStruct((m, heads * dv), BF16),
        scratch_shapes=[pltpu.VMEM((hp, dv, dk), F32)],
        compiler_params=_params("parallel", "parallel", "arbitrary"),
        name="gla_attention",
    )(z, z, z, z, z, w_gate, b_gate, gn_gain)


BIAS_PIECES = 3


def _split3(x):
    hi = x.astype(BF16)
    r1 = x - hi.astype(F32)
    mid = r1.astype(BF16)
    lo = (r1 - mid.astype(F32)).astype(BF16)
    return hi, mid, lo


def _fox_gate_kernel(x_ref, g_ref, w_ref, b_ref, route_ref, o_ref, carry_ref):
    s = pl.program_id(1)
    t = x_ref.shape[0]

    @pl.when(s == 0)
    def _():
        carry_ref[...] = jnp.zeros_like(carry_ref)

    h = _rms(x_ref[...], g_ref[...]).astype(BF16)
    log_f = _log_sigmoid(_dot(h, w_ref[...]) + b_ref[...])
    ti = lax.broadcasted_iota(jnp.int32, (t, t), 0)
    si = lax.broadcasted_iota(jnp.int32, (t, t), 1)
    tri = jnp.where(si <= ti, 1.0, 0.0).astype(BF16)
    f_hi, f_mid, f_lo = _split3(log_f)
    c = carry_ref[...] + _dot(tri, f_hi) + _dot(tri, f_mid) + _dot(tri, f_lo)
    carry_ref[...] = c[t - 1:t, :]
    pieces = jnp.concatenate(_split3(c * (-LOG2_E)), axis=1)
    o_ref[...] = _dot(pieces, route_ref[...]).astype(BF16)


def fox_gates(x, gain, w_f, b_f, batch, seq_len):
    m, d = x.shape
    t = _pick_tile(seq_len, (512, 256))
    tiles = seq_len // t
    src = lax.broadcasted_iota(jnp.int32, (BIAS_PIECES * LANES, FOX_HEADS * LANES), 0)
    dst = lax.broadcasted_iota(jnp.int32, (BIAS_PIECES * LANES, FOX_HEADS * LANES), 1)
    route = ((src // LANES == dst % LANES) & (src % LANES == dst // LANES)).astype(BF16)
    return pl.pallas_call(
        _fox_gate_kernel,
        grid=(batch, tiles),
        in_specs=[
            pl.BlockSpec((t, d), lambda b, s: (b * tiles + s, 0)),
            pl.BlockSpec((1, d), lambda b, s: (0, 0)),
            pl.BlockSpec((d, LANES), lambda b, s: (0, 0)),
            pl.BlockSpec((1, LANES), lambda b, s: (0, 0)),
            pl.BlockSpec(route.shape, lambda b, s: (0, 0)),
        ],
        out_specs=pl.BlockSpec((t, FOX_HEADS * LANES), lambda b, s: (b * tiles + s, 0)),
        out_shape=jax.ShapeDtypeStruct((m, FOX_HEADS * LANES), BF16),
        scratch_shapes=[pltpu.VMEM((1, LANES), F32)],
        compiler_params=_params("parallel", "arbitrary"),
        name="fox_gates",
    )(x, gain.reshape(1, d), w_f, b_f, route)


FOX_HEADS_PER_STEP = 2


def _fox_attn_kernel(q_ref, k_ref, kb_ref, v_ref, o_ref, m_ref, acc_ref, s0_ref, s1_ref, *, heads):
    qi = pl.program_id(2)
    tq = q_ref.shape[0]
    tk = tq // 2
    dh = q_ref.shape[1] // heads
    cols = [slice(hd * dh, (hd + 1) * dh) for hd in range(heads)]
    lane = lax.broadcasted_iota(jnp.int32, (tq, dh), 1)
    bias_taps = jnp.where(lane < BIAS_PIECES, 1.0, 0.0).astype(BF16)
    ones_col = jnp.where(lax.broadcasted_iota(jnp.int32, (tk, dh), 1) == 0, 1.0, 0.0).astype(BF16)
    qs = [jnp.concatenate([(q_ref[:, cols[hd]].astype(F32) * (dh ** -0.5 * LOG2_E)).astype(BF16), bias_taps],
                          axis=1) for hd in range(heads)]

    def key_rows(j):
        return pl.ds(pl.multiple_of(j * tk, tk), tk)

    def issue_scores(s_ref, j, rows):
        ks = key_rows(j)
        for hd in range(heads):
            keys = jnp.concatenate([k_ref[ks, cols[hd]], kb_ref[ks, cols[hd]]], axis=1)
            s_ref[hd, rows, :] = _dot_nt(qs[hd][rows], keys)

    def update(hd, s, j, rows):
        m_prev = m_ref[hd, rows, :]
        m_new = jnp.maximum(m_prev, jnp.max(s, axis=-1, keepdims=True))
        alpha = jnp.exp2(m_prev - m_new)
        p = jnp.concatenate([jnp.exp2((s[:, c:c + LANES] - m_new).astype(BF16))
                             for c in range(0, tk, LANES)], axis=1)
        pv = _dot(p, jnp.concatenate([v_ref[key_rows(j), cols[hd]], ones_col], axis=1))
        acc = acc_ref[hd, rows, :]
        acc_ref[hd, rows, :] = jnp.concatenate([alpha * acc[:, c:c + LANES] for c in range(0, 2 * dh, LANES)],
                                               axis=1) + pv
        m_ref[hd, rows, :] = m_new

    every = slice(0, tq)
    lower = slice(tk, tq)

    m_ref[...] = jnp.full(m_ref.shape, -jnp.inf, F32)
    acc_ref[...] = jnp.zeros_like(acc_ref)
    issue_scores(s0_ref, 0, every)

    def pair(jj, carry):
        j = 2 * jj
        issue_scores(s1_ref, j + 1, every)
        for hd in range(heads):
            update(hd, s0_ref[hd], j, every)
        issue_scores(s0_ref, j + 2, every)
        for hd in range(heads):
            update(hd, s1_ref[hd], j + 1, every)
        return carry

    lax.fori_loop(0, qi, pair, 0)

    issue_scores(s1_ref, 2 * qi + 1, lower)
    ti = lax.broadcasted_iota(jnp.int32, (tq, tk), 0)
    si = lax.broadcasted_iota(jnp.int32, (tq, tk), 1)
    for hd in range(heads):
        update(hd, jnp.where(si <= ti, s0_ref[hd], -jnp.inf), 2 * qi, every)
    tl = lax.broadcasted_iota(jnp.int32, (tk, tk), 0)
    sl = lax.broadcasted_iota(jnp.int32, (tk, tk), 1)
    for hd in range(heads):
        update(hd, jnp.where(sl <= tl, s1_ref[hd, lower, :], -jnp.inf), 2 * qi + 1, lower)
    for hd in range(heads):
        acc = acc_ref[hd]
        o_ref[:, cols[hd]] = (acc[:, 0:dh] / acc[:, dh:dh + 1]).astype(o_ref.dtype)


def fox_attention(zq, kv, kbias, batch, seq_len):
    m = zq.shape[0]
    hp = FOX_HEADS_PER_STEP
    width = hp * LANES
    groups = FOX_HEADS // hp
    tq = _pick_tile(seq_len, (1024, 512, 256))
    tiles = seq_len // tq
    kern = functools.partial(_fox_attn_kernel, heads=hp)
    return pl.pallas_call(
        kern,
        grid=(batch, groups, tiles),
        in_specs=[
            pl.BlockSpec((tq, width), lambda b, h, s: (b * tiles + s, h)),
            pl.BlockSpec((seq_len, width), lambda b, h, s: (b, h)),
            pl.BlockSpec((seq_len, width), lambda b, h, s: (b, h)),
            pl.BlockSpec((seq_len, width), lambda b, h, s: (b, groups + h)),
        ],
        out_specs=pl.BlockSpec((tq, width), lambda b, h, s: (b * tiles + s, h)),
        out_shape=jax.ShapeDtypeStruct((m, FOX_HEADS * LANES), BF16),
        scratch_shapes=[pltpu.VMEM((hp, tq, LANES), F32), pltpu.VMEM((hp, tq, 2 * LANES), F32),
                        pltpu.VMEM((hp, tq, tq // 2), F32), pltpu.VMEM((hp, tq, tq // 2), F32)],
        compiler_params=_params("parallel", "parallel", "arbitrary"),
        name="fox_attention",
    )(zq, kv, kbias, kv)


def kernel(x, mem, norm_mix, norm_ffn, norm_mem, norm_final, mem_w_kv, gla_w_in, gla_w_gate_up,
           gla_b_gate, gla_norm, gla_w_out, fox_kv_norm, fox_w_kv, fox_b_f, fox_w_in, fox_w_out,
           ffn_w_up, ffn_conv_w, ffn_conv_b, ffn_w_down):
    batch, seq_len, d = x.shape
    n_mem = mem.shape[1]
    depth = norm_mix.shape[0]
    n_gla = gla_w_in.shape[0]
    rank = gla_w_gate_up.shape[1]
    qk_width = gla_w_gate_up.shape[2]
    dk = qk_width // GLA_HEADS
    v_width = gla_norm.shape[1]
    dv = v_width // GLA_HEADS
    mem_width = mem_w_kv.shape[2] // 2
    fox_width = (fox_w_kv.shape[1] - FOX_HEADS) // 2
    assert mem_width == MEM_HEADS * LANES and fox_width == FOX_HEADS * LANES
    assert (2 * v_width) % dk == 0 and rank <= LANES

    xf = x.reshape(batch * seq_len, d)

    w_mem = jnp.transpose(mem_w_kv, (1, 0, 2)).reshape(d, depth * 2 * mem_width).astype(BF16)
    mem_kv = norm_matmul(mem.reshape(batch * n_mem, d), norm_mem, w_mem, BF16,
                         _pick_tile(w_mem.shape[1], (1024, 512)))

    fox_kv = fox_c = None
    for i in range(depth):
        if i == n_gla:
            fox_kv = norm_matmul(xf, fox_kv_norm, fox_w_kv[:, :2 * fox_width].astype(BF16), BF16,
                                 _pick_tile(2 * fox_width, (1024, 768, 512)))
            w_f = jnp.pad(fox_w_kv[:, 2 * fox_width:], ((0, 0), (0, LANES - FOX_HEADS))).astype(BF16)
            b_f = jnp.pad(fox_b_f, (0, LANES - FOX_HEADS)).reshape(1, LANES)
            fox_c = fox_gates(xf, fox_kv_norm, w_f, b_f, batch, seq_len)
        if i < n_gla:
            w = gla_w_in[i]
            o_q, o_k, o_v = 0, qk_width, 2 * qk_width
            o_glr = o_v + v_width
            o_og = o_glr + rank
            o_mq = o_og + v_width
            w_in = jnp.concatenate([
                w[:, o_v:o_glr], w[:, o_og:o_mq], w[:, o_q:o_k], w[:, o_k:o_v], w[:, o_mq:],
                jnp.pad(w[:, o_glr:o_og], ((0, 0), (0, LANES - rank)))], axis=1).astype(BF16)
            z = norm_matmul(xf, norm_mix[i], w_in, BF16, _pick_tile(w_in.shape[1], (1920, 1152, 640, 384, 128)))
            a = gla_attention(z, gla_w_gate_up[i].reshape(rank, GLA_HEADS, dk).transpose(1, 0, 2).astype(BF16),
                              gla_b_gate[i].reshape(GLA_HEADS, 1, dk), gla_norm[i].reshape(GLA_HEADS, 1, dv),
                              batch, seq_len, dk, dv, rank)
            mq_block = (2 * v_width + 2 * qk_width) // mem_width
            w_out = gla_w_out[i].astype(BF16)
        else:
            j = i - n_gla
            z = norm_matmul(xf, norm_mix[i], fox_w_in[j].astype(BF16), BF16,
                            _pick_tile(fox_w_in.shape[2], (1024, 512)))
            a = fox_attention(z, fox_kv, fox_c, batch, seq_len)
            mq_block = fox_width // mem_width
            w_out = fox_w_out[j].astype(BF16)
        mo = mem_attention(z, mq_block, mem_kv, i, batch, seq_len, n_mem)
        xf = out_proj(a, mo, w_out, xf)
        xf = conv_ffn(xf, norm_ffn[i], ffn_w_up[i].astype(BF16), ffn_conv_w[i], ffn_conv_b[i],
                      ffn_w_down[i].astype(BF16), norm_final, seq_len, final_norm=(i == depth - 1))
    return xf.reshape(batch, seq_len, d)
```

```python
import functools
import math

import jax
import jax.numpy as jnp
from jax import lax
from jax.experimental import pallas as pl
from jax.experimental.pallas import tpu as pltpu

GLA_HEADS = 4
GLA_GATE_TAU = 16.0
FOX_HEADS = 12
MEM_HEADS = 4
CONV_WIDTH = 3
EPS = 1e-6
LOG2_E = math.log2(math.e)

LANES = 128
SUBLANES_F32 = 8
SUBLANES_BF16 = 16
VMEM_LIMIT = 56 * 1024 * 1024

BF16 = jnp.bfloat16
F32 = jnp.float32

NT_DIMS = (((1,), (1,)), ((), ()))
TN_DIMS = (((0,), (0,)), ((), ()))


def _params(*sem):
    return pltpu.CompilerParams(dimension_semantics=sem, vmem_limit_bytes=VMEM_LIMIT)


def _rms(x, gain):
    return x * lax.rsqrt(jnp.mean(x * x, axis=-1, keepdims=True) + EPS) * gain


def _dot(a, b):
    return jnp.dot(a, b, preferred_element_type=F32)


def _dot_nt(a, b):
    return lax.dot_general(a, b, NT_DIMS, preferred_element_type=F32)


def _dot_tn(a, b):
    return lax.dot_general(a, b, TN_DIMS, preferred_element_type=F32)


def _split2(x):
    hi = x.astype(BF16)
    lo = (x - hi.astype(F32)).astype(BF16)
    return hi, lo


def _neg_abs(x):
    bits = lax.bitcast_convert_type(x, jnp.uint32) | jnp.uint32(0x80000000)
    return lax.bitcast_convert_type(bits, F32)


def _log_sigmoid(x):
    return jnp.minimum(x, 0.0) - jnp.log1p(jnp.exp(-jnp.abs(x)))


def _pick_tile(n, preferred):
    for t in preferred:
        if n % t == 0:
            return t
    return n


def _norm_matmul_kernel(x_ref, g_ref, w_ref, o_ref, *, tn):
    h = _rms(x_ref[...], g_ref[...]).astype(BF16)
    for c in range(0, o_ref.shape[1], tn):
        o_ref[:, c:c + tn] = _dot(h, w_ref[:, c:c + tn]).astype(o_ref.dtype)


def norm_matmul(x, gain, w, out_dtype, tn):
    m, d = x.shape
    n = w.shape[1]
    out_bytes = jnp.dtype(out_dtype).itemsize

    def vmem_need(tm):
        return 2 * tm * d * 4 + d * n * 2 + 2 * tm * n * out_bytes + tm * d * 2 + tm * tn * 4

    tm = next(t for t in (1024, 512, 256, 128) if m % t == 0 and vmem_need(t) <= 0.9 * VMEM_LIMIT)
    return pl.pallas_call(
        functools.partial(_norm_matmul_kernel, tn=tn),
        grid=(m // tm,),
        in_specs=[
            pl.BlockSpec((tm, d), lambda i: (i, 0)),
            pl.BlockSpec((1, d), lambda i: (0, 0)),
            pl.BlockSpec((d, n), lambda i: (0, 0), pipeline_mode=pl.Buffered(1)),
        ],
        out_specs=pl.BlockSpec((tm, n), lambda i: (i, 0)),
        out_shape=jax.ShapeDtypeStruct((m, n), out_dtype),
        compiler_params=_params("parallel"),
        name="norm_matmul",
    )(x, gain.reshape(1, d), w)


def _mixer_out_kernel(a_ref, q_ref, mk_ref, mv_ref, w_ref, x_ref, o_ref, *, heads):
    ka = a_ref.shape[1]
    main = _dot(a_ref[...], w_ref[0:ka, :])
    dh = q_ref.shape[1] // heads
    scale = dh ** -0.5
    mem = []
    for hd in range(heads):
        cols = slice(hd * dh, (hd + 1) * dh)
        s = _dot_nt(q_ref[:, cols], mk_ref[:, cols]) * scale
        s = s - jnp.max(s, axis=-1, keepdims=True)
        p = jnp.exp(s)
        p = p / jnp.sum(p, axis=-1, keepdims=True)
        mem.append(_dot(p.astype(BF16), mv_ref[:, cols]).astype(BF16))
    o_ref[...] = x_ref[...] + main + _dot(jnp.concatenate(mem, axis=1), w_ref[ka:, :])


def mixer_out(a, z, q_col_block, mem_kv, layer, w, x, seq_len, n_mem):
    m, d = x.shape
    ka = a.shape[1]
    mem_width = MEM_HEADS * LANES
    tm = _pick_tile(seq_len, (512, 256))
    tiles = seq_len // tm
    kern = functools.partial(_mixer_out_kernel, heads=MEM_HEADS)
    return pl.pallas_call(
        kern,
        grid=(m // tm,),
        in_specs=[
            pl.BlockSpec((tm, ka), lambda i: (i, 0)),
            pl.BlockSpec((tm, mem_width), lambda i: (i, q_col_block)),
            pl.BlockSpec((n_mem, mem_width), lambda i: (i // tiles, 2 * layer)),
            pl.BlockSpec((n_mem, mem_width), lambda i: (i // tiles, 2 * layer + 1)),
            pl.BlockSpec((ka + mem_width, d), lambda i: (0, 0), pipeline_mode=pl.Buffered(1)),
            pl.BlockSpec((tm, d), lambda i: (i, 0)),
        ],
        out_specs=pl.BlockSpec((tm, d), lambda i: (i, 0)),
        out_shape=jax.ShapeDtypeStruct((m, d), F32),
        compiler_params=_params("parallel"),
        name="mixer_out",
    )(a, z, mem_kv, mem_kv, w, x)


HALO = SUBLANES_BF16
FFN_CHUNK_PARTS = (1, 1)


def _ffn_kernel(x_ref, xh_ref, g_ref, w_ref, taps_ref, wd_ref, gf_ref, o_ref, h_ref, *, tiles_per_seq,
                final_norm):
    i = pl.program_id(0)
    f = pl.program_id(1)
    tm = x_ref.shape[0]
    tf = wd_ref.shape[0]
    hidden = taps_ref.shape[1] // 2

    @pl.when(f == 0)
    def _():
        x = x_ref[...]
        prev = _rms(xh_ref[...], g_ref[...])
        prev = jnp.where(i % tiles_per_seq == 0, 0.0, prev)
        h_ref[0:HALO, :] = prev.astype(BF16)
        h_ref[HALO:HALO + tm, :] = _rms(x, g_ref[...]).astype(BF16)
        o_ref[...] = x

    def conv(u, first_col):
        taps = taps_ref[:, pl.ds(pl.multiple_of(first_col, LANES), tf)]
        y = taps[2:3, :] * u + taps[1:2, :] * pltpu.roll(u, 1, axis=0) + taps[0:1, :] * pltpu.roll(u, 2, axis=0)
        return y[HALO:, :] + taps[3:4, :]

    def up(lo, rc):
        h = h_ref[lo:lo + rc + HALO, :]
        return _dot(h, w_ref[0]), _dot(h, w_ref[1])

    sizes = [c * tm // sum(FFN_CHUNK_PARTS) for c in FFN_CHUNK_PARTS]
    starts = [sum(sizes[:n]) for n in range(len(sizes))]
    u_next = up(starts[0], sizes[0])
    for n, (lo, rc) in enumerate(zip(starts, sizes)):
        ua, uv = u_next
        if n + 1 < len(starts):
            u_next = up(starts[n + 1], sizes[n + 1])
        ya = conv(ua, f * tf)
        yv = conv(uv, hidden + f * tf)
        act = (ya * jax.nn.sigmoid(ya) * yv).astype(BF16)
        o_ref[lo:lo + rc, :] += _dot(act, wd_ref[...])

    if final_norm:
        @pl.when(f == pl.num_programs(1) - 1)
        def _():
            o_ref[...] = _rms(o_ref[...], gf_ref[...])


def conv_ffn(x, gain, w_up, conv_w, conv_b, w_down, final_gain, seq_len, final_norm):
    m, d = x.shape
    hidden = w_down.shape[0]
    tm = _pick_tile(seq_len, (1024, 512, 256))
    tf = _pick_tile(hidden, (512, 256, 128))
    nf = hidden // tf
    halo_blocks = tm // HALO
    assert tm % (sum(FFN_CHUNK_PARTS) * SUBLANES_BF16) == 0
    w_tiles = w_up.reshape(d, 2, nf, tf).transpose(2, 1, 0, 3)
    taps = jnp.concatenate([conv_w, conv_b.reshape(1, -1)], axis=0)
    kern = functools.partial(_ffn_kernel, tiles_per_seq=seq_len // tm, final_norm=final_norm)
    return pl.pallas_call(
        kern,
        grid=(m // tm, nf),
        in_specs=[
            pl.BlockSpec((tm, d), lambda i, f: (i, 0)),
            pl.BlockSpec((HALO, d), lambda i, f: (jnp.maximum(i * halo_blocks - 1, 0), 0)),
            pl.BlockSpec((1, d), lambda i, f: (0, 0)),
            pl.BlockSpec((None, 2, d, tf), lambda i, f: (f, 0, 0, 0)),
            pl.BlockSpec(taps.shape, lambda i, f: (0, 0)),
            pl.BlockSpec((tf, d), lambda i, f: (f, 0)),
            pl.BlockSpec((1, d), lambda i, f: (0, 0)),
        ],
        out_specs=pl.BlockSpec((tm, d), lambda i, f: (i, 0)),
        out_shape=jax.ShapeDtypeStruct((m, d), F32),
        scratch_shapes=[pltpu.VMEM((tm + HALO, d), BF16)],
        compiler_params=_params("parallel", "arbitrary"),
        name="conv_ffn",
    )(x, x, gain.reshape(1, d), w_tiles, taps, w_down, final_gain.reshape(1, d))


GLA_BLOCK = 256


def _ref_rows(bc, half):
    rows, dk = bc.shape
    if half >= SUBLANES_F32:
        nb = rows // (2 * half)
        b3 = bc.reshape(nb, 2 * half, dk)
        return jnp.broadcast_to(b3[:, half:half + 1, :], b3.shape).reshape(rows, dk)
    b3 = bc.reshape(rows // SUBLANES_F32, SUBLANES_F32, dk)
    sub = lax.broadcasted_iota(jnp.int32, b3.shape, 1)
    out = None
    for mid in range(half, SUBLANES_F32, 2 * half):
        cand = jnp.broadcast_to(b3[:, mid:mid + 1, :], b3.shape)
        out = cand if out is None else jnp.where(sub >= mid - half, cand, out)
    return out.reshape(rows, dk)


GLA_HEADS_PER_STEP = 2


def _gla_kernel(q_ref, k_ref, v_ref, glr_ref, og_ref, wg_ref, bg_ref, gn_ref, o_ref, state_ref, *, rank, heads):
    t = pl.program_id(2)
    rows = q_ref.shape[0]
    dk = q_ref.shape[1] // heads
    dv = v_ref.shape[1] // heads
    blk = GLA_BLOCK if rows % GLA_BLOCK == 0 else rows
    n_levels = int(math.log2(blk))

    @pl.when(t == 0)
    def _():
        state_ref[...] = jnp.zeros_like(state_ref)

    ti = lax.broadcasted_iota(jnp.int32, (blk, blk), 0)
    si = lax.broadcasted_iota(jnp.int32, (blk, blk), 1)
    tri = jnp.where(si <= ti, 1.0, 0.0).astype(BF16)
    x = ti ^ si
    top_bit = 31 - lax.clz(jnp.maximum(x, 1))
    pair_level = jnp.where(si < ti, top_bit, jnp.where(si == ti, -1, -2))
    qscale = dk ** -0.5

    def head_step(hd, rs, glr):
        kc = slice(hd * dk, (hd + 1) * dk)
        vc = slice(hd * dv, (hd + 1) * dv)
        q = q_ref[rs, kc].astype(F32) * qscale
        k = k_ref[rs, kc].astype(F32)
        v = v_ref[rs, vc].astype(BF16)
        g = _log_sigmoid(_dot(glr, wg_ref[hd]) + bg_ref[hd]) * (LOG2_E / GLA_GATE_TAU)
        g_hi, g_lo = _split2(g)
        bc = _dot(tri, g_hi) + _dot(tri, g_lo)
        b_last = bc[blk - 1:blk, :]
        state = state_ref[hd]

        o = _dot_nt((q * jnp.exp2(bc)).astype(BF16), state.astype(BF16))

        att = jnp.zeros((blk, blk), F32)
        q_b, k_b = q.astype(BF16), k.astype(BF16)
        for p in range(n_levels):
            e = jnp.exp2(_neg_abs(bc - _ref_rows(bc, 1 << p)).astype(BF16))
            att = jnp.where(pair_level == p, _dot_nt(q_b * e, k_b * e), att)
        att = jnp.where(pair_level == -1, _dot_nt(q_b, k_b), att)
        o = o + _dot(att.astype(BF16), v)

        kd = (k * jnp.exp2(b_last - bc)).astype(BF16)
        state_ref[hd] = state * jnp.exp2(b_last) + _dot_tn(v, kd)

        o = o * lax.rsqrt(jnp.mean(o * o, axis=-1, keepdims=True) + EPS) * gn_ref[hd]
        og = og_ref[rs, vc].astype(F32)
        o_ref[rs, vc] = (o * (og * jax.nn.sigmoid(og))).astype(o_ref.dtype)

    def step(i, carry):
        rs = pl.ds(pl.multiple_of(i * blk, blk), blk)
        glr = glr_ref[rs, 0:rank].astype(BF16)
        for hd in range(heads):
            head_step(hd, rs, glr)
        return carry

    lax.fori_loop(0, rows // blk, step, 0)


def gla_attention(z, w_gate, b_gate, gn_gain, batch, seq_len, dk, dv, rank):
    m = z.shape[0]
    heads = GLA_HEADS
    hp = GLA_HEADS_PER_STEP
    groups = heads // hp
    t = _pick_tile(seq_len, (1024, 512, 256))
    tiles = seq_len // t
    v0, og0 = 0, groups
    q0 = (2 * heads * dv) // (hp * dk)
    k0 = q0 + groups
    glr0 = (2 * heads * dv + 2 * heads * dk + MEM_HEADS * LANES) // LANES
    assert (2 * heads * dv) % (hp * dk) == 0
    rows = lambda b, h, s: b * tiles + s
    kern = functools.partial(_gla_kernel, rank=rank, heads=hp)
    return pl.pallas_call(
        kern,
        grid=(batch, groups, tiles),
        in_specs=[
            pl.BlockSpec((t, hp * dk), lambda b, h, s: (rows(b, h, s), q0 + h)),
            pl.BlockSpec((t, hp * dk), lambda b, h, s: (rows(b, h, s), k0 + h)),
            pl.BlockSpec((t, hp * dv), lambda b, h, s: (rows(b, h, s), v0 + h)),
            pl.BlockSpec((t, LANES), lambda b, h, s: (rows(b, h, s), glr0)),
            pl.BlockSpec((t, hp * dv), lambda b, h, s: (rows(b, h, s), og0 + h)),
            pl.BlockSpec((hp, rank, dk), lambda b, h, s: (h, 0, 0)),
            pl.BlockSpec((hp, 1, dk), lambda b, h, s: (h, 0, 0)),
            pl.BlockSpec((hp, 1, dv), lambda b, h, s: (h, 0, 0)),
        ],
        out_specs=pl.BlockSpec((t, hp * dv), lambda b, h, s: (rows(b, h, s), h)),
        out_shape=jax.ShapeDtypeStruct((m, heads * dv), BF16),
        scratch_shapes=[pltpu.VMEM((hp, dv, dk), F32)],
        compiler_params=_params("parallel", "parallel", "arbitrary"),
        name="gla_attention",
    )(z, z, z, z, z, w_gate, b_gate, gn_gain)


BIAS_PIECES = 3


def _split3(x):
    hi = x.astype(BF16)
    r1 = x - hi.astype(F32)
    mid = r1.astype(BF16)
    lo = (r1 - mid.astype(F32)).astype(BF16)
    return hi, mid, lo


def _fox_gate_kernel(x_ref, g_ref, w_ref, b_ref, route_ref, o_ref, carry_ref):
    s = pl.program_id(1)
    t = x_ref.shape[0]

    @pl.when(s == 0)
    def _():
        carry_ref[...] = jnp.zeros_like(carry_ref)

    h = _rms(x_ref[...], g_ref[...]).astype(BF16)
    log_f = _log_sigmoid(_dot(h, w_ref[...]) + b_ref[...])
    ti = lax.broadcasted_iota(jnp.int32, (t, t), 0)
    si = lax.broadcasted_iota(jnp.int32, (t, t), 1)
    tri = jnp.where(si <= ti, 1.0, 0.0).astype(BF16)
    f_hi, f_mid, f_lo = _split3(log_f)
    c = carry_ref[...] + _dot(tri, f_hi) + _dot(tri, f_mid) + _dot(tri, f_lo)
    carry_ref[...] = c[t - 1:t, :]
    pieces = jnp.concatenate(_split3(c * (-LOG2_E)), axis=1)
    o_ref[...] = _dot(pieces, route_ref[...]).astype(BF16)


def fox_gates(x, gain, w_f, b_f, batch, seq_len):
    m, d = x.shape
    t = _pick_tile(seq_len, (512, 256))
    tiles = seq_len // t
    src = lax.broadcasted_iota(jnp.int32, (BIAS_PIECES * LANES, FOX_HEADS * LANES), 0)
    dst = lax.broadcasted_iota(jnp.int32, (BIAS_PIECES * LANES, FOX_HEADS * LANES), 1)
    route = ((src // LANES == dst % LANES) & (src % LANES == dst // LANES)).astype(BF16)
    return pl.pallas_call(
        _fox_gate_kernel,
        grid=(batch, tiles),
        in_specs=[
            pl.BlockSpec((t, d), lambda b, s: (b * tiles + s, 0)),
            pl.BlockSpec((1, d), lambda b, s: (0, 0)),
            pl.BlockSpec((d, LANES), lambda b, s: (0, 0)),
            pl.BlockSpec((1, LANES), lambda b, s: (0, 0)),
            pl.BlockSpec(route.shape, lambda b, s: (0, 0)),
        ],
        out_specs=pl.BlockSpec((t, FOX_HEADS * LANES), lambda b, s: (b * tiles + s, 0)),
        out_shape=jax.ShapeDtypeStruct((m, FOX_HEADS * LANES), BF16),
        scratch_shapes=[pltpu.VMEM((1, LANES), F32)],
        compiler_params=_params("parallel", "arbitrary"),
        name="fox_gates",
    )(x, gain.reshape(1, d), w_f, b_f, route)


FOX_HEADS_PER_STEP = 2


def _fox_attn_kernel(q_ref, k_ref, kb_ref, v_ref, o_ref, m_ref, acc_ref, s0_ref, s1_ref, *, heads):
    qi = pl.program_id(2)
    tq = q_ref.shape[0]
    tk = tq // 2
    dh = q_ref.shape[1] // heads
    cols = [slice(hd * dh, (hd + 1) * dh) for hd in range(heads)]
    lane = lax.broadcasted_iota(jnp.int32, (tq, dh), 1)
    bias_taps = jnp.where(lane < BIAS_PIECES, 1.0, 0.0).astype(BF16)
    ones_col = jnp.where(lax.broadcasted_iota(jnp.int32, (tk, dh), 1) == 0, 1.0, 0.0).astype(BF16)
    qs = [jnp.concatenate([(q_ref[:, cols[hd]].astype(F32) * (dh ** -0.5 * LOG2_E)).astype(BF16), bias_taps],
                          axis=1) for hd in range(heads)]

    def key_rows(j):
        return pl.ds(pl.multiple_of(j * tk, tk), tk)

    def issue_scores(s_ref, j, rows):
        ks = key_rows(j)
        for hd in range(heads):
            keys = jnp.concatenate([k_ref[ks, cols[hd]], kb_ref[ks, cols[hd]]], axis=1)
            s_ref[hd, rows, :] = _dot_nt(qs[hd][rows], keys)

    def update(hd, s, j, rows):
        m_prev = m_ref[hd, rows, :]
        m_new = jnp.maximum(m_prev, jnp.max(s, axis=-1, keepdims=True))
        alpha = jnp.exp2(m_prev - m_new)
        p = jnp.concatenate([jnp.exp2((s[:, c:c + LANES] - m_new).astype(BF16))
                             for c in range(0, tk, LANES)], axis=1)
        pv = _dot(p, jnp.concatenate([v_ref[key_rows(j), cols[hd]], ones_col], axis=1))
        acc = acc_ref[hd, rows, :]
        acc_ref[hd, rows, :] = jnp.concatenate([alpha * acc[:, c:c + LANES] for c in range(0, 2 * dh, LANES)],
                                               axis=1) + pv
        m_ref[hd, rows, :] = m_new

    every = slice(0, tq)
    lower = slice(tk, tq)

    m_ref[...] = jnp.full(m_ref.shape, -jnp.inf, F32)
    acc_ref[...] = jnp.zeros_like(acc_ref)
    issue_scores(s0_ref, 0, every)

    def pair(jj, carry):
        j = 2 * jj
        issue_scores(s1_ref, j + 1, every)
        for hd in range(heads):
            update(hd, s0_ref[hd], j, every)
        issue_scores(s0_ref, j + 2, every)
        for hd in range(heads):
            update(hd, s1_ref[hd], j + 1, every)
        return carry

    lax.fori_loop(0, qi, pair, 0)

    issue_scores(s1_ref, 2 * qi + 1, lower)
    ti = lax.broadcasted_iota(jnp.int32, (tq, tk), 0)
    si = lax.broadcasted_iota(jnp.int32, (tq, tk), 1)
    for hd in range(heads):
        update(hd, jnp.where(si <= ti, s0_ref[hd], -jnp.inf), 2 * qi, every)
    tl = lax.broadcasted_iota(jnp.int32, (tk, tk), 0)
    sl = lax.broadcasted_iota(jnp.int32, (tk, tk), 1)
    for hd in range(heads):
        update(hd, jnp.where(sl <= tl, s1_ref[hd, lower, :], -jnp.inf), 2 * qi + 1, lower)
    for hd in range(heads):
        acc = acc_ref[hd]
        o_ref[:, cols[hd]] = (acc[:, 0:dh] / acc[:, dh:dh + 1]).astype(o_ref.dtype)


def fox_attention(zq, kv, kbias, batch, seq_len):
    m = zq.shape[0]
    hp = FOX_HEADS_PER_STEP
    width = hp * LANES
    groups = FOX_HEADS // hp
    tq = _pick_tile(seq_len, (1024, 512, 256))
    tiles = seq_len // tq
    kern = functools.partial(_fox_attn_kernel, heads=hp)
    return pl.pallas_call(
        kern,
        grid=(batch, groups, tiles),
        in_specs=[
            pl.BlockSpec((tq, width), lambda b, h, s: (b * tiles + s, h)),
            pl.BlockSpec((seq_len, width), lambda b, h, s: (b, h)),
            pl.BlockSpec((seq_len, width), lambda b, h, s: (b, h)),
            pl.BlockSpec((seq_len, width), lambda b, h, s: (b, groups + h)),
        ],
        out_specs=pl.BlockSpec((tq, width), lambda b, h, s: (b * tiles + s, h)),
        out_shape=jax.ShapeDtypeStruct((m, FOX_HEADS * LANES), BF16),
        scratch_shapes=[pltpu.VMEM((hp, tq, LANES), F32), pltpu.VMEM((hp, tq, 2 * LANES), F32),
                        pltpu.VMEM((hp, tq, tq // 2), F32), pltpu.VMEM((hp, tq, tq // 2), F32)],
        compiler_params=_params("parallel", "parallel", "arbitrary"),
        name="fox_attention",
    )(zq, kv, kbias, kv)


def kernel(x, mem, norm_mix, norm_ffn, norm_mem, norm_final, mem_w_kv, gla_w_in, gla_w_gate_up,
           gla_b_gate, gla_norm, gla_w_out, fox_kv_norm, fox_w_kv, fox_b_f, fox_w_in, fox_w_out,
           ffn_w_up, ffn_conv_w, ffn_conv_b, ffn_w_down):
    batch, seq_len, d = x.shape
    n_mem = mem.shape[1]
    depth = norm_mix.shape[0]
    n_gla = gla_w_in.shape[0]
    rank = gla_w_gate_up.shape[1]
    qk_width = gla_w_gate_up.shape[2]
    dk = qk_width // GLA_HEADS
    v_width = gla_norm.shape[1]
    dv = v_width // GLA_HEADS
    mem_width = mem_w_kv.shape[2] // 2
    fox_width = (fox_w_kv.shape[1] - FOX_HEADS) // 2
    assert mem_width == MEM_HEADS * LANES and fox_width == FOX_HEADS * LANES
    assert (2 * v_width) % dk == 0 and rank <= LANES

    xf = x.reshape(batch * seq_len, d)

    w_mem = jnp.transpose(mem_w_kv, (1, 0, 2)).reshape(d, depth * 2 * mem_width).astype(BF16)
    mem_kv = norm_matmul(mem.reshape(batch * n_mem, d), norm_mem, w_mem, BF16,
                         _pick_tile(w_mem.shape[1], (1024, 512)))

    fox_kv = fox_c = None
    for i in range(depth):
        if i == n_gla:
            fox_kv = norm_matmul(xf, fox_kv_norm, fox_w_kv[:, :2 * fox_width].astype(BF16), BF16,
                                 _pick_tile(2 * fox_width, (1024, 768, 512)))
            w_f = jnp.pad(fox_w_kv[:, 2 * fox_width:], ((0, 0), (0, LANES - FOX_HEADS))).astype(BF16)
            b_f = jnp.pad(fox_b_f, (0, LANES - FOX_HEADS)).reshape(1, LANES)
            fox_c = fox_gates(xf, fox_kv_norm, w_f, b_f, batch, seq_len)
        if i < n_gla:
            w = gla_w_in[i]
            o_q, o_k, o_v = 0, qk_width, 2 * qk_width
            o_glr = o_v + v_width
            o_og = o_glr + rank
            o_mq = o_og + v_width
            w_in = jnp.concatenate([
                w[:, o_v:o_glr], w[:, o_og:o_mq], w[:, o_q:o_k], w[:, o_k:o_v], w[:, o_mq:],
                jnp.pad(w[:, o_glr:o_og], ((0, 0), (0, LANES - rank)))], axis=1).astype(BF16)
            z = norm_matmul(xf, norm_mix[i], w_in, BF16, _pick_tile(w_in.shape[1], (1920, 1152, 640, 384, 128)))
            a = gla_attention(z, gla_w_gate_up[i].reshape(rank, GLA_HEADS, dk).transpose(1, 0, 2).astype(BF16),
                              gla_b_gate[i].reshape(GLA_HEADS, 1, dk), gla_norm[i].reshape(GLA_HEADS, 1, dv),
                              batch, seq_len, dk, dv, rank)
            mq_block = (2 * v_width + 2 * qk_width) // mem_width
            w_out = gla_w_out[i].astype(BF16)
        else:
            j = i - n_gla
            z = norm_matmul(xf, norm_mix[i], fox_w_in[j].astype(BF16), BF16,
                            _pick_tile(fox_w_in.shape[2], (1024, 512)))
            a = fox_attention(z, fox_kv, fox_c, batch, seq_len)
            mq_block = fox_width // mem_width
            w_out = fox_w_out[j].astype(BF16)
        xf = mixer_out(a, z, mq_block, mem_kv, i, w_out, xf, seq_len, n_mem)
        xf = conv_ffn(xf, norm_ffn[i], ffn_w_up[i].astype(BF16), ffn_conv_w[i], ffn_conv_b[i],
                      ffn_w_down[i].astype(BF16), norm_final, seq_len, final_norm=(i == depth - 1))
    return xf.reshape(batch, seq_len, d)
```

```python
import functools
import math

import jax
import jax.numpy as jnp
from jax import lax
from jax.experimental import pallas as pl
from jax.experimental.pallas import tpu as pltpu

GLA_HEADS = 4
GLA_GATE_TAU = 16.0
FOX_HEADS = 12
MEM_HEADS = 4
CONV_WIDTH = 3
EPS = 1e-6
LOG2_E = math.log2(math.e)

LANES = 128
SUBLANES_F32 = 8
SUBLANES_BF16 = 16
VMEM_LIMIT = 56 * 1024 * 1024

BF16 = jnp.bfloat16
F32 = jnp.float32

NT_DIMS = (((1,), (1,)), ((), ()))
TN_DIMS = (((0,), (0,)), ((), ()))


def _params(*sem):
    return pltpu.CompilerParams(dimension_semantics=sem, vmem_limit_bytes=VMEM_LIMIT)


def _rms(x, gain):
    return x * lax.rsqrt(jnp.mean(x * x, axis=-1, keepdims=True) + EPS) * gain


def _dot(a, b):
    return jnp.dot(a, b, preferred_element_type=F32)


def _dot_nt(a, b):
    return lax.dot_general(a, b, NT_DIMS, preferred_element_type=F32)


def _dot_tn(a, b):
    return lax.dot_general(a, b, TN_DIMS, preferred_element_type=F32)


def _split2(x):
    hi = x.astype(BF16)
    lo = (x - hi.astype(F32)).astype(BF16)
    return hi, lo


def _neg_abs(x):
    bits = lax.bitcast_convert_type(x, jnp.uint32) | jnp.uint32(0x80000000)
    return lax.bitcast_convert_type(bits, F32)


def _log_sigmoid(x):
    return jnp.minimum(x, 0.0) - jnp.log1p(jnp.exp(-jnp.abs(x)))


def _pick_tile(n, preferred):
    for t in preferred:
        if n % t == 0:
            return t
    return n


def _cast_kernel(x_ref, o_ref):
    o_ref[...] = x_ref[...].astype(o_ref.dtype)


def cast_blocks(w, grid, in_block, in_map, out_shape, out_block, out_map):
    return pl.pallas_call(
        _cast_kernel,
        grid=grid,
        in_specs=[pl.BlockSpec(in_block, in_map)],
        out_specs=pl.BlockSpec(out_block, out_map),
        out_shape=jax.ShapeDtypeStruct(out_shape, BF16),
        compiler_params=_params(*(("parallel",) * len(grid))),
        name="cast_blocks",
    )(w)


def cast_rows(w):
    layers, rows, cols = w.shape
    tr = _pick_tile(rows, (512, 256, 128))
    return cast_blocks(w, (layers, rows // tr), (None, tr, cols), lambda l, r: (l, r, 0),
                       w.shape, (None, tr, cols), lambda l, r: (l, r, 0))


def _norm_matmul_kernel(x_ref, g_ref, w_ref, o_ref, *, tn):
    h = _rms(x_ref[...], g_ref[...]).astype(BF16)
    n = o_ref.shape[1]
    for c in range(0, n, tn):
        cols = slice(c, min(c + tn, n))
        o_ref[:, cols] = _dot(h, w_ref[:, cols]).astype(o_ref.dtype)


def norm_matmul(x, gain, w, out_dtype, tn, layer=None):
    m, d = x.shape
    n = w.shape[-1]
    if layer is None:
        w_spec = pl.BlockSpec((d, n), lambda i: (0, 0), pipeline_mode=pl.Buffered(1))
    else:
        w_spec = pl.BlockSpec((None, d, n), lambda i: (layer, 0, 0), pipeline_mode=pl.Buffered(1))
    out_bytes = jnp.dtype(out_dtype).itemsize

    def vmem_need(tm):
        return 2 * tm * d * 4 + d * n * 2 + 2 * tm * n * out_bytes + tm * d * 2 + tm * tn * 4

    tm = next(t for t in (1024, 512, 256, 128) if m % t == 0 and vmem_need(t) <= 0.9 * VMEM_LIMIT)
    return pl.pallas_call(
        functools.partial(_norm_matmul_kernel, tn=tn),
        grid=(m // tm,),
        in_specs=[
            pl.BlockSpec((tm, d), lambda i: (i, 0)),
            pl.BlockSpec((1, d), lambda i: (0, 0)),
            w_spec,
        ],
        out_specs=pl.BlockSpec((tm, n), lambda i: (i, 0)),
        out_shape=jax.ShapeDtypeStruct((m, n), out_dtype),
        compiler_params=_params("parallel"),
        name="norm_matmul",
    )(x, gain.reshape(1, d), w)


def _mixer_out_kernel(a_ref, q_ref, mk_ref, mv_ref, w_ref, x_ref, o_ref, *, heads):
    ka = a_ref.shape[1]
    main = _dot(a_ref[...], w_ref[0:ka, :])
    dh = q_ref.shape[1] // heads
    scale = dh ** -0.5
    mem = []
    for hd in range(heads):
        cols = slice(hd * dh, (hd + 1) * dh)
        s = _dot_nt(q_ref[:, cols], mk_ref[:, cols]) * scale
        s = s - jnp.max(s, axis=-1, keepdims=True)
        p = jnp.exp(s)
        p = p / jnp.sum(p, axis=-1, keepdims=True)
        mem.append(_dot(p.astype(BF16), mv_ref[:, cols]).astype(BF16))
    o_ref[...] = x_ref[...] + main + _dot(jnp.concatenate(mem, axis=1), w_ref[ka:, :])


def mixer_out(a, z, q_col_block, mem_kv, layer, w, w_layer, x, seq_len, n_mem):
    m, d = x.shape
    ka = a.shape[1]
    mem_width = MEM_HEADS * LANES
    tm = _pick_tile(seq_len, (512, 256))
    tiles = seq_len // tm
    kern = functools.partial(_mixer_out_kernel, heads=MEM_HEADS)
    return pl.pallas_call(
        kern,
        grid=(m // tm,),
        in_specs=[
            pl.BlockSpec((tm, ka), lambda i: (i, 0)),
            pl.BlockSpec((tm, mem_width), lambda i: (i, q_col_block)),
            pl.BlockSpec((n_mem, mem_width), lambda i: (i // tiles, 2 * layer)),
            pl.BlockSpec((n_mem, mem_width), lambda i: (i // tiles, 2 * layer + 1)),
            pl.BlockSpec((None, ka + mem_width, d), lambda i: (w_layer, 0, 0), pipeline_mode=pl.Buffered(1)),
            pl.BlockSpec((tm, d), lambda i: (i, 0)),
        ],
        out_specs=pl.BlockSpec((tm, d), lambda i: (i, 0)),
        out_shape=jax.ShapeDtypeStruct((m, d), F32),
        compiler_params=_params("parallel"),
        name="mixer_out",
    )(a, z, mem_kv, mem_kv, w, x)


HALO = SUBLANES_BF16
FFN_CHUNK_PARTS = (1, 1)


def _ffn_kernel(x_ref, xh_ref, g_ref, w_ref, taps_ref, wd_ref, gf_ref, o_ref, h_ref, *, tiles_per_seq,
                final_norm):
    i = pl.program_id(0)
    f = pl.program_id(1)
    tm = x_ref.shape[0]
    tf = wd_ref.shape[0]
    hidden = taps_ref.shape[1] // 2

    @pl.when(f == 0)
    def _():
        x = x_ref[...]
        prev = _rms(xh_ref[...], g_ref[...])
        prev = jnp.where(i % tiles_per_seq == 0, 0.0, prev)
        h_ref[0:HALO, :] = prev.astype(BF16)
        h_ref[HALO:HALO + tm, :] = _rms(x, g_ref[...]).astype(BF16)
        o_ref[...] = x

    def conv(u, first_col):
        taps = taps_ref[:, pl.ds(pl.multiple_of(first_col, LANES), tf)]
        y = taps[2:3, :] * u + taps[1:2, :] * pltpu.roll(u, 1, axis=0) + taps[0:1, :] * pltpu.roll(u, 2, axis=0)
        return y[HALO:, :] + taps[3:4, :]

    def up(lo, rc):
        h = h_ref[lo:lo + rc + HALO, :]
        return _dot(h, w_ref[0]), _dot(h, w_ref[1])

    sizes = [c * tm // sum(FFN_CHUNK_PARTS) for c in FFN_CHUNK_PARTS]
    starts = [sum(sizes[:n]) for n in range(len(sizes))]
    u_next = up(starts[0], sizes[0])
    for n, (lo, rc) in enumerate(zip(starts, sizes)):
        ua, uv = u_next
        if n + 1 < len(starts):
            u_next = up(starts[n + 1], sizes[n + 1])
        ya = conv(ua, f * tf)
        yv = conv(uv, hidden + f * tf)
        act = (ya * jax.nn.sigmoid(ya) * yv).astype(BF16)
        o_ref[lo:lo + rc, :] += _dot(act, wd_ref[...])

    if final_norm:
        @pl.when(f == pl.num_programs(1) - 1)
        def _():
            o_ref[...] = _rms(o_ref[...], gf_ref[...])


def ffn_col_tile(hidden):
    return _pick_tile(hidden, (512, 256, 128))


def stage_ffn_up(w_up):
    layers, d, two_hidden = w_up.shape
    tf = ffn_col_tile(two_hidden // 2)
    nf = two_hidden // 2 // tf
    return cast_blocks(w_up, (layers, nf, 2), (None, d, tf), lambda l, f, g: (l, 0, g * nf + f),
                       (layers, nf, 2, d, tf), (None, None, None, d, tf), lambda l, f, g: (l, f, g, 0, 0))


def conv_ffn(x, gain, w_tiles, conv_w, conv_b, w_down, layer, final_gain, seq_len, final_norm):
    m, d = x.shape
    hidden = w_down.shape[1]
    tm = _pick_tile(seq_len, (1024, 512, 256))
    _, nf, _, _, tf = w_tiles.shape
    halo_blocks = tm // HALO
    assert tm % (sum(FFN_CHUNK_PARTS) * SUBLANES_BF16) == 0
    taps = jnp.concatenate([conv_w, conv_b.reshape(1, -1)], axis=0)
    kern = functools.partial(_ffn_kernel, tiles_per_seq=seq_len // tm, final_norm=final_norm)
    return pl.pallas_call(
        kern,
        grid=(m // tm, nf),
        in_specs=[
            pl.BlockSpec((tm, d), lambda i, f: (i, 0)),
            pl.BlockSpec((HALO, d), lambda i, f: (jnp.maximum(i * halo_blocks - 1, 0), 0)),
            pl.BlockSpec((1, d), lambda i, f: (0, 0)),
            pl.BlockSpec((None, None, 2, d, tf), lambda i, f: (layer, f, 0, 0, 0)),
            pl.BlockSpec(taps.shape, lambda i, f: (0, 0)),
            pl.BlockSpec((None, tf, d), lambda i, f: (layer, f, 0)),
            pl.BlockSpec((1, d), lambda i, f: (0, 0)),
        ],
        out_specs=pl.BlockSpec((tm, d), lambda i, f: (i, 0)),
        out_shape=jax.ShapeDtypeStruct((m, d), F32),
        scratch_shapes=[pltpu.VMEM((tm + HALO, d), BF16)],
        compiler_params=_params("parallel", "arbitrary"),
        name="conv_ffn",
    )(x, x, gain.reshape(1, d), w_tiles, taps, w_down, final_gain.reshape(1, d))


GLA_BLOCK = 256


def _ref_rows(bc, half):
    rows, dk = bc.shape
    if half >= SUBLANES_F32:
        nb = rows // (2 * half)
        b3 = bc.reshape(nb, 2 * half, dk)
        return jnp.broadcast_to(b3[:, half:half + 1, :], b3.shape).reshape(rows, dk)
    b3 = bc.reshape(rows // SUBLANES_F32, SUBLANES_F32, dk)
    sub = lax.broadcasted_iota(jnp.int32, b3.shape, 1)
    out = None
    for mid in range(half, SUBLANES_F32, 2 * half):
        cand = jnp.broadcast_to(b3[:, mid:mid + 1, :], b3.shape)
        out = cand if out is None else jnp.where(sub >= mid - half, cand, out)
    return out.reshape(rows, dk)


GLA_HEADS_PER_STEP = 2


def _gla_kernel(q_ref, k_ref, v_ref, glr_ref, og_ref, wg_ref, bg_ref, gn_ref, o_ref, state_ref, *, rank, heads):
    t = pl.program_id(2)
    rows = q_ref.shape[0]
    dk = q_ref.shape[1] // heads
    dv = v_ref.shape[1] // heads
    blk = GLA_BLOCK if rows % GLA_BLOCK == 0 else rows
    n_levels = int(math.log2(blk))

    @pl.when(t == 0)
    def _():
        state_ref[...] = jnp.zeros_like(state_ref)

    ti = lax.broadcasted_iota(jnp.int32, (blk, blk), 0)
    si = lax.broadcasted_iota(jnp.int32, (blk, blk), 1)
    tri = jnp.where(si <= ti, 1.0, 0.0).astype(BF16)
    x = ti ^ si
    top_bit = 31 - lax.clz(jnp.maximum(x, 1))
    pair_level = jnp.where(si < ti, top_bit, jnp.where(si == ti, -1, -2))
    qscale = dk ** -0.5

    def head_step(hd, rs, glr):
        kc = slice(hd * dk, (hd + 1) * dk)
        vc = slice(hd * dv, (hd + 1) * dv)
        q = q_ref[rs, kc].astype(F32) * qscale
        k = k_ref[rs, kc].astype(F32)
        v = v_ref[rs, vc].astype(BF16)
        g = _log_sigmoid(_dot(glr, wg_ref[hd]) + bg_ref[hd]) * (LOG2_E / GLA_GATE_TAU)
        g_hi, g_lo = _split2(g)
        bc = _dot(tri, g_hi) + _dot(tri, g_lo)
        b_last = bc[blk - 1:blk, :]
        state = state_ref[hd]

        o = _dot_nt((q * jnp.exp2(bc)).astype(BF16), state.astype(BF16))

        att = jnp.zeros((blk, blk), F32)
        q_b, k_b = q.astype(BF16), k.astype(BF16)
        for p in range(n_levels):
            e = jnp.exp2(_neg_abs(bc - _ref_rows(bc, 1 << p)).astype(BF16))
            att = jnp.where(pair_level == p, _dot_nt(q_b * e, k_b * e), att)
        att = jnp.where(pair_level == -1, _dot_nt(q_b, k_b), att)
        o = o + _dot(att.astype(BF16), v)

        kd = (k * jnp.exp2(b_last - bc)).astype(BF16)
        state_ref[hd] = state * jnp.exp2(b_last) + _dot_tn(v, kd)

        o = o * lax.rsqrt(jnp.mean(o * o, axis=-1, keepdims=True) + EPS) * gn_ref[hd]
        og = og_ref[rs, vc].astype(F32)
        o_ref[rs, vc] = (o * (og * jax.nn.sigmoid(og))).astype(o_ref.dtype)

    def step(i, carry):
        rs = pl.ds(pl.multiple_of(i * blk, blk), blk)
        glr = glr_ref[rs, 0:rank].astype(BF16)
        for hd in range(heads):
            head_step(hd, rs, glr)
        return carry

    lax.fori_loop(0, rows // blk, step, 0)


def gla_attention(z, w_gate, b_gate, gn_gain, batch, seq_len, dk, dv, rank):
    m = z.shape[0]
    heads = GLA_HEADS
    hp = GLA_HEADS_PER_STEP
    groups = heads // hp
    t = _pick_tile(seq_len, (1024, 512, 256))
    tiles = seq_len // t
    v0, og0 = 0, groups
    q0 = (2 * heads * dv) // (hp * dk)
    k0 = q0 + groups
    glr0 = (2 * heads * dv + 2 * heads * dk + MEM_HEADS * LANES) // LANES
    assert (2 * heads * dv) % (hp * dk) == 0
    rows = lambda b, h, s: b * tiles + s
    kern = functools.partial(_gla_kernel, rank=rank, heads=hp)
    return pl.pallas_call(
        kern,
        grid=(batch, groups, tiles),
        in_specs=[
            pl.BlockSpec((t, hp * dk), lambda b, h, s: (rows(b, h, s), q0 + h)),
            pl.BlockSpec((t, hp * dk), lambda b, h, s: (rows(b, h, s), k0 + h)),
            pl.BlockSpec((t, hp * dv), lambda b, h, s: (rows(b, h, s), v0 + h)),
            pl.BlockSpec((t, LANES), lambda b, h, s: (rows(b, h, s), glr0)),
            pl.BlockSpec((t, hp * dv), lambda b, h, s: (rows(b, h, s), og0 + h)),
            pl.BlockSpec((hp, rank, dk), lambda b, h, s: (h, 0, 0)),
            pl.BlockSpec((hp, 1, dk), lambda b, h, s: (h, 0, 0)),
            pl.BlockSpec((hp, 1, dv), lambda b, h, s: (h, 0, 0)),
        ],
        out_specs=pl.BlockSpec((t, hp * dv), lambda b, h, s: (rows(b, h, s), h)),
        out_shape=jax.ShapeDtypeStruct((m, heads * dv), BF16),
        scratch_shapes=[pltpu.VMEM((hp, dv, dk), F32)],
        compiler_params=_params("parallel", "parallel", "arbitrary"),
        name="gla_attention",
    )(z, z, z, z, z, w_gate, b_gate, gn_gain)


BIAS_PIECES = 3


def _split3(x):
    hi = x.astype(BF16)
    r1 = x - hi.astype(F32)
    mid = r1.astype(BF16)
    lo = (r1 - mid.astype(F32)).astype(BF16)
    return hi, mid, lo


def _fox_gate_kernel(x_ref, g_ref, w_ref, b_ref, route_ref, o_ref, carry_ref):
    s = pl.program_id(1)
    t = x_ref.shape[0]

    @pl.when(s == 0)
    def _():
        carry_ref[...] = jnp.zeros_like(carry_ref)

    h = _rms(x_ref[...], g_ref[...]).astype(BF16)
    log_f = _log_sigmoid(_dot(h, w_ref[...]) + b_ref[...])
    ti = lax.broadcasted_iota(jnp.int32, (t, t), 0)
    si = lax.broadcasted_iota(jnp.int32, (t, t), 1)
    tri = jnp.where(si <= ti, 1.0, 0.0).astype(BF16)
    f_hi, f_mid, f_lo = _split3(log_f)
    c = carry_ref[...] + _dot(tri, f_hi) + _dot(tri, f_mid) + _dot(tri, f_lo)
    carry_ref[...] = c[t - 1:t, :]
    pieces = jnp.concatenate(_split3(c * (-LOG2_E)), axis=1)
    o_ref[...] = _dot(pieces, route_ref[...]).astype(BF16)


def fox_gates(x, gain, w_f, b_f, batch, seq_len):
    m, d = x.shape
    t = _pick_tile(seq_len, (512, 256))
    tiles = seq_len // t
    src = lax.broadcasted_iota(jnp.int32, (BIAS_PIECES * LANES, FOX_HEADS * LANES), 0)
    dst = lax.broadcasted_iota(jnp.int32, (BIAS_PIECES * LANES, FOX_HEADS * LANES), 1)
    route = ((src // LANES == dst % LANES) & (src % LANES == dst // LANES)).astype(BF16)
    return pl.pallas_call(
        _fox_gate_kernel,
        grid=(batch, tiles),
        in_specs=[
            pl.BlockSpec((t, d), lambda b, s: (b * tiles + s, 0)),
            pl.BlockSpec((1, d), lambda b, s: (0, 0)),
            pl.BlockSpec((d, LANES), lambda b, s: (0, 0)),
            pl.BlockSpec((1, LANES), lambda b, s: (0, 0)),
            pl.BlockSpec(route.shape, lambda b, s: (0, 0)),
        ],
        out_specs=pl.BlockSpec((t, FOX_HEADS * LANES), lambda b, s: (b * tiles + s, 0)),
        out_shape=jax.ShapeDtypeStruct((m, FOX_HEADS * LANES), BF16),
        scratch_shapes=[pltpu.VMEM((1, LANES), F32)],
        compiler_params=_params("parallel", "arbitrary"),
        name="fox_gates",
    )(x, gain.reshape(1, d), w_f, b_f, route)


FOX_HEADS_PER_STEP = 2


def _fox_attn_kernel(q_ref, k_ref, kb_ref, v_ref, o_ref, m_ref, acc_ref, s0_ref, s1_ref, *, heads):
    qi = pl.program_id(2)
    tq = q_ref.shape[0]
    tk = tq // 2
    dh = q_ref.shape[1] // heads
    cols = [slice(hd * dh, (hd + 1) * dh) for hd in range(heads)]
    lane = lax.broadcasted_iota(jnp.int32, (tq, dh), 1)
    bias_taps = jnp.where(lane < BIAS_PIECES, 1.0, 0.0).astype(BF16)
    ones_col = jnp.where(lax.broadcasted_iota(jnp.int32, (tk, dh), 1) == 0, 1.0, 0.0).astype(BF16)
    qs = [jnp.concatenate([(q_ref[:, cols[hd]].astype(F32) * (dh ** -0.5 * LOG2_E)).astype(BF16), bias_taps],
                          axis=1) for hd in range(heads)]

    def key_rows(j):
        return pl.ds(pl.multiple_of(j * tk, tk), tk)

    def issue_scores(s_ref, j, rows):
        ks = key_rows(j)
        for hd in range(heads):
            keys = jnp.concatenate([k_ref[ks, cols[hd]], kb_ref[ks, cols[hd]]], axis=1)
            s_ref[hd, rows, :] = _dot_nt(qs[hd][rows], keys)

    def update(hd, s, j, rows):
        m_prev = m_ref[hd, rows, :]
        m_new = jnp.maximum(m_prev, jnp.max(s, axis=-1, keepdims=True))
        alpha = jnp.exp2(m_prev - m_new)
        p = jnp.concatenate([jnp.exp2((s[:, c:c + LANES] - m_new).astype(BF16))
                             for c in range(0, tk, LANES)], axis=1)
        pv = _dot(p, jnp.concatenate([v_ref[key_rows(j), cols[hd]], ones_col], axis=1))
        acc = acc_ref[hd, rows, :]
        acc_ref[hd, rows, :] = jnp.concatenate([alpha * acc[:, c:c + LANES] for c in range(0, 2 * dh, LANES)],
                                               axis=1) + pv
        m_ref[hd, rows, :] = m_new

    every = slice(0, tq)
    lower = slice(tk, tq)

    m_ref[...] = jnp.full(m_ref.shape, -jnp.inf, F32)
    acc_ref[...] = jnp.zeros_like(acc_ref)
    issue_scores(s0_ref, 0, every)

    def pair(jj, carry):
        j = 2 * jj
        issue_scores(s1_ref, j + 1, every)
        for hd in range(heads):
            update(hd, s0_ref[hd], j, every)
        issue_scores(s0_ref, j + 2, every)
        for hd in range(heads):
            update(hd, s1_ref[hd], j + 1, every)
        return carry

    lax.fori_loop(0, qi, pair, 0)

    issue_scores(s1_ref, 2 * qi + 1, lower)
    ti = lax.broadcasted_iota(jnp.int32, (tq, tk), 0)
    si = lax.broadcasted_iota(jnp.int32, (tq, tk), 1)
    for hd in range(heads):
        update(hd, jnp.where(si <= ti, s0_ref[hd], -jnp.inf), 2 * qi, every)
    tl = lax.broadcasted_iota(jnp.int32, (tk, tk), 0)
    sl = lax.broadcasted_iota(jnp.int32, (tk, tk), 1)
    for hd in range(heads):
        update(hd, jnp.where(sl <= tl, s1_ref[hd, lower, :], -jnp.inf), 2 * qi + 1, lower)
    for hd in range(heads):
        acc = acc_ref[hd]
        o_ref[:, cols[hd]] = (acc[:, 0:dh] / acc[:, dh:dh + 1]).astype(o_ref.dtype)


def fox_attention(zq, kv, kbias, batch, seq_len):
    m = zq.shape[0]
    hp = FOX_HEADS_PER_STEP
    width = hp * LANES
    groups = FOX_HEADS // hp
    tq = _pick_tile(seq_len, (1024, 512, 256))
    tiles = seq_len // tq
    kern = functools.partial(_fox_attn_kernel, heads=hp)
    return pl.pallas_call(
        kern,
        grid=(batch, groups, tiles),
        in_specs=[
            pl.BlockSpec((tq, width), lambda b, h, s: (b * tiles + s, h)),
            pl.BlockSpec((seq_len, width), lambda b, h, s: (b, h)),
            pl.BlockSpec((seq_len, width), lambda b, h, s: (b, h)),
            pl.BlockSpec((seq_len, width), lambda b, h, s: (b, groups + h)),
        ],
        out_specs=pl.BlockSpec((tq, width), lambda b, h, s: (b * tiles + s, h)),
        out_shape=jax.ShapeDtypeStruct((m, FOX_HEADS * LANES), BF16),
        scratch_shapes=[pltpu.VMEM((hp, tq, LANES), F32), pltpu.VMEM((hp, tq, 2 * LANES), F32),
                        pltpu.VMEM((hp, tq, tq // 2), F32), pltpu.VMEM((hp, tq, tq // 2), F32)],
        compiler_params=_params("parallel", "parallel", "arbitrary"),
        name="fox_attention",
    )(zq, kv, kbias, kv)


def kernel(x, mem, norm_mix, norm_ffn, norm_mem, norm_final, mem_w_kv, gla_w_in, gla_w_gate_up,
           gla_b_gate, gla_norm, gla_w_out, fox_kv_norm, fox_w_kv, fox_b_f, fox_w_in, fox_w_out,
           ffn_w_up, ffn_conv_w, ffn_conv_b, ffn_w_down):
    batch, seq_len, d = x.shape
    n_mem = mem.shape[1]
    depth = norm_mix.shape[0]
    n_gla = gla_w_in.shape[0]
    rank = gla_w_gate_up.shape[1]
    qk_width = gla_w_gate_up.shape[2]
    dk = qk_width // GLA_HEADS
    v_width = gla_norm.shape[1]
    dv = v_width // GLA_HEADS
    mem_width = mem_w_kv.shape[2] // 2
    fox_width = (fox_w_kv.shape[1] - FOX_HEADS) // 2
    assert mem_width == MEM_HEADS * LANES and fox_width == FOX_HEADS * LANES
    assert (2 * v_width) % dk == 0 and rank <= LANES

    xf = x.reshape(batch * seq_len, d)

    tr = _pick_tile(d, (512, 256, 128))
    w_mem = cast_blocks(mem_w_kv, (depth, d // tr), (None, tr, 2 * mem_width), lambda l, r: (l, r, 0),
                        (d, depth * 2 * mem_width), (tr, 2 * mem_width), lambda l, r: (r, l))
    ffn_up = stage_ffn_up(ffn_w_up)
    ffn_down = cast_rows(ffn_w_down)
    gla_out, fox_in, fox_out = cast_rows(gla_w_out), cast_rows(fox_w_in), cast_rows(fox_w_out)
    mem_kv = norm_matmul(mem.reshape(batch * n_mem, d), norm_mem, w_mem, BF16,
                         _pick_tile(w_mem.shape[1], (1024, 512)))

    fox_kv = fox_c = None
    for i in range(depth):
        if i == n_gla:
            w_kv = cast_blocks(fox_w_kv, (d // tr,), (tr, 2 * fox_width), lambda r: (r, 0),
                               (d, 2 * fox_width), (tr, 2 * fox_width), lambda r: (r, 0))
            fox_kv = norm_matmul(xf, fox_kv_norm, w_kv, BF16,
                                 _pick_tile(2 * fox_width, (1024, 768, 512)))
            w_f = jnp.pad(fox_w_kv[:, 2 * fox_width:], ((0, 0), (0, LANES - FOX_HEADS))).astype(BF16)
            b_f = jnp.pad(fox_b_f, (0, LANES - FOX_HEADS)).reshape(1, LANES)
            fox_c = fox_gates(xf, fox_kv_norm, w_f, b_f, batch, seq_len)
        if i < n_gla:
            w = gla_w_in[i]
            o_q, o_k, o_v = 0, qk_width, 2 * qk_width
            o_glr = o_v + v_width
            o_og = o_glr + rank
            o_mq = o_og + v_width
            w_in = jnp.concatenate([
                w[:, o_v:o_glr], w[:, o_og:o_mq], w[:, o_q:o_k], w[:, o_k:o_v], w[:, o_mq:],
                jnp.pad(w[:, o_glr:o_og], ((0, 0), (0, LANES - rank)))], axis=1).astype(BF16)
            z = norm_matmul(xf, norm_mix[i], w_in, BF16, 8 * 2 * LANES)
            a = gla_attention(z, gla_w_gate_up[i].reshape(rank, GLA_HEADS, dk).transpose(1, 0, 2).astype(BF16),
                              gla_b_gate[i].reshape(GLA_HEADS, 1, dk), gla_norm[i].reshape(GLA_HEADS, 1, dv),
                              batch, seq_len, dk, dv, rank)
            mq_block = (2 * v_width + 2 * qk_width) // mem_width
            w_out, w_layer = gla_out, i
        else:
            j = i - n_gla
            z = norm_matmul(xf, norm_mix[i], fox_in, BF16, _pick_tile(fox_w_in.shape[2], (1024, 512)), layer=j)
            a = fox_attention(z, fox_kv, fox_c, batch, seq_len)
            mq_block = fox_width // mem_width
            w_out, w_layer = fox_out, j
        xf = mixer_out(a, z, mq_block, mem_kv, i, w_out, w_layer, xf, seq_len, n_mem)
        xf = conv_ffn(xf, norm_ffn[i], ffn_up, ffn_conv_w[i], ffn_conv_b[i], ffn_down, i, norm_final,
                      seq_len, final_norm=(i == depth - 1))
    return xf.reshape(batch, seq_len, d)
```

```python
import functools
import math

import jax
import jax.numpy as jnp
from jax import lax
from jax.experimental import pallas as pl
from jax.experimental.pallas import tpu as pltpu

GLA_HEADS = 4
GLA_GATE_TAU = 16.0
FOX_HEADS = 12
MEM_HEADS = 4
CONV_WIDTH = 3
EPS = 1e-6
LOG2_E = math.log2(math.e)

LANES = 128
SUBLANES_F32 = 8
SUBLANES_BF16 = 16
VMEM_LIMIT = 56 * 1024 * 1024

BF16 = jnp.bfloat16
F32 = jnp.float32

NT_DIMS = (((1,), (1,)), ((), ()))
TN_DIMS = (((0,), (0,)), ((), ()))


def _params(*sem):
    return pltpu.CompilerParams(dimension_semantics=sem, vmem_limit_bytes=VMEM_LIMIT)


def _rms(x, gain):
    return x * lax.rsqrt(jnp.mean(x * x, axis=-1, keepdims=True) + EPS) * gain


def _dot(a, b):
    return jnp.dot(a, b, preferred_element_type=F32)


def _dot_nt(a, b):
    return lax.dot_general(a, b, NT_DIMS, preferred_element_type=F32)


def _dot_tn(a, b):
    return lax.dot_general(a, b, TN_DIMS, preferred_element_type=F32)


def _split2(x):
    hi = x.astype(BF16)
    lo = (x - hi.astype(F32)).astype(BF16)
    return hi, lo


def _neg_abs(x):
    bits = lax.bitcast_convert_type(x, jnp.uint32) | jnp.uint32(0x80000000)
    return lax.bitcast_convert_type(bits, F32)


def _log_sigmoid(x):
    return jnp.minimum(x, 0.0) - jnp.log1p(jnp.exp(-jnp.abs(x)))


def _pick_tile(n, preferred):
    for t in preferred:
        if n % t == 0:
            return t
    return n


def _cast_kernel(x_ref, o_ref):
    o_ref[...] = x_ref[...].astype(o_ref.dtype)


def cast_blocks(w, grid, in_block, in_map, out_shape, out_block, out_map):
    return pl.pallas_call(
        _cast_kernel,
        grid=grid,
        in_specs=[pl.BlockSpec(in_block, in_map)],
        out_specs=pl.BlockSpec(out_block, out_map),
        out_shape=jax.ShapeDtypeStruct(out_shape, BF16),
        compiler_params=_params(*(("parallel",) * len(grid))),
        name="cast_blocks",
    )(w)


def cast_rows(w):
    layers, rows, cols = w.shape
    tr = _pick_tile(rows, (512, 256, 128))
    return cast_blocks(w, (layers, rows // tr), (None, tr, cols), lambda l, r: (l, r, 0),
                       w.shape, (None, tr, cols), lambda l, r: (l, r, 0))


def _norm_matmul_kernel(x_ref, g_ref, w_ref, o_ref, *, tn):
    h = _rms(x_ref[...], g_ref[...]).astype(BF16)
    n = o_ref.shape[1]
    for c in range(0, n, tn):
        cols = slice(c, min(c + tn, n))
        o_ref[:, cols] = _dot(h, w_ref[:, cols]).astype(o_ref.dtype)


def norm_matmul(x, gain, w, out_dtype, tn, layer=None):
    m, d = x.shape
    n = w.shape[-1]
    if layer is None:
        w_spec = pl.BlockSpec((d, n), lambda i: (0, 0), pipeline_mode=pl.Buffered(1))
    else:
        w_spec = pl.BlockSpec((None, d, n), lambda i: (layer, 0, 0), pipeline_mode=pl.Buffered(1))
    out_bytes = jnp.dtype(out_dtype).itemsize

    def vmem_need(tm):
        return 2 * tm * d * 4 + d * n * 2 + 2 * tm * n * out_bytes + tm * d * 2 + tm * tn * 4

    tm = next(t for t in (1024, 512, 256, 128) if m % t == 0 and vmem_need(t) <= 0.9 * VMEM_LIMIT)
    return pl.pallas_call(
        functools.partial(_norm_matmul_kernel, tn=tn),
        grid=(m // tm,),
        in_specs=[
            pl.BlockSpec((tm, d), lambda i: (i, 0)),
            pl.BlockSpec((1, d), lambda i: (0, 0)),
            w_spec,
        ],
        out_specs=pl.BlockSpec((tm, n), lambda i: (i, 0)),
        out_shape=jax.ShapeDtypeStruct((m, n), out_dtype),
        compiler_params=_params("parallel"),
        name="norm_matmul",
    )(x, gain.reshape(1, d), w)


def _mixer_out_kernel(a_ref, q_ref, mk_ref, mv_ref, w_ref, x_ref, o_ref, *, heads):
    ka = a_ref.shape[1]
    main = _dot(a_ref[...], w_ref[0:ka, :])
    dh = q_ref.shape[1] // heads
    scale = dh ** -0.5
    cols = [slice(hd * dh, (hd + 1) * dh) for hd in range(heads)]
    scores = [_dot_nt(q_ref[:, c], mk_ref[:, c]) for c in cols]
    probs, inv_sums = [], []
    for s in scores:
        s = s * scale
        p = jnp.exp(s - jnp.max(s, axis=-1, keepdims=True))
        probs.append(p.astype(BF16))
        inv_sums.append(1.0 / jnp.sum(p, axis=-1, keepdims=True))
    mem = [(_dot(p, mv_ref[:, c]) * r).astype(BF16) for p, r, c in zip(probs, inv_sums, cols)]
    o_ref[...] = x_ref[...] + main + _dot(jnp.concatenate(mem, axis=1), w_ref[ka:, :])


def mixer_out(a, z, q_col_block, mem_kv, layer, w, w_layer, x, seq_len, n_mem):
    m, d = x.shape
    ka = a.shape[1]
    mem_width = MEM_HEADS * LANES
    tm = _pick_tile(seq_len, (512, 256))
    tiles = seq_len // tm
    kern = functools.partial(_mixer_out_kernel, heads=MEM_HEADS)
    return pl.pallas_call(
        kern,
        grid=(m // tm,),
        in_specs=[
            pl.BlockSpec((tm, ka), lambda i: (i, 0)),
            pl.BlockSpec((tm, mem_width), lambda i: (i, q_col_block)),
            pl.BlockSpec((n_mem, mem_width), lambda i: (i // tiles, 2 * layer)),
            pl.BlockSpec((n_mem, mem_width), lambda i: (i // tiles, 2 * layer + 1)),
            pl.BlockSpec((None, ka + mem_width, d), lambda i: (w_layer, 0, 0), pipeline_mode=pl.Buffered(1)),
            pl.BlockSpec((tm, d), lambda i: (i, 0)),
        ],
        out_specs=pl.BlockSpec((tm, d), lambda i: (i, 0)),
        out_shape=jax.ShapeDtypeStruct((m, d), F32),
        compiler_params=_params("parallel"),
        name="mixer_out",
    )(a, z, mem_kv, mem_kv, w, x)


HALO = SUBLANES_BF16
FFN_CHUNK_PARTS = (1, 1)


def _ffn_kernel(x_ref, xh_ref, g_ref, w_ref, taps_ref, wd_ref, gf_ref, o_ref, h_ref, *, tiles_per_seq,
                final_norm):
    i = pl.program_id(0)
    f = pl.program_id(1)
    tm = x_ref.shape[0]
    tf = wd_ref.shape[0]
    hidden = taps_ref.shape[1] // 2

    @pl.when(f == 0)
    def _():
        x = x_ref[...]
        prev = _rms(xh_ref[...], g_ref[...])
        prev = jnp.where(i % tiles_per_seq == 0, 0.0, prev)
        h_ref[0:HALO, :] = prev.astype(BF16)
        h_ref[HALO:HALO + tm, :] = _rms(x, g_ref[...]).astype(BF16)
        o_ref[...] = x

    def conv(u, first_col):
        taps = taps_ref[:, pl.ds(pl.multiple_of(first_col, LANES), tf)]
        y = taps[2:3, :] * u + taps[1:2, :] * pltpu.roll(u, 1, axis=0) + taps[0:1, :] * pltpu.roll(u, 2, axis=0)
        return y[HALO:, :] + taps[3:4, :]

    def up(lo, rc):
        h = h_ref[lo:lo + rc + HALO, :]
        return _dot(h, w_ref[0]), _dot(h, w_ref[1])

    sizes = [c * tm // sum(FFN_CHUNK_PARTS) for c in FFN_CHUNK_PARTS]
    starts = [sum(sizes[:n]) for n in range(len(sizes))]
    u_next = up(starts[0], sizes[0])
    for n, (lo, rc) in enumerate(zip(starts, sizes)):
        ua, uv = u_next
        if n + 1 < len(starts):
            u_next = up(starts[n + 1], sizes[n + 1])
        ya = conv(ua, f * tf)
        yv = conv(uv, hidden + f * tf)
        act = (ya * jax.nn.sigmoid(ya) * yv).astype(BF16)
        o_ref[lo:lo + rc, :] += _dot(act, wd_ref[...])

    if final_norm:
        @pl.when(f == pl.num_programs(1) - 1)
        def _():
            o_ref[...] = _rms(o_ref[...], gf_ref[...])


def ffn_col_tile(hidden):
    return _pick_tile(hidden, (512, 256, 128))


def stage_ffn_up(w_up):
    layers, d, two_hidden = w_up.shape
    tf = ffn_col_tile(two_hidden // 2)
    nf = two_hidden // 2 // tf
    return cast_blocks(w_up, (layers, nf, 2), (None, d, tf), lambda l, f, g: (l, 0, g * nf + f),
                       (layers, nf, 2, d, tf), (None, None, None, d, tf), lambda l, f, g: (l, f, g, 0, 0))


def conv_ffn(x, gain, w_tiles, conv_w, conv_b, w_down, layer, final_gain, seq_len, final_norm):
    m, d = x.shape
    hidden = w_down.shape[1]
    tm = _pick_tile(seq_len, (1024, 512, 256))
    _, nf, _, _, tf = w_tiles.shape
    halo_blocks = tm // HALO
    assert tm % (sum(FFN_CHUNK_PARTS) * SUBLANES_BF16) == 0
    taps = jnp.concatenate([conv_w, conv_b.reshape(1, -1)], axis=0)
    kern = functools.partial(_ffn_kernel, tiles_per_seq=seq_len // tm, final_norm=final_norm)
    return pl.pallas_call(
        kern,
        grid=(m // tm, nf),
        in_specs=[
            pl.BlockSpec((tm, d), lambda i, f: (i, 0)),
            pl.BlockSpec((HALO, d), lambda i, f: (jnp.maximum(i * halo_blocks - 1, 0), 0)),
            pl.BlockSpec((1, d), lambda i, f: (0, 0)),
            pl.BlockSpec((None, None, 2, d, tf), lambda i, f: (layer, f, 0, 0, 0)),
            pl.BlockSpec(taps.shape, lambda i, f: (0, 0)),
            pl.BlockSpec((None, tf, d), lambda i, f: (layer, f, 0)),
            pl.BlockSpec((1, d), lambda i, f: (0, 0)),
        ],
        out_specs=pl.BlockSpec((tm, d), lambda i, f: (i, 0)),
        out_shape=jax.ShapeDtypeStruct((m, d), F32),
        scratch_shapes=[pltpu.VMEM((tm + HALO, d), BF16)],
        compiler_params=_params("parallel", "arbitrary"),
        name="conv_ffn",
    )(x, x, gain.reshape(1, d), w_tiles, taps, w_down, final_gain.reshape(1, d))


GLA_BLOCK = 256


def _ref_rows(bc, half):
    rows, dk = bc.shape
    if half >= SUBLANES_F32:
        nb = rows // (2 * half)
        b3 = bc.reshape(nb, 2 * half, dk)
        return jnp.broadcast_to(b3[:, half:half + 1, :], b3.shape).reshape(rows, dk)
    b3 = bc.reshape(rows // SUBLANES_F32, SUBLANES_F32, dk)
    sub = lax.broadcasted_iota(jnp.int32, b3.shape, 1)
    out = None
    for mid in range(half, SUBLANES_F32, 2 * half):
        cand = jnp.broadcast_to(b3[:, mid:mid + 1, :], b3.shape)
        out = cand if out is None else jnp.where(sub >= mid - half, cand, out)
    return out.reshape(rows, dk)


GLA_HEADS_PER_STEP = 2


def _gla_kernel(q_ref, k_ref, v_ref, glr_ref, og_ref, wg_ref, bg_ref, gn_ref, o_ref, state_ref, *, rank, heads):
    t = pl.program_id(2)
    rows = q_ref.shape[0]
    dk = q_ref.shape[1] // heads
    dv = v_ref.shape[1] // heads
    blk = GLA_BLOCK if rows % GLA_BLOCK == 0 else rows
    n_levels = int(math.log2(blk))

    @pl.when(t == 0)
    def _():
        state_ref[...] = jnp.zeros_like(state_ref)

    ti = lax.broadcasted_iota(jnp.int32, (blk, blk), 0)
    si = lax.broadcasted_iota(jnp.int32, (blk, blk), 1)
    tri = jnp.where(si <= ti, 1.0, 0.0).astype(BF16)
    x = ti ^ si
    top_bit = 31 - lax.clz(jnp.maximum(x, 1))
    pair_level = jnp.where(si < ti, top_bit, jnp.where(si == ti, -1, -2))
    qscale = dk ** -0.5

    def head_step(hd, rs, glr):
        kc = slice(hd * dk, (hd + 1) * dk)
        vc = slice(hd * dv, (hd + 1) * dv)
        q = q_ref[rs, kc].astype(F32) * qscale
        k = k_ref[rs, kc].astype(F32)
        v = v_ref[rs, vc].astype(BF16)
        g = _log_sigmoid(_dot(glr, wg_ref[hd]) + bg_ref[hd]) * (LOG2_E / GLA_GATE_TAU)
        g_hi, g_lo = _split2(g)
        bc = _dot(tri, g_hi) + _dot(tri, g_lo)
        b_last = bc[blk - 1:blk, :]
        state = state_ref[hd]

        o = _dot_nt((q * jnp.exp2(bc)).astype(BF16), state.astype(BF16))

        att = jnp.zeros((blk, blk), F32)
        q_b, k_b = q.astype(BF16), k.astype(BF16)
        for p in range(n_levels):
            e = jnp.exp2(_neg_abs(bc - _ref_rows(bc, 1 << p)).astype(BF16))
            att = jnp.where(pair_level == p, _dot_nt(q_b * e, k_b * e), att)
        att = jnp.where(pair_level == -1, _dot_nt(q_b, k_b), att)
        o = o + _dot(att.astype(BF16), v)

        kd = (k * jnp.exp2(b_last - bc)).astype(BF16)
        state_ref[hd] = state * jnp.exp2(b_last) + _dot_tn(v, kd)

        o = o * lax.rsqrt(jnp.mean(o * o, axis=-1, keepdims=True) + EPS) * gn_ref[hd]
        og = og_ref[rs, vc].astype(F32)
        o_ref[rs, vc] = (o * (og * jax.nn.sigmoid(og))).astype(o_ref.dtype)

    def step(i, carry):
        rs = pl.ds(pl.multiple_of(i * blk, blk), blk)
        glr = glr_ref[rs, 0:rank].astype(BF16)
        for hd in range(heads):
            head_step(hd, rs, glr)
        return carry

    lax.fori_loop(0, rows // blk, step, 0)


def gla_attention(z, w_gate, b_gate, gn_gain, batch, seq_len, dk, dv, rank):
    m = z.shape[0]
    heads = GLA_HEADS
    hp = GLA_HEADS_PER_STEP
    groups = heads // hp
    t = _pick_tile(seq_len, (1024, 512, 256))
    tiles = seq_len // t
    v0, og0 = 0, groups
    q0 = (2 * heads * dv) // (hp * dk)
    k0 = q0 + groups
    glr0 = (2 * heads * dv + 2 * heads * dk + MEM_HEADS * LANES) // LANES
    assert (2 * heads * dv) % (hp * dk) == 0
    rows = lambda b, h, s: b * tiles + s
    kern = functools.partial(_gla_kernel, rank=rank, heads=hp)
    return pl.pallas_call(
        kern,
        grid=(batch, groups, tiles),
        in_specs=[
            pl.BlockSpec((t, hp * dk), lambda b, h, s: (rows(b, h, s), q0 + h)),
            pl.BlockSpec((t, hp * dk), lambda b, h, s: (rows(b, h, s), k0 + h)),
            pl.BlockSpec((t, hp * dv), lambda b, h, s: (rows(b, h, s), v0 + h)),
            pl.BlockSpec((t, LANES), lambda b, h, s: (rows(b, h, s), glr0)),
            pl.BlockSpec((t, hp * dv), lambda b, h, s: (rows(b, h, s), og0 + h)),
            pl.BlockSpec((hp, rank, dk), lambda b, h, s: (h, 0, 0)),
            pl.BlockSpec((hp, 1, dk), lambda b, h, s: (h, 0, 0)),
            pl.BlockSpec((hp, 1, dv), lambda b, h, s: (h, 0, 0)),
        ],
        out_specs=pl.BlockSpec((t, hp * dv), lambda b, h, s: (rows(b, h, s), h)),
        out_shape=jax.ShapeDtypeStruct((m, heads * dv), BF16),
        scratch_shapes=[pltpu.VMEM((hp, dv, dk), F32)],
        compiler_params=_params("parallel", "parallel", "arbitrary"),
        name="gla_attention",
    )(z, z, z, z, z, w_gate, b_gate, gn_gain)


BIAS_PIECES = 3


def _split3(x):
    hi = x.astype(BF16)
    r1 = x - hi.astype(F32)
    mid = r1.astype(BF16)
    lo = (r1 - mid.astype(F32)).astype(BF16)
    return hi, mid, lo


def _fox_gate_kernel(x_ref, g_ref, w_ref, b_ref, route_ref, o_ref, carry_ref):
    s = pl.program_id(1)
    t = x_ref.shape[0]

    @pl.when(s == 0)
    def _():
        carry_ref[...] = jnp.zeros_like(carry_ref)

    h = _rms(x_ref[...], g_ref[...]).astype(BF16)
    log_f = _log_sigmoid(_dot(h, w_ref[...]) + b_ref[...])
    ti = lax.broadcasted_iota(jnp.int32, (t, t), 0)
    si = lax.broadcasted_iota(jnp.int32, (t, t), 1)
    tri = jnp.where(si <= ti, 1.0, 0.0).astype(BF16)
    f_hi, f_mid, f_lo = _split3(log_f)
    c = carry_ref[...] + _dot(tri, f_hi) + _dot(tri, f_mid) + _dot(tri, f_lo)
    carry_ref[...] = c[t - 1:t, :]
    pieces = jnp.concatenate(_split3(c * (-LOG2_E)), axis=1)
    o_ref[...] = _dot(pieces, route_ref[...]).astype(BF16)


def fox_gates(x, gain, w_f, b_f, batch, seq_len):
    m, d = x.shape
    t = _pick_tile(seq_len, (512, 256))
    tiles = seq_len // t
    src = lax.broadcasted_iota(jnp.int32, (BIAS_PIECES * LANES, FOX_HEADS * LANES), 0)
    dst = lax.broadcasted_iota(jnp.int32, (BIAS_PIECES * LANES, FOX_HEADS * LANES), 1)
    route = ((src // LANES == dst % LANES) & (src % LANES == dst // LANES)).astype(BF16)
    return pl.pallas_call(
        _fox_gate_kernel,
        grid=(batch, tiles),
        in_specs=[
            pl.BlockSpec((t, d), lambda b, s: (b * tiles + s, 0)),
            pl.BlockSpec((1, d), lambda b, s: (0, 0)),
            pl.BlockSpec((d, LANES), lambda b, s: (0, 0)),
            pl.BlockSpec((1, LANES), lambda b, s: (0, 0)),
            pl.BlockSpec(route.shape, lambda b, s: (0, 0)),
        ],
        out_specs=pl.BlockSpec((t, FOX_HEADS * LANES), lambda b, s: (b * tiles + s, 0)),
        out_shape=jax.ShapeDtypeStruct((m, FOX_HEADS * LANES), BF16),
        scratch_shapes=[pltpu.VMEM((1, LANES), F32)],
        compiler_params=_params("parallel", "arbitrary"),
        name="fox_gates",
    )(x, gain.reshape(1, d), w_f, b_f, route)


FOX_HEADS_PER_STEP = 2


def _fox_attn_kernel(q_ref, k_ref, kb_ref, v_ref, o_ref, m_ref, acc_ref, s0_ref, s1_ref, *, heads):
    qi = pl.program_id(2)
    tq = q_ref.shape[0]
    tk = tq // 2
    dh = q_ref.shape[1] // heads
    cols = [slice(hd * dh, (hd + 1) * dh) for hd in range(heads)]
    lane = lax.broadcasted_iota(jnp.int32, (tq, dh), 1)
    bias_taps = jnp.where(lane < BIAS_PIECES, 1.0, 0.0).astype(BF16)
    ones_col = jnp.where(lax.broadcasted_iota(jnp.int32, (tk, dh), 1) == 0, 1.0, 0.0).astype(BF16)
    qs = [jnp.concatenate([(q_ref[:, cols[hd]].astype(F32) * (dh ** -0.5 * LOG2_E)).astype(BF16), bias_taps],
                          axis=1) for hd in range(heads)]

    def key_rows(j):
        return pl.ds(pl.multiple_of(j * tk, tk), tk)

    def issue_scores(s_ref, j, rows):
        ks = key_rows(j)
        for hd in range(heads):
            keys = jnp.concatenate([k_ref[ks, cols[hd]], kb_ref[ks, cols[hd]]], axis=1)
            s_ref[hd, rows, :] = _dot_nt(qs[hd][rows], keys)

    def update(hd, s, j, rows):
        m_prev = m_ref[hd, rows, :]
        m_new = jnp.maximum(m_prev, jnp.max(s, axis=-1, keepdims=True))
        alpha = jnp.exp2(m_prev - m_new)
        p = jnp.concatenate([jnp.exp2((s[:, c:c + LANES] - m_new).astype(BF16))
                             for c in range(0, tk, LANES)], axis=1)
        pv = _dot(p, jnp.concatenate([v_ref[key_rows(j), cols[hd]], ones_col], axis=1))
        acc = acc_ref[hd, rows, :]
        acc_ref[hd, rows, :] = jnp.concatenate([alpha * acc[:, c:c + LANES] for c in range(0, 2 * dh, LANES)],
                                               axis=1) + pv
        m_ref[hd, rows, :] = m_new

    every = slice(0, tq)
    lower = slice(tk, tq)

    m_ref[...] = jnp.full(m_ref.shape, -jnp.inf, F32)
    acc_ref[...] = jnp.zeros_like(acc_ref)
    issue_scores(s0_ref, 0, every)

    def pair(jj, carry):
        j = 2 * jj
        issue_scores(s1_ref, j + 1, every)
        for hd in range(heads):
            update(hd, s0_ref[hd], j, every)
        issue_scores(s0_ref, j + 2, every)
        for hd in range(heads):
            update(hd, s1_ref[hd], j + 1, every)
        return carry

    lax.fori_loop(0, qi, pair, 0)

    issue_scores(s1_ref, 2 * qi + 1, lower)
    ti = lax.broadcasted_iota(jnp.int32, (tq, tk), 0)
    si = lax.broadcasted_iota(jnp.int32, (tq, tk), 1)
    for hd in range(heads):
        update(hd, jnp.where(si <= ti, s0_ref[hd], -jnp.inf), 2 * qi, every)
    tl = lax.broadcasted_iota(jnp.int32, (tk, tk), 0)
    sl = lax.broadcasted_iota(jnp.int32, (tk, tk), 1)
    for hd in range(heads):
        update(hd, jnp.where(sl <= tl, s1_ref[hd, lower, :], -jnp.inf), 2 * qi + 1, lower)
    for hd in range(heads):
        acc = acc_ref[hd]
        o_ref[:, cols[hd]] = (acc[:, 0:dh] / acc[:, dh:dh + 1]).astype(o_ref.dtype)


def fox_attention(zq, kv, kbias, batch, seq_len):
    m = zq.shape[0]
    hp = FOX_HEADS_PER_STEP
    width = hp * LANES
    groups = FOX_HEADS // hp
    tq = _pick_tile(seq_len, (1024, 512, 256))
    tiles = seq_len // tq
    kern = functools.partial(_fox_attn_kernel, heads=hp)
    return pl.pallas_call(
        kern,
        grid=(batch, groups, tiles),
        in_specs=[
            pl.BlockSpec((tq, width), lambda b, h, s: (b * tiles + s, h)),
            pl.BlockSpec((seq_len, width), lambda b, h, s: (b, h)),
            pl.BlockSpec((seq_len, width), lambda b, h, s: (b, h)),
            pl.BlockSpec((seq_len, width), lambda b, h, s: (b, groups + h)),
        ],
        out_specs=pl.BlockSpec((tq, width), lambda b, h, s: (b * tiles + s, h)),
        out_shape=jax.ShapeDtypeStruct((m, FOX_HEADS * LANES), BF16),
        scratch_shapes=[pltpu.VMEM((hp, tq, LANES), F32), pltpu.VMEM((hp, tq, 2 * LANES), F32),
                        pltpu.VMEM((hp, tq, tq // 2), F32), pltpu.VMEM((hp, tq, tq // 2), F32)],
        compiler_params=_params("parallel", "parallel", "arbitrary"),
        name="fox_attention",
    )(zq, kv, kbias, kv)


def kernel(x, mem, norm_mix, norm_ffn, norm_mem, norm_final, mem_w_kv, gla_w_in, gla_w_gate_up,
           gla_b_gate, gla_norm, gla_w_out, fox_kv_norm, fox_w_kv, fox_b_f, fox_w_in, fox_w_out,
           ffn_w_up, ffn_conv_w, ffn_conv_b, ffn_w_down):
    batch, seq_len, d = x.shape
    n_mem = mem.shape[1]
    depth = norm_mix.shape[0]
    n_gla = gla_w_in.shape[0]
    rank = gla_w_gate_up.shape[1]
    qk_width = gla_w_gate_up.shape[2]
    dk = qk_width // GLA_HEADS
    v_width = gla_norm.shape[1]
    dv = v_width // GLA_HEADS
    mem_width = mem_w_kv.shape[2] // 2
    fox_width = (fox_w_kv.shape[1] - FOX_HEADS) // 2
    assert mem_width == MEM_HEADS * LANES and fox_width == FOX_HEADS * LANES
    assert (2 * v_width) % dk == 0 and rank <= LANES

    xf = x.reshape(batch * seq_len, d)

    tr = _pick_tile(d, (512, 256, 128))
    w_mem = cast_blocks(mem_w_kv, (depth, d // tr), (None, tr, 2 * mem_width), lambda l, r: (l, r, 0),
                        (d, depth * 2 * mem_width), (tr, 2 * mem_width), lambda l, r: (r, l))
    ffn_up = stage_ffn_up(ffn_w_up)
    ffn_down = cast_rows(ffn_w_down)
    gla_out, fox_in, fox_out = cast_rows(gla_w_out), cast_rows(fox_w_in), cast_rows(fox_w_out)
    mem_kv = norm_matmul(mem.reshape(batch * n_mem, d), norm_mem, w_mem, BF16,
                         _pick_tile(w_mem.shape[1], (1024, 512)))

    fox_kv = fox_c = None
    for i in range(depth):
        if i == n_gla:
            w_kv = cast_blocks(fox_w_kv, (d // tr,), (tr, 2 * fox_width), lambda r: (r, 0),
                               (d, 2 * fox_width), (tr, 2 * fox_width), lambda r: (r, 0))
            fox_kv = norm_matmul(xf, fox_kv_norm, w_kv, BF16,
                                 _pick_tile(2 * fox_width, (1024, 768, 512)))
            w_f = jnp.pad(fox_w_kv[:, 2 * fox_width:], ((0, 0), (0, LANES - FOX_HEADS))).astype(BF16)
            b_f = jnp.pad(fox_b_f, (0, LANES - FOX_HEADS)).reshape(1, LANES)
            fox_c = fox_gates(xf, fox_kv_norm, w_f, b_f, batch, seq_len)
        if i < n_gla:
            w = gla_w_in[i]
            o_q, o_k, o_v = 0, qk_width, 2 * qk_width
            o_glr = o_v + v_width
            o_og = o_glr + rank
            o_mq = o_og + v_width
            w_in = jnp.concatenate([
                w[:, o_v:o_glr], w[:, o_og:o_mq], w[:, o_q:o_k], w[:, o_k:o_v], w[:, o_mq:],
                jnp.pad(w[:, o_glr:o_og], ((0, 0), (0, LANES - rank)))], axis=1).astype(BF16)
            z = norm_matmul(xf, norm_mix[i], w_in, BF16, 8 * 2 * LANES)
            a = gla_attention(z, gla_w_gate_up[i].reshape(rank, GLA_HEADS, dk).transpose(1, 0, 2).astype(BF16),
                              gla_b_gate[i].reshape(GLA_HEADS, 1, dk), gla_norm[i].reshape(GLA_HEADS, 1, dv),
                              batch, seq_len, dk, dv, rank)
            mq_block = (2 * v_width + 2 * qk_width) // mem_width
            w_out, w_layer = gla_out, i
        else:
            j = i - n_gla
            z = norm_matmul(xf, norm_mix[i], fox_in, BF16, _pick_tile(fox_w_in.shape[2], (1024, 512)), layer=j)
            a = fox_attention(z, fox_kv, fox_c, batch, seq_len)
            mq_block = fox_width // mem_width
            w_out, w_layer = fox_out, j
        xf = mixer_out(a, z, mq_block, mem_kv, i, w_out, w_layer, xf, seq_len, n_mem)
        xf = conv_ffn(xf, norm_ffn[i], ffn_up, ffn_conv_w[i], ffn_conv_b[i], ffn_down, i, norm_final,
                      seq_len, final_norm=(i == depth - 1))
    return xf.reshape(batch, seq_len, d)
```

```python
import functools
import math

import jax
import jax.numpy as jnp
from jax import lax
from jax.experimental import pallas as pl
from jax.experimental.pallas import tpu as pltpu

GLA_HEADS = 4
GLA_GATE_TAU = 16.0
FOX_HEADS = 12
MEM_HEADS = 4
CONV_WIDTH = 3
EPS = 1e-6
LOG2_E = math.log2(math.e)

LANES = 128
SUBLANES_F32 = 8
SUBLANES_BF16 = 16
VMEM_LIMIT = 56 * 1024 * 1024

BF16 = jnp.bfloat16
F32 = jnp.float32

NT_DIMS = (((1,), (1,)), ((), ()))
TN_DIMS = (((0,), (0,)), ((), ()))


def _params(*sem):
    return pltpu.CompilerParams(dimension_semantics=sem, vmem_limit_bytes=VMEM_LIMIT)


def _rms(x, gain):
    return x * lax.rsqrt(jnp.mean(x * x, axis=-1, keepdims=True) + EPS) * gain


def _dot(a, b):
    return jnp.dot(a, b, preferred_element_type=F32)


def _dot_nt(a, b):
    return lax.dot_general(a, b, NT_DIMS, preferred_element_type=F32)


def _dot_tn(a, b):
    return lax.dot_general(a, b, TN_DIMS, preferred_element_type=F32)


def _split2(x):
    hi = x.astype(BF16)
    lo = (x - hi.astype(F32)).astype(BF16)
    return hi, lo


def _neg_abs(x):
    bits = lax.bitcast_convert_type(x, jnp.uint32) | jnp.uint32(0x80000000)
    return lax.bitcast_convert_type(bits, F32)


def _log_sigmoid(x):
    return jnp.minimum(x, 0.0) - jnp.log1p(jnp.exp(-jnp.abs(x)))


def _pick_tile(n, preferred):
    for t in preferred:
        if n % t == 0:
            return t
    return n


def _cast_kernel(x_ref, o_ref):
    o_ref[...] = x_ref[...].astype(o_ref.dtype)


def cast_blocks(w, grid, in_block, in_map, out_shape, out_block, out_map):
    return pl.pallas_call(
        _cast_kernel,
        grid=grid,
        in_specs=[pl.BlockSpec(in_block, in_map)],
        out_specs=pl.BlockSpec(out_block, out_map),
        out_shape=jax.ShapeDtypeStruct(out_shape, BF16),
        compiler_params=_params(*(("parallel",) * len(grid))),
        name="cast_blocks",
    )(w)


def cast_rows(w):
    layers, rows, cols = w.shape
    tr = _pick_tile(rows, (512, 256, 128))
    return cast_blocks(w, (layers, rows // tr), (None, tr, cols), lambda l, r: (l, r, 0),
                       w.shape, (None, tr, cols), lambda l, r: (l, r, 0))


def _norm_matmul_kernel(x_ref, g_ref, w_ref, o_ref, *, tn):
    h = _rms(x_ref[...], g_ref[...]).astype(BF16)
    n = o_ref.shape[1]
    for c in range(0, n, tn):
        cols = slice(c, min(c + tn, n))
        o_ref[:, cols] = _dot(h, w_ref[:, cols]).astype(o_ref.dtype)


def norm_matmul(x, gain, w, out_dtype, tn, layer=None):
    m, d = x.shape
    n = w.shape[-1]
    if layer is None:
        w_spec = pl.BlockSpec((d, n), lambda i: (0, 0), pipeline_mode=pl.Buffered(1))
    else:
        w_spec = pl.BlockSpec((None, d, n), lambda i: (layer, 0, 0), pipeline_mode=pl.Buffered(1))
    out_bytes = jnp.dtype(out_dtype).itemsize

    def vmem_need(tm):
        return 2 * tm * d * 4 + d * n * 2 + 2 * tm * n * out_bytes + tm * d * 2 + tm * tn * 4

    tm = next(t for t in (1024, 512, 256, 128) if m % t == 0 and vmem_need(t) <= 0.9 * VMEM_LIMIT)
    return pl.pallas_call(
        functools.partial(_norm_matmul_kernel, tn=tn),
        grid=(m // tm,),
        in_specs=[
            pl.BlockSpec((tm, d), lambda i: (i, 0)),
            pl.BlockSpec((1, d), lambda i: (0, 0)),
            w_spec,
        ],
        out_specs=pl.BlockSpec((tm, n), lambda i: (i, 0)),
        out_shape=jax.ShapeDtypeStruct((m, n), out_dtype),
        compiler_params=_params("parallel"),
        name="norm_matmul",
    )(x, gain.reshape(1, d), w)


def _mixer_out_kernel(a_ref, q_ref, mk_ref, mv_ref, w_ref, x_ref, o_ref, *, heads):
    ka = a_ref.shape[1]
    main = _dot(a_ref[...], w_ref[0:ka, :])
    dh = q_ref.shape[1] // heads
    scale = dh ** -0.5
    cols = [slice(hd * dh, (hd + 1) * dh) for hd in range(heads)]
    scores = [_dot_nt(q_ref[:, c], mk_ref[:, c]) for c in cols]
    probs, inv_sums = [], []
    for s in scores:
        s = s * scale
        p = jnp.exp(s - jnp.max(s, axis=-1, keepdims=True))
        probs.append(p.astype(BF16))
        inv_sums.append(1.0 / jnp.sum(p, axis=-1, keepdims=True))
    mem = [(_dot(p, mv_ref[:, c]) * r).astype(BF16) for p, r, c in zip(probs, inv_sums, cols)]
    o_ref[...] = x_ref[...] + main + _dot(jnp.concatenate(mem, axis=1), w_ref[ka:, :])


def mixer_out(a, z, q_col_block, mem_kv, layer, w, w_layer, x, seq_len, n_mem):
    m, d = x.shape
    ka = a.shape[1]
    mem_width = MEM_HEADS * LANES
    tm = _pick_tile(seq_len, (512, 256))
    tiles = seq_len // tm
    kern = functools.partial(_mixer_out_kernel, heads=MEM_HEADS)
    return pl.pallas_call(
        kern,
        grid=(m // tm,),
        in_specs=[
            pl.BlockSpec((tm, ka), lambda i: (i, 0)),
            pl.BlockSpec((tm, mem_width), lambda i: (i, q_col_block)),
            pl.BlockSpec((n_mem, mem_width), lambda i: (i // tiles, 2 * layer)),
            pl.BlockSpec((n_mem, mem_width), lambda i: (i // tiles, 2 * layer + 1)),
            pl.BlockSpec((None, ka + mem_width, d), lambda i: (w_layer, 0, 0), pipeline_mode=pl.Buffered(1)),
            pl.BlockSpec((tm, d), lambda i: (i, 0)),
        ],
        out_specs=pl.BlockSpec((tm, d), lambda i: (i, 0)),
        out_shape=jax.ShapeDtypeStruct((m, d), F32),
        compiler_params=_params("parallel"),
        name="mixer_out",
    )(a, z, mem_kv, mem_kv, w, x)


HALO = SUBLANES_BF16
FFN_CHUNK_PARTS = (1, 1)


def _ffn_kernel(x_ref, xh_ref, g_ref, w_ref, taps_ref, wd_ref, gf_ref, o_ref, h_ref, *, tiles_per_seq,
                final_norm):
    i = pl.program_id(0)
    f = pl.program_id(1)
    tm = x_ref.shape[0]
    tf = wd_ref.shape[0]
    hidden = taps_ref.shape[1] // 2

    @pl.when(f == 0)
    def _():
        x = x_ref[...]
        prev = _rms(xh_ref[...], g_ref[...])
        prev = jnp.where(i % tiles_per_seq == 0, 0.0, prev)
        h_ref[0:HALO, :] = prev.astype(BF16)
        h_ref[HALO:HALO + tm, :] = _rms(x, g_ref[...]).astype(BF16)
        o_ref[...] = x

    def conv(u, first_col):
        taps = taps_ref[:, pl.ds(pl.multiple_of(first_col, LANES), tf)]
        y = taps[2:3, :] * u + taps[1:2, :] * pltpu.roll(u, 1, axis=0) + taps[0:1, :] * pltpu.roll(u, 2, axis=0)
        return y[HALO:, :] + taps[3:4, :]

    def up(lo, rc):
        h = h_ref[lo:lo + rc + HALO, :]
        return _dot(h, w_ref[0]), _dot(h, w_ref[1])

    sizes = [c * tm // sum(FFN_CHUNK_PARTS) for c in FFN_CHUNK_PARTS]
    starts = [sum(sizes[:n]) for n in range(len(sizes))]
    u_next = up(starts[0], sizes[0])
    for n, (lo, rc) in enumerate(zip(starts, sizes)):
        ua, uv = u_next
        if n + 1 < len(starts):
            u_next = up(starts[n + 1], sizes[n + 1])
        ya = conv(ua, f * tf)
        yv = conv(uv, hidden + f * tf)
        act = (ya * jax.nn.sigmoid(ya) * yv).astype(BF16)
        o_ref[lo:lo + rc, :] += _dot(act, wd_ref[...])

    if final_norm:
        @pl.when(f == pl.num_programs(1) - 1)
        def _():
            o_ref[...] = _rms(o_ref[...], gf_ref[...])


def ffn_col_tile(hidden):
    return _pick_tile(hidden, (512, 256, 128))


def stage_ffn_up(w_up):
    layers, d, two_hidden = w_up.shape
    tf = ffn_col_tile(two_hidden // 2)
    nf = two_hidden // 2 // tf
    return cast_blocks(w_up, (layers, nf, 2), (None, d, tf), lambda l, f, g: (l, 0, g * nf + f),
                       (layers, nf, 2, d, tf), (None, None, None, d, tf), lambda l, f, g: (l, f, g, 0, 0))


def conv_ffn(x, gain, w_tiles, conv_w, conv_b, w_down, layer, final_gain, seq_len, final_norm):
    m, d = x.shape
    hidden = w_down.shape[1]
    tm = _pick_tile(seq_len, (1024, 512, 256))
    _, nf, _, _, tf = w_tiles.shape
    halo_blocks = tm // HALO
    assert tm % (sum(FFN_CHUNK_PARTS) * SUBLANES_BF16) == 0
    taps = jnp.concatenate([conv_w, conv_b.reshape(1, -1)], axis=0)
    kern = functools.partial(_ffn_kernel, tiles_per_seq=seq_len // tm, final_norm=final_norm)
    return pl.pallas_call(
        kern,
        grid=(m // tm, nf),
        in_specs=[
            pl.BlockSpec((tm, d), lambda i, f: (i, 0)),
            pl.BlockSpec((HALO, d), lambda i, f: (jnp.maximum(i * halo_blocks - 1, 0), 0)),
            pl.BlockSpec((1, d), lambda i, f: (0, 0)),
            pl.BlockSpec((None, None, 2, d, tf), lambda i, f: (layer, f, 0, 0, 0)),
            pl.BlockSpec(taps.shape, lambda i, f: (0, 0)),
            pl.BlockSpec((None, tf, d), lambda i, f: (layer, f, 0)),
            pl.BlockSpec((1, d), lambda i, f: (0, 0)),
        ],
        out_specs=pl.BlockSpec((tm, d), lambda i, f: (i, 0)),
        out_shape=jax.ShapeDtypeStruct((m, d), F32),
        scratch_shapes=[pltpu.VMEM((tm + HALO, d), BF16)],
        compiler_params=_params("parallel", "arbitrary"),
        name="conv_ffn",
    )(x, x, gain.reshape(1, d), w_tiles, taps, w_down, final_gain.reshape(1, d))


GLA_BLOCK = 256


def _ref_rows(bc, half):
    rows, dk = bc.shape
    if half >= SUBLANES_F32:
        nb = rows // (2 * half)
        b3 = bc.reshape(nb, 2 * half, dk)
        return jnp.broadcast_to(b3[:, half:half + 1, :], b3.shape).reshape(rows, dk)
    b3 = bc.reshape(rows // SUBLANES_F32, SUBLANES_F32, dk)
    sub = lax.broadcasted_iota(jnp.int32, b3.shape, 1)
    out = None
    for mid in range(half, SUBLANES_F32, 2 * half):
        cand = jnp.broadcast_to(b3[:, mid:mid + 1, :], b3.shape)
        out = cand if out is None else jnp.where(sub >= mid - half, cand, out)
    return out.reshape(rows, dk)


GLA_HEADS_PER_STEP = 4


def _gla_kernel(q_ref, k_ref, v_ref, glr_ref, og_ref, wg_ref, bg_ref, gn_ref, o_ref, state_ref, *, rank, heads):
    t = pl.program_id(2)
    rows = q_ref.shape[0]
    dk = q_ref.shape[1] // heads
    dv = v_ref.shape[1] // heads
    blk = GLA_BLOCK if rows % GLA_BLOCK == 0 else rows
    n_levels = int(math.log2(blk))

    @pl.when(t == 0)
    def _():
        state_ref[...] = jnp.zeros_like(state_ref)

    ti = lax.broadcasted_iota(jnp.int32, (blk, blk), 0)
    si = lax.broadcasted_iota(jnp.int32, (blk, blk), 1)
    tri = jnp.where(si <= ti, 1.0, 0.0).astype(BF16)
    x = ti ^ si
    top_bit = 31 - lax.clz(jnp.maximum(x, 1))
    pair_level = jnp.where(si < ti, top_bit, jnp.where(si == ti, -1, -2))
    qscale = dk ** -0.5

    def step(i, carry):
        rs = pl.ds(pl.multiple_of(i * blk, blk), blk)
        glr = glr_ref[rs, 0:rank].astype(BF16)
        hs = range(heads)
        kc = [slice(hd * dk, (hd + 1) * dk) for hd in hs]
        vc = [slice(hd * dv, (hd + 1) * dv) for hd in hs]
        q = [q_ref[rs, kc[hd]].astype(F32) * qscale for hd in hs]
        k = [k_ref[rs, kc[hd]].astype(F32) for hd in hs]
        v = [v_ref[rs, vc[hd]].astype(BF16) for hd in hs]
        g = [_log_sigmoid(_dot(glr, wg_ref[hd]) + bg_ref[hd]) * (LOG2_E / GLA_GATE_TAU) for hd in hs]
        gs = [_split2(g[hd]) for hd in hs]
        bc = [_dot(tri, gs[hd][0]) + _dot(tri, gs[hd][1]) for hd in hs]
        b_last = [bc[hd][blk - 1:blk, :] for hd in hs]
        state = [state_ref[hd] for hd in hs]

        o = [_dot_nt((q[hd] * jnp.exp2(bc[hd])).astype(BF16), state[hd].astype(BF16)) for hd in hs]

        att = [jnp.zeros((blk, blk), F32) for hd in hs]
        q_b = [q[hd].astype(BF16) for hd in hs]
        k_b = [k[hd].astype(BF16) for hd in hs]
        for p in range(n_levels):
            e = [jnp.exp2(_neg_abs(bc[hd] - _ref_rows(bc[hd], 1 << p)).astype(BF16)) for hd in hs]
            sp = [_dot_nt(q_b[hd] * e[hd], k_b[hd] * e[hd]) for hd in hs]
            att = [jnp.where(pair_level == p, sp[hd], att[hd]) for hd in hs]
        sd = [_dot_nt(q_b[hd], k_b[hd]) for hd in hs]
        att = [jnp.where(pair_level == -1, sd[hd], att[hd]) for hd in hs]
        o = [o[hd] + _dot(att[hd].astype(BF16), v[hd]) for hd in hs]

        kd = [(k[hd] * jnp.exp2(b_last[hd] - bc[hd])).astype(BF16) for hd in hs]
        upd = [_dot_tn(v[hd], kd[hd]) for hd in hs]
        for hd in hs:
            state_ref[hd] = state[hd] * jnp.exp2(b_last[hd]) + upd[hd]

        for hd in hs:
            on = o[hd] * lax.rsqrt(jnp.mean(o[hd] * o[hd], axis=-1, keepdims=True) + EPS) * gn_ref[hd]
            og = og_ref[rs, vc[hd]].astype(F32)
            o_ref[rs, vc[hd]] = (on * (og * jax.nn.sigmoid(og))).astype(o_ref.dtype)
        return carry

    lax.fori_loop(0, rows // blk, step, 0)


def gla_attention(z, w_gate, b_gate, gn_gain, batch, seq_len, dk, dv, rank):
    m = z.shape[0]
    heads = GLA_HEADS
    hp = GLA_HEADS_PER_STEP
    groups = heads // hp
    t = _pick_tile(seq_len, (1024, 512, 256))
    tiles = seq_len // t
    v0, og0 = 0, groups
    q0 = (2 * heads * dv) // (hp * dk)
    k0 = q0 + groups
    glr0 = (2 * heads * dv + 2 * heads * dk + MEM_HEADS * LANES) // LANES
    assert (2 * heads * dv) % (hp * dk) == 0
    rows = lambda b, h, s: b * tiles + s
    kern = functools.partial(_gla_kernel, rank=rank, heads=hp)
    return pl.pallas_call(
        kern,
        grid=(batch, groups, tiles),
        in_specs=[
            pl.BlockSpec((t, hp * dk), lambda b, h, s: (rows(b, h, s), q0 + h)),
            pl.BlockSpec((t, hp * dk), lambda b, h, s: (rows(b, h, s), k0 + h)),
            pl.BlockSpec((t, hp * dv), lambda b, h, s: (rows(b, h, s), v0 + h)),
            pl.BlockSpec((t, LANES), lambda b, h, s: (rows(b, h, s), glr0)),
            pl.BlockSpec((t, hp * dv), lambda b, h, s: (rows(b, h, s), og0 + h)),
            pl.BlockSpec((hp, rank, dk), lambda b, h, s: (h, 0, 0)),
            pl.BlockSpec((hp, 1, dk), lambda b, h, s: (h, 0, 0)),
            pl.BlockSpec((hp, 1, dv), lambda b, h, s: (h, 0, 0)),
        ],
        out_specs=pl.BlockSpec((t, hp * dv), lambda b, h, s: (rows(b, h, s), h)),
        out_shape=jax.ShapeDtypeStruct((m, heads * dv), BF16),
        scratch_shapes=[pltpu.VMEM((hp, dv, dk), F32)],
        compiler_params=_params("parallel", "parallel", "arbitrary"),
        name="gla_attention",
    )(z, z, z, z, z, w_gate, b_gate, gn_gain)


BIAS_PIECES = 3


def _split3(x):
    hi = x.astype(BF16)
    r1 = x - hi.astype(F32)
    mid = r1.astype(BF16)
    lo = (r1 - mid.astype(F32)).astype(BF16)
    return hi, mid, lo


def _fox_gate_kernel(x_ref, g_ref, w_ref, b_ref, route_ref, o_ref, carry_ref):
    s = pl.program_id(1)
    t = x_ref.shape[0]

    @pl.when(s == 0)
    def _():
        carry_ref[...] = jnp.zeros_like(carry_ref)

    h = _rms(x_ref[...], g_ref[...]).astype(BF16)
    log_f = _log_sigmoid(_dot(h, w_ref[...]) + b_ref[...])
    ti = lax.broadcasted_iota(jnp.int32, (t, t), 0)
    si = lax.broadcasted_iota(jnp.int32, (t, t), 1)
    tri = jnp.where(si <= ti, 1.0, 0.0).astype(BF16)
    f_hi, f_mid, f_lo = _split3(log_f)
    c = carry_ref[...] + _dot(tri, f_hi) + _dot(tri, f_mid) + _dot(tri, f_lo)
    carry_ref[...] = c[t - 1:t, :]
    pieces = jnp.concatenate(_split3(c * (-LOG2_E)), axis=1)
    o_ref[...] = _dot(pieces, route_ref[...]).astype(BF16)


def fox_gates(x, gain, w_f, b_f, batch, seq_len):
    m, d = x.shape
    t = _pick_tile(seq_len, (512, 256))
    tiles = seq_len // t
    src = lax.broadcasted_iota(jnp.int32, (BIAS_PIECES * LANES, FOX_HEADS * LANES), 0)
    dst = lax.broadcasted_iota(jnp.int32, (BIAS_PIECES * LANES, FOX_HEADS * LANES), 1)
    route = ((src // LANES == dst % LANES) & (src % LANES == dst // LANES)).astype(BF16)
    return pl.pallas_call(
        _fox_gate_kernel,
        grid=(batch, tiles),
        in_specs=[
            pl.BlockSpec((t, d), lambda b, s: (b * tiles + s, 0)),
            pl.BlockSpec((1, d), lambda b, s: (0, 0)),
            pl.BlockSpec((d, LANES), lambda b, s: (0, 0)),
            pl.BlockSpec((1, LANES), lambda b, s: (0, 0)),
            pl.BlockSpec(route.shape, lambda b, s: (0, 0)),
        ],
        out_specs=pl.BlockSpec((t, FOX_HEADS * LANES), lambda b, s: (b * tiles + s, 0)),
        out_shape=jax.ShapeDtypeStruct((m, FOX_HEADS * LANES), BF16),
        scratch_shapes=[pltpu.VMEM((1, LANES), F32)],
        compiler_params=_params("parallel", "arbitrary"),
        name="fox_gates",
    )(x, gain.reshape(1, d), w_f, b_f, route)


FOX_HEADS_PER_STEP = 2


def _fox_attn_kernel(q_ref, k_ref, kb_ref, v_ref, o_ref, m_ref, acc_ref, s0_ref, s1_ref, *, heads):
    qi = pl.program_id(2)
    tq = q_ref.shape[0]
    tk = tq // 2
    dh = q_ref.shape[1] // heads
    cols = [slice(hd * dh, (hd + 1) * dh) for hd in range(heads)]
    lane = lax.broadcasted_iota(jnp.int32, (tq, dh), 1)
    bias_taps = jnp.where(lane < BIAS_PIECES, 1.0, 0.0).astype(BF16)
    ones_col = jnp.where(lax.broadcasted_iota(jnp.int32, (tk, dh), 1) == 0, 1.0, 0.0).astype(BF16)
    qs = [jnp.concatenate([(q_ref[:, cols[hd]].astype(F32) * (dh ** -0.5 * LOG2_E)).astype(BF16), bias_taps],
                          axis=1) for hd in range(heads)]

    def key_rows(j):
        return pl.ds(pl.multiple_of(j * tk, tk), tk)

    def issue_scores(s_ref, j, rows):
        ks = key_rows(j)
        for hd in range(heads):
            keys = jnp.concatenate([k_ref[ks, cols[hd]], kb_ref[ks, cols[hd]]], axis=1)
            s_ref[hd, rows, :] = _dot_nt(qs[hd][rows], keys)

    def update(hd, s, j, rows):
        m_prev = m_ref[hd, rows, :]
        m_new = jnp.maximum(m_prev, jnp.max(s, axis=-1, keepdims=True))
        alpha = jnp.exp2(m_prev - m_new)
        p = jnp.concatenate([jnp.exp2((s[:, c:c + LANES] - m_new).astype(BF16))
                             for c in range(0, tk, LANES)], axis=1)
        pv = _dot(p, jnp.concatenate([v_ref[key_rows(j), cols[hd]], ones_col], axis=1))
        acc = acc_ref[hd, rows, :]
        acc_ref[hd, rows, :] = jnp.concatenate([alpha * acc[:, c:c + LANES] for c in range(0, 2 * dh, LANES)],
                                               axis=1) + pv
        m_ref[hd, rows, :] = m_new

    every = slice(0, tq)
    lower = slice(tk, tq)

    m_ref[...] = jnp.full(m_ref.shape, -jnp.inf, F32)
    acc_ref[...] = jnp.zeros_like(acc_ref)
    issue_scores(s0_ref, 0, every)

    def pair(jj, carry):
        j = 2 * jj
        issue_scores(s1_ref, j + 1, every)
        for hd in range(heads):
            update(hd, s0_ref[hd], j, every)
        issue_scores(s0_ref, j + 2, every)
        for hd in range(heads):
            update(hd, s1_ref[hd], j + 1, every)
        return carry

    lax.fori_loop(0, qi, pair, 0)

    issue_scores(s1_ref, 2 * qi + 1, lower)
    ti = lax.broadcasted_iota(jnp.int32, (tq, tk), 0)
    si = lax.broadcasted_iota(jnp.int32, (tq, tk), 1)
    for hd in range(heads):
        update(hd, jnp.where(si <= ti, s0_ref[hd], -jnp.inf), 2 * qi, every)
    tl = lax.broadcasted_iota(jnp.int32, (tk, tk), 0)
    sl = lax.broadcasted_iota(jnp.int32, (tk, tk), 1)
    for hd in range(heads):
        update(hd, jnp.where(sl <= tl, s1_ref[hd, lower, :], -jnp.inf), 2 * qi + 1, lower)
    for hd in range(heads):
        acc = acc_ref[hd]
        o_ref[:, cols[hd]] = (acc[:, 0:dh] / acc[:, dh:dh + 1]).astype(o_ref.dtype)


def fox_attention(zq, kv, kbias, batch, seq_len):
    m = zq.shape[0]
    hp = FOX_HEADS_PER_STEP
    width = hp * LANES
    groups = FOX_HEADS // hp
    tq = _pick_tile(seq_len, (1024, 512, 256))
    tiles = seq_len // tq
    kern = functools.partial(_fox_attn_kernel, heads=hp)
    return pl.pallas_call(
        kern,
        grid=(batch, groups, tiles),
        in_specs=[
            pl.BlockSpec((tq, width), lambda b, h, s: (b * tiles + s, h)),
            pl.BlockSpec((seq_len, width), lambda b, h, s: (b, h)),
            pl.BlockSpec((seq_len, width), lambda b, h, s: (b, h)),
            pl.BlockSpec((seq_len, width), lambda b, h, s: (b, groups + h)),
        ],
        out_specs=pl.BlockSpec((tq, width), lambda b, h, s: (b * tiles + s, h)),
        out_shape=jax.ShapeDtypeStruct((m, FOX_HEADS * LANES), BF16),
        scratch_shapes=[pltpu.VMEM((hp, tq, LANES), F32), pltpu.VMEM((hp, tq, 2 * LANES), F32),
                        pltpu.VMEM((hp, tq, tq // 2), F32), pltpu.VMEM((hp, tq, tq // 2), F32)],
        compiler_params=_params("parallel", "parallel", "arbitrary"),
        name="fox_attention",
    )(zq, kv, kbias, kv)


def kernel(x, mem, norm_mix, norm_ffn, norm_mem, norm_final, mem_w_kv, gla_w_in, gla_w_gate_up,
           gla_b_gate, gla_norm, gla_w_out, fox_kv_norm, fox_w_kv, fox_b_f, fox_w_in, fox_w_out,
           ffn_w_up, ffn_conv_w, ffn_conv_b, ffn_w_down):
    batch, seq_len, d = x.shape
    n_mem = mem.shape[1]
    depth = norm_mix.shape[0]
    n_gla = gla_w_in.shape[0]
    rank = gla_w_gate_up.shape[1]
    qk_width = gla_w_gate_up.shape[2]
    dk = qk_width // GLA_HEADS
    v_width = gla_norm.shape[1]
    dv = v_width // GLA_HEADS
    mem_width = mem_w_kv.shape[2] // 2
    fox_width = (fox_w_kv.shape[1] - FOX_HEADS) // 2
    assert mem_width == MEM_HEADS * LANES and fox_width == FOX_HEADS * LANES
    assert (2 * v_width) % dk == 0 and rank <= LANES

    xf = x.reshape(batch * seq_len, d)

    tr = _pick_tile(d, (512, 256, 128))
    w_mem = cast_blocks(mem_w_kv, (depth, d // tr), (None, tr, 2 * mem_width), lambda l, r: (l, r, 0),
                        (d, depth * 2 * mem_width), (tr, 2 * mem_width), lambda l, r: (r, l))
    ffn_up = stage_ffn_up(ffn_w_up)
    ffn_down = cast_rows(ffn_w_down)
    gla_out, fox_in, fox_out = cast_rows(gla_w_out), cast_rows(fox_w_in), cast_rows(fox_w_out)
    mem_kv = norm_matmul(mem.reshape(batch * n_mem, d), norm_mem, w_mem, BF16,
                         _pick_tile(w_mem.shape[1], (1024, 512)))

    fox_kv = fox_c = None
    for i in range(depth):
        if i == n_gla:
            w_kv = cast_blocks(fox_w_kv, (d // tr,), (tr, 2 * fox_width), lambda r: (r, 0),
                               (d, 2 * fox_width), (tr, 2 * fox_width), lambda r: (r, 0))
            fox_kv = norm_matmul(xf, fox_kv_norm, w_kv, BF16,
                                 _pick_tile(2 * fox_width, (1024, 768, 512)))
            w_f = jnp.pad(fox_w_kv[:, 2 * fox_width:], ((0, 0), (0, LANES - FOX_HEADS))).astype(BF16)
            b_f = jnp.pad(fox_b_f, (0, LANES - FOX_HEADS)).reshape(1, LANES)
            fox_c = fox_gates(xf, fox_kv_norm, w_f, b_f, batch, seq_len)
        if i < n_gla:
            w = gla_w_in[i]
            o_q, o_k, o_v = 0, qk_width, 2 * qk_width
            o_glr = o_v + v_width
            o_og = o_glr + rank
            o_mq = o_og + v_width
            w_in = jnp.concatenate([
                w[:, o_v:o_glr], w[:, o_og:o_mq], w[:, o_q:o_k], w[:, o_k:o_v], w[:, o_mq:],
                jnp.pad(w[:, o_glr:o_og], ((0, 0), (0, LANES - rank)))], axis=1).astype(BF16)
            z = norm_matmul(xf, norm_mix[i], w_in, BF16, 8 * 2 * LANES)
            a = gla_attention(z, gla_w_gate_up[i].reshape(rank, GLA_HEADS, dk).transpose(1, 0, 2).astype(BF16),
                              gla_b_gate[i].reshape(GLA_HEADS, 1, dk), gla_norm[i].reshape(GLA_HEADS, 1, dv),
                              batch, seq_len, dk, dv, rank)
            mq_block = (2 * v_width + 2 * qk_width) // mem_width
            w_out, w_layer = gla_out, i
        else:
            j = i - n_gla
            z = norm_matmul(xf, norm_mix[i], fox_in, BF16, _pick_tile(fox_w_in.shape[2], (1024, 512)), layer=j)
            a = fox_attention(z, fox_kv, fox_c, batch, seq_len)
            mq_block = fox_width // mem_width
            w_out, w_layer = fox_out, j
        xf = mixer_out(a, z, mq_block, mem_kv, i, w_out, w_layer, xf, seq_len, n_mem)
        xf = conv_ffn(xf, norm_ffn[i], ffn_up, ffn_conv_w[i], ffn_conv_b[i], ffn_down, i, norm_final,
                      seq_len, final_norm=(i == depth - 1))
    return xf.reshape(batch, seq_len, d)
```

```python
import functools
import math

import jax
import jax.numpy as jnp
from jax import lax
from jax.experimental import pallas as pl
from jax.experimental.pallas import tpu as pltpu

GLA_HEADS = 4
GLA_GATE_TAU = 16.0
FOX_HEADS = 12
MEM_HEADS = 4
CONV_WIDTH = 3
EPS = 1e-6
LOG2_E = math.log2(math.e)

LANES = 128
SUBLANES_F32 = 8
SUBLANES_BF16 = 16
VMEM_LIMIT = 56 * 1024 * 1024

BF16 = jnp.bfloat16
F32 = jnp.float32

NT_DIMS = (((1,), (1,)), ((), ()))
TN_DIMS = (((0,), (0,)), ((), ()))


def _params(*sem):
    return pltpu.CompilerParams(dimension_semantics=sem, vmem_limit_bytes=VMEM_LIMIT)


def _rms(x, gain):
    return x * lax.rsqrt(jnp.mean(x * x, axis=-1, keepdims=True) + EPS) * gain


def _dot(a, b):
    return jnp.dot(a, b, preferred_element_type=F32)


def _dot_nt(a, b):
    return lax.dot_general(a, b, NT_DIMS, preferred_element_type=F32)


def _dot_tn(a, b):
    return lax.dot_general(a, b, TN_DIMS, preferred_element_type=F32)


def _split2(x):
    hi = x.astype(BF16)
    lo = (x - hi.astype(F32)).astype(BF16)
    return hi, lo


def _neg_abs(x):
    bits = lax.bitcast_convert_type(x, jnp.uint32) | jnp.uint32(0x80000000)
    return lax.bitcast_convert_type(bits, F32)


def _log_sigmoid(x):
    return jnp.minimum(x, 0.0) - jnp.log1p(jnp.exp(-jnp.abs(x)))


def _pick_tile(n, preferred):
    for t in preferred:
        if n % t == 0:
            return t
    return n


def _cast_kernel(x_ref, o_ref):
    o_ref[...] = x_ref[...].astype(o_ref.dtype)


def cast_blocks(w, grid, in_block, in_map, out_shape, out_block, out_map):
    return pl.pallas_call(
        _cast_kernel,
        grid=grid,
        in_specs=[pl.BlockSpec(in_block, in_map)],
        out_specs=pl.BlockSpec(out_block, out_map),
        out_shape=jax.ShapeDtypeStruct(out_shape, BF16),
        compiler_params=_params(*(("parallel",) * len(grid))),
        name="cast_blocks",
    )(w)


def cast_rows(w):
    layers, rows, cols = w.shape
    tr = _pick_tile(rows, (512, 256, 128))
    return cast_blocks(w, (layers, rows // tr), (None, tr, cols), lambda l, r: (l, r, 0),
                       w.shape, (None, tr, cols), lambda l, r: (l, r, 0))


def _norm_matmul_kernel(x_ref, g_ref, w_ref, o_ref, *, tn):
    h = _rms(x_ref[...], g_ref[...]).astype(BF16)
    n = o_ref.shape[1]
    for c in range(0, n, tn):
        cols = slice(c, min(c + tn, n))
        o_ref[:, cols] = _dot(h, w_ref[:, cols]).astype(o_ref.dtype)


def norm_matmul(x, gain, w, out_dtype, tn, layer=None):
    m, d = x.shape
    n = w.shape[-1]
    if layer is None:
        w_spec = pl.BlockSpec((d, n), lambda i: (0, 0), pipeline_mode=pl.Buffered(1))
    else:
        w_spec = pl.BlockSpec((None, d, n), lambda i: (layer, 0, 0), pipeline_mode=pl.Buffered(1))
    out_bytes = jnp.dtype(out_dtype).itemsize

    def vmem_need(tm):
        return 2 * tm * d * 4 + d * n * 2 + 2 * tm * n * out_bytes + tm * d * 2 + tm * tn * 4

    tm = next(t for t in (1024, 512, 256, 128) if m % t == 0 and vmem_need(t) <= 0.9 * VMEM_LIMIT)
    return pl.pallas_call(
        functools.partial(_norm_matmul_kernel, tn=tn),
        grid=(m // tm,),
        in_specs=[
            pl.BlockSpec((tm, d), lambda i: (i, 0)),
            pl.BlockSpec((1, d), lambda i: (0, 0)),
            w_spec,
        ],
        out_specs=pl.BlockSpec((tm, n), lambda i: (i, 0)),
        out_shape=jax.ShapeDtypeStruct((m, n), out_dtype),
        compiler_params=_params("parallel"),
        name="norm_matmul",
    )(x, gain.reshape(1, d), w)


def _mixer_out_kernel(a_ref, q_ref, mk_ref, mv_ref, w_ref, x_ref, o_ref, *, heads):
    ka = a_ref.shape[1]
    main = _dot(a_ref[...], w_ref[0:ka, :])
    dh = q_ref.shape[1] // heads
    scale = dh ** -0.5
    cols = [slice(hd * dh, (hd + 1) * dh) for hd in range(heads)]
    scores = [_dot_nt(q_ref[:, c], mk_ref[:, c]) for c in cols]
    probs, inv_sums = [], []
    for s in scores:
        s = s * scale
        p = jnp.exp(s - jnp.max(s, axis=-1, keepdims=True))
        probs.append(p.astype(BF16))
        inv_sums.append(1.0 / jnp.sum(p, axis=-1, keepdims=True))
    mem = [(_dot(p, mv_ref[:, c]) * r).astype(BF16) for p, r, c in zip(probs, inv_sums, cols)]
    o_ref[...] = x_ref[...] + main + _dot(jnp.concatenate(mem, axis=1), w_ref[ka:, :])


def mixer_out(a, z, q_col_block, mem_kv, layer, w, w_layer, x, seq_len, n_mem):
    m, d = x.shape
    ka = a.shape[1]
    mem_width = MEM_HEADS * LANES
    tm = _pick_tile(seq_len, (512, 256))
    tiles = seq_len // tm
    kern = functools.partial(_mixer_out_kernel, heads=MEM_HEADS)
    return pl.pallas_call(
        kern,
        grid=(m // tm,),
        in_specs=[
            pl.BlockSpec((tm, ka), lambda i: (i, 0)),
            pl.BlockSpec((tm, mem_width), lambda i: (i, q_col_block)),
            pl.BlockSpec((n_mem, mem_width), lambda i: (i // tiles, 2 * layer)),
            pl.BlockSpec((n_mem, mem_width), lambda i: (i // tiles, 2 * layer + 1)),
            pl.BlockSpec((None, ka + mem_width, d), lambda i: (w_layer, 0, 0), pipeline_mode=pl.Buffered(1)),
            pl.BlockSpec((tm, d), lambda i: (i, 0)),
        ],
        out_specs=pl.BlockSpec((tm, d), lambda i: (i, 0)),
        out_shape=jax.ShapeDtypeStruct((m, d), F32),
        compiler_params=_params("parallel"),
        name="mixer_out",
    )(a, z, mem_kv, mem_kv, w, x)


HALO = SUBLANES_BF16
FFN_CHUNK_PARTS = (5, 3)


def _ffn_kernel(x_ref, xh_ref, g_ref, w_ref, taps_ref, wd_ref, gf_ref, o_ref, h_ref, *, tiles_per_seq,
                final_norm):
    i = pl.program_id(0)
    f = pl.program_id(1)
    tm = x_ref.shape[0]
    tf = wd_ref.shape[0]
    hidden = taps_ref.shape[1] // 2

    @pl.when(f == 0)
    def _():
        x = x_ref[...]
        prev = _rms(xh_ref[...], g_ref[...])
        prev = jnp.where(i % tiles_per_seq == 0, 0.0, prev)
        h_ref[0:HALO, :] = prev.astype(BF16)
        h_ref[HALO:HALO + tm, :] = _rms(x, g_ref[...]).astype(BF16)
        o_ref[...] = x

    def conv(u, first_col):
        taps = taps_ref[:, pl.ds(pl.multiple_of(first_col, LANES), tf)]
        y = taps[2:3, :] * u + taps[1:2, :] * pltpu.roll(u, 1, axis=0) + taps[0:1, :] * pltpu.roll(u, 2, axis=0)
        return y[HALO:, :] + taps[3:4, :]

    def up(lo, rc):
        h = h_ref[lo:lo + rc + HALO, :]
        return _dot(h, w_ref[0]), _dot(h, w_ref[1])

    sizes = [c * tm // sum(FFN_CHUNK_PARTS) for c in FFN_CHUNK_PARTS]
    starts = [sum(sizes[:n]) for n in range(len(sizes))]
    u_next = up(starts[0], sizes[0])
    for n, (lo, rc) in enumerate(zip(starts, sizes)):
        ua, uv = u_next
        if n + 1 < len(starts):
            u_next = up(starts[n + 1], sizes[n + 1])
        ya = conv(ua, f * tf)
        yv = conv(uv, hidden + f * tf)
        act = (ya * jax.nn.sigmoid(ya) * yv).astype(BF16)
        o_ref[lo:lo + rc, :] += _dot(act, wd_ref[...])

    if final_norm:
        @pl.when(f == pl.num_programs(1) - 1)
        def _():
            o_ref[...] = _rms(o_ref[...], gf_ref[...])


def ffn_col_tile(hidden):
    return _pick_tile(hidden, (512, 256, 128))


def stage_ffn_up(w_up):
    layers, d, two_hidden = w_up.shape
    tf = ffn_col_tile(two_hidden // 2)
    nf = two_hidden // 2 // tf
    return cast_blocks(w_up, (layers, nf, 2), (None, d, tf), lambda l, f, g: (l, 0, g * nf + f),
                       (layers, nf, 2, d, tf), (None, None, None, d, tf), lambda l, f, g: (l, f, g, 0, 0))


def conv_ffn(x, gain, w_tiles, conv_w, conv_b, w_down, layer, final_gain, seq_len, final_norm):
    m, d = x.shape
    hidden = w_down.shape[1]
    tm = _pick_tile(seq_len, (1024, 512, 256))
    _, nf, _, _, tf = w_tiles.shape
    halo_blocks = tm // HALO
    assert tm % (sum(FFN_CHUNK_PARTS) * SUBLANES_BF16) == 0
    taps = jnp.concatenate([conv_w, conv_b.reshape(1, -1)], axis=0)
    kern = functools.partial(_ffn_kernel, tiles_per_seq=seq_len // tm, final_norm=final_norm)
    return pl.pallas_call(
        kern,
        grid=(m // tm, nf),
        in_specs=[
            pl.BlockSpec((tm, d), lambda i, f: (i, 0)),
            pl.BlockSpec((HALO, d), lambda i, f: (jnp.maximum(i * halo_blocks - 1, 0), 0)),
            pl.BlockSpec((1, d), lambda i, f: (0, 0)),
            pl.BlockSpec((None, None, 2, d, tf), lambda i, f: (layer, f, 0, 0, 0)),
            pl.BlockSpec(taps.shape, lambda i, f: (0, 0)),
            pl.BlockSpec((None, tf, d), lambda i, f: (layer, f, 0)),
            pl.BlockSpec((1, d), lambda i, f: (0, 0)),
        ],
        out_specs=pl.BlockSpec((tm, d), lambda i, f: (i, 0)),
        out_shape=jax.ShapeDtypeStruct((m, d), F32),
        scratch_shapes=[pltpu.VMEM((tm + HALO, d), BF16)],
        compiler_params=_params("parallel", "arbitrary"),
        name="conv_ffn",
    )(x, x, gain.reshape(1, d), w_tiles, taps, w_down, final_gain.reshape(1, d))


GLA_BLOCK = 256


def _ref_rows(bc, half):
    rows, dk = bc.shape
    if half >= SUBLANES_F32:
        nb = rows // (2 * half)
        b3 = bc.reshape(nb, 2 * half, dk)
        return jnp.broadcast_to(b3[:, half:half + 1, :], b3.shape).reshape(rows, dk)
    b3 = bc.reshape(rows // SUBLANES_F32, SUBLANES_F32, dk)
    sub = lax.broadcasted_iota(jnp.int32, b3.shape, 1)
    out = None
    for mid in range(half, SUBLANES_F32, 2 * half):
        cand = jnp.broadcast_to(b3[:, mid:mid + 1, :], b3.shape)
        out = cand if out is None else jnp.where(sub >= mid - half, cand, out)
    return out.reshape(rows, dk)


GLA_HEADS_PER_STEP = 4


def _gla_kernel(q_ref, k_ref, v_ref, glr_ref, og_ref, wg_ref, bg_ref, gn_ref, o_ref, state_ref, *, rank, heads):
    t = pl.program_id(2)
    rows = q_ref.shape[0]
    dk = q_ref.shape[1] // heads
    dv = v_ref.shape[1] // heads
    blk = GLA_BLOCK if rows % GLA_BLOCK == 0 else rows
    n_levels = int(math.log2(blk))

    @pl.when(t == 0)
    def _():
        state_ref[...] = jnp.zeros_like(state_ref)

    ti = lax.broadcasted_iota(jnp.int32, (blk, blk), 0)
    si = lax.broadcasted_iota(jnp.int32, (blk, blk), 1)
    tri = jnp.where(si <= ti, 1.0, 0.0).astype(BF16)
    x = ti ^ si
    top_bit = 31 - lax.clz(jnp.maximum(x, 1))
    pair_level = jnp.where(si < ti, top_bit, jnp.where(si == ti, -1, -2))
    qscale = dk ** -0.5

    def step(i, carry):
        rs = pl.ds(pl.multiple_of(i * blk, blk), blk)
        glr = glr_ref[rs, 0:rank].astype(BF16)
        hs = range(heads)
        kc = [slice(hd * dk, (hd + 1) * dk) for hd in hs]
        vc = [slice(hd * dv, (hd + 1) * dv) for hd in hs]
        q = [q_ref[rs, kc[hd]].astype(F32) * qscale for hd in hs]
        k = [k_ref[rs, kc[hd]].astype(F32) for hd in hs]
        v = [v_ref[rs, vc[hd]].astype(BF16) for hd in hs]
        g = [_log_sigmoid(_dot(glr, wg_ref[hd]) + bg_ref[hd]) * (LOG2_E / GLA_GATE_TAU) for hd in hs]
        gs = [_split2(g[hd]) for hd in hs]
        bc = [_dot(tri, gs[hd][0]) + _dot(tri, gs[hd][1]) for hd in hs]
        b_last = [bc[hd][blk - 1:blk, :] for hd in hs]
        state = [state_ref[hd] for hd in hs]

        o = [_dot_nt((q[hd] * jnp.exp2(bc[hd])).astype(BF16), state[hd].astype(BF16)) for hd in hs]

        att = [jnp.zeros((blk, blk), F32) for hd in hs]
        q_b = [q[hd].astype(BF16) for hd in hs]
        k_b = [k[hd].astype(BF16) for hd in hs]
        for p in range(n_levels):
            e = [jnp.exp2(_neg_abs(bc[hd] - _ref_rows(bc[hd], 1 << p)).astype(BF16)) for hd in hs]
            sp = [_dot_nt(q_b[hd] * e[hd], k_b[hd] * e[hd]) for hd in hs]
            att = [jnp.where(pair_level == p, sp[hd], att[hd]) for hd in hs]
        sd = [_dot_nt(q_b[hd], k_b[hd]) for hd in hs]
        att = [jnp.where(pair_level == -1, sd[hd], att[hd]) for hd in hs]
        o = [o[hd] + _dot(att[hd].astype(BF16), v[hd]) for hd in hs]

        kd = [(k[hd] * jnp.exp2(b_last[hd] - bc[hd])).astype(BF16) for hd in hs]
        upd = [_dot_tn(v[hd], kd[hd]) for hd in hs]
        for hd in hs:
            state_ref[hd] = state[hd] * jnp.exp2(b_last[hd]) + upd[hd]

        for hd in hs:
            on = o[hd] * lax.rsqrt(jnp.mean(o[hd] * o[hd], axis=-1, keepdims=True) + EPS) * gn_ref[hd]
            og = og_ref[rs, vc[hd]].astype(F32)
            o_ref[rs, vc[hd]] = (on * (og * jax.nn.sigmoid(og))).astype(o_ref.dtype)
        return carry

    lax.fori_loop(0, rows // blk, step, 0)


def gla_attention(z, w_gate, b_gate, gn_gain, batch, seq_len, dk, dv, rank):
    m = z.shape[0]
    heads = GLA_HEADS
    hp = GLA_HEADS_PER_STEP
    groups = heads // hp
    t = _pick_tile(seq_len, (1024, 512, 256))
    tiles = seq_len // t
    v0, og0 = 0, groups
    q0 = (2 * heads * dv) // (hp * dk)
    k0 = q0 + groups
    glr0 = (2 * heads * dv + 2 * heads * dk + MEM_HEADS * LANES) // LANES
    assert (2 * heads * dv) % (hp * dk) == 0
    rows = lambda b, h, s: b * tiles + s
    kern = functools.partial(_gla_kernel, rank=rank, heads=hp)
    return pl.pallas_call(
        kern,
        grid=(batch, groups, tiles),
        in_specs=[
            pl.BlockSpec((t, hp * dk), lambda b, h, s: (rows(b, h, s), q0 + h)),
            pl.BlockSpec((t, hp * dk), lambda b, h, s: (rows(b, h, s), k0 + h)),
            pl.BlockSpec((t, hp * dv), lambda b, h, s: (rows(b, h, s), v0 + h)),
            pl.BlockSpec((t, LANES), lambda b, h, s: (rows(b, h, s), glr0)),
            pl.BlockSpec((t, hp * dv), lambda b, h, s: (rows(b, h, s), og0 + h)),
            pl.BlockSpec((hp, rank, dk), lambda b, h, s: (h, 0, 0)),
            pl.BlockSpec((hp, 1, dk), lambda b, h, s: (h, 0, 0)),
            pl.BlockSpec((hp, 1, dv), lambda b, h, s: (h, 0, 0)),
        ],
        out_specs=pl.BlockSpec((t, hp * dv), lambda b, h, s: (rows(b, h, s), h)),
        out_shape=jax.ShapeDtypeStruct((m, heads * dv), BF16),
        scratch_shapes=[pltpu.VMEM((hp, dv, dk), F32)],
        compiler_params=_params("parallel", "parallel", "arbitrary"),
        name="gla_attention",
    )(z, z, z, z, z, w_gate, b_gate, gn_gain)


BIAS_PIECES = 3


def _split3(x):
    hi = x.astype(BF16)
    r1 = x - hi.astype(F32)
    mid = r1.astype(BF16)
    lo = (r1 - mid.astype(F32)).astype(BF16)
    return hi, mid, lo


def _fox_gate_kernel(x_ref, g_ref, w_ref, b_ref, route_ref, o_ref, carry_ref):
    s = pl.program_id(1)
    t = x_ref.shape[0]

    @pl.when(s == 0)
    def _():
        carry_ref[...] = jnp.zeros_like(carry_ref)

    h = _rms(x_ref[...], g_ref[...]).astype(BF16)
    log_f = _log_sigmoid(_dot(h, w_ref[...]) + b_ref[...])
    ti = lax.broadcasted_iota(jnp.int32, (t, t), 0)
    si = lax.broadcasted_iota(jnp.int32, (t, t), 1)
    tri = jnp.where(si <= ti, 1.0, 0.0).astype(BF16)
    f_hi, f_mid, f_lo = _split3(log_f)
    c = carry_ref[...] + _dot(tri, f_hi) + _dot(tri, f_mid) + _dot(tri, f_lo)
    carry_ref[...] = c[t - 1:t, :]
    pieces = jnp.concatenate(_split3(c * (-LOG2_E)), axis=1)
    o_ref[...] = _dot(pieces, route_ref[...]).astype(BF16)


def fox_gates(x, gain, w_f, b_f, batch, seq_len):
    m, d = x.shape
    t = _pick_tile(seq_len, (512, 256))
    tiles = seq_len // t
    src = lax.broadcasted_iota(jnp.int32, (BIAS_PIECES * LANES, FOX_HEADS * LANES), 0)
    dst = lax.broadcasted_iota(jnp.int32, (BIAS_PIECES * LANES, FOX_HEADS * LANES), 1)
    route = ((src // LANES == dst % LANES) & (src % LANES == dst // LANES)).astype(BF16)
    return pl.pallas_call(
        _fox_gate_kernel,
        grid=(batch, tiles),
        in_specs=[
            pl.BlockSpec((t, d), lambda b, s: (b * tiles + s, 0)),
            pl.BlockSpec((1, d), lambda b, s: (0, 0)),
            pl.BlockSpec((d, LANES), lambda b, s: (0, 0)),
            pl.BlockSpec((1, LANES), lambda b, s: (0, 0)),
            pl.BlockSpec(route.shape, lambda b, s: (0, 0)),
        ],
        out_specs=pl.BlockSpec((t, FOX_HEADS * LANES), lambda b, s: (b * tiles + s, 0)),
        out_shape=jax.ShapeDtypeStruct((m, FOX_HEADS * LANES), BF16),
        scratch_shapes=[pltpu.VMEM((1, LANES), F32)],
        compiler_params=_params("parallel", "arbitrary"),
        name="fox_gates",
    )(x, gain.reshape(1, d), w_f, b_f, route)


FOX_HEADS_PER_STEP = 2


def _fox_attn_kernel(q_ref, k_ref, kb_ref, v_ref, o_ref, m_ref, acc_ref, s0_ref, s1_ref, *, heads):
    qi = pl.program_id(2)
    tq = q_ref.shape[0]
    tk = tq // 2
    dh = q_ref.shape[1] // heads
    cols = [slice(hd * dh, (hd + 1) * dh) for hd in range(heads)]
    lane = lax.broadcasted_iota(jnp.int32, (tq, dh), 1)
    bias_taps = jnp.where(lane < BIAS_PIECES, 1.0, 0.0).astype(BF16)
    ones_col = jnp.where(lax.broadcasted_iota(jnp.int32, (tk, dh), 1) == 0, 1.0, 0.0).astype(BF16)
    qs = [jnp.concatenate([(q_ref[:, cols[hd]].astype(F32) * (dh ** -0.5 * LOG2_E)).astype(BF16), bias_taps],
                          axis=1) for hd in range(heads)]

    def key_rows(j):
        return pl.ds(pl.multiple_of(j * tk, tk), tk)

    def issue_scores(s_ref, j, rows):
        ks = key_rows(j)
        for hd in range(heads):
            keys = jnp.concatenate([k_ref[ks, cols[hd]], kb_ref[ks, cols[hd]]], axis=1)
            s_ref[hd, rows, :] = _dot_nt(qs[hd][rows], keys)

    def update(scores, j, rows):
        hs = range(heads)
        m_prev = [m_ref[hd, rows, :] for hd in hs]
        m_new = [jnp.maximum(m_prev[hd], jnp.max(scores[hd], axis=-1, keepdims=True)) for hd in hs]
        p = [jnp.concatenate([jnp.exp2((scores[hd][:, c:c + LANES] - m_new[hd]).astype(BF16))
                              for c in range(0, tk, LANES)], axis=1) for hd in hs]
        pv = [_dot(p[hd], jnp.concatenate([v_ref[key_rows(j), cols[hd]], ones_col], axis=1)) for hd in hs]
        for hd in hs:
            alpha = jnp.exp2(m_prev[hd] - m_new[hd])
            acc = acc_ref[hd, rows, :]
            acc_ref[hd, rows, :] = jnp.concatenate(
                [alpha * acc[:, c:c + LANES] for c in range(0, 2 * dh, LANES)], axis=1) + pv[hd]
            m_ref[hd, rows, :] = m_new[hd]

    every = slice(0, tq)
    lower = slice(tk, tq)

    m_ref[...] = jnp.full(m_ref.shape, -jnp.inf, F32)
    acc_ref[...] = jnp.zeros_like(acc_ref)
    issue_scores(s0_ref, 0, every)

    def pair(jj, carry):
        j = 2 * jj
        issue_scores(s1_ref, j + 1, every)
        update([s0_ref[hd] for hd in range(heads)], j, every)
        issue_scores(s0_ref, j + 2, every)
        update([s1_ref[hd] for hd in range(heads)], j + 1, every)
        return carry

    lax.fori_loop(0, qi, pair, 0)

    issue_scores(s1_ref, 2 * qi + 1, lower)
    ti = lax.broadcasted_iota(jnp.int32, (tq, tk), 0)
    si = lax.broadcasted_iota(jnp.int32, (tq, tk), 1)
    update([jnp.where(si <= ti, s0_ref[hd], -jnp.inf) for hd in range(heads)], 2 * qi, every)
    tl = lax.broadcasted_iota(jnp.int32, (tk, tk), 0)
    sl = lax.broadcasted_iota(jnp.int32, (tk, tk), 1)
    update([jnp.where(sl <= tl, s1_ref[hd, lower, :], -jnp.inf) for hd in range(heads)], 2 * qi + 1, lower)
    for hd in range(heads):
        acc = acc_ref[hd]
        o_ref[:, cols[hd]] = (acc[:, 0:dh] / acc[:, dh:dh + 1]).astype(o_ref.dtype)


def fox_attention(zq, kv, kbias, batch, seq_len):
    m = zq.shape[0]
    hp = FOX_HEADS_PER_STEP
    width = hp * LANES
    groups = FOX_HEADS // hp
    tq = _pick_tile(seq_len, (1024, 512, 256))
    tiles = seq_len // tq
    kern = functools.partial(_fox_attn_kernel, heads=hp)
    return pl.pallas_call(
        kern,
        grid=(batch, groups, tiles),
        in_specs=[
            pl.BlockSpec((tq, width), lambda b, h, s: (b * tiles + s, h)),
            pl.BlockSpec((seq_len, width), lambda b, h, s: (b, h)),
            pl.BlockSpec((seq_len, width), lambda b, h, s: (b, h)),
            pl.BlockSpec((seq_len, width), lambda b, h, s: (b, groups + h)),
        ],
        out_specs=pl.BlockSpec((tq, width), lambda b, h, s: (b * tiles + s, h)),
        out_shape=jax.ShapeDtypeStruct((m, FOX_HEADS * LANES), BF16),
        scratch_shapes=[pltpu.VMEM((hp, tq, LANES), F32), pltpu.VMEM((hp, tq, 2 * LANES), F32),
                        pltpu.VMEM((hp, tq, tq // 2), F32), pltpu.VMEM((hp, tq, tq // 2), F32)],
        compiler_params=_params("parallel", "parallel", "arbitrary"),
        name="fox_attention",
    )(zq, kv, kbias, kv)


def kernel(x, mem, norm_mix, norm_ffn, norm_mem, norm_final, mem_w_kv, gla_w_in, gla_w_gate_up,
           gla_b_gate, gla_norm, gla_w_out, fox_kv_norm, fox_w_kv, fox_b_f, fox_w_in, fox_w_out,
           ffn_w_up, ffn_conv_w, ffn_conv_b, ffn_w_down):
    batch, seq_len, d = x.shape
    n_mem = mem.shape[1]
    depth = norm_mix.shape[0]
    n_gla = gla_w_in.shape[0]
    rank = gla_w_gate_up.shape[1]
    qk_width = gla_w_gate_up.shape[2]
    dk = qk_width // GLA_HEADS
    v_width = gla_norm.shape[1]
    dv = v_width // GLA_HEADS
    mem_width = mem_w_kv.shape[2] // 2
    fox_width = (fox_w_kv.shape[1] - FOX_HEADS) // 2
    assert mem_width == MEM_HEADS * LANES and fox_width == FOX_HEADS * LANES
    assert (2 * v_width) % dk == 0 and rank <= LANES

    xf = x.reshape(batch * seq_len, d)

    tr = _pick_tile(d, (512, 256, 128))
    w_mem = cast_blocks(mem_w_kv, (depth, d // tr), (None, tr, 2 * mem_width), lambda l, r: (l, r, 0),
                        (d, depth * 2 * mem_width), (tr, 2 * mem_width), lambda l, r: (r, l))
    ffn_up = stage_ffn_up(ffn_w_up)
    ffn_down = cast_rows(ffn_w_down)
    gla_out, fox_in, fox_out = cast_rows(gla_w_out), cast_rows(fox_w_in), cast_rows(fox_w_out)
    mem_kv = norm_matmul(mem.reshape(batch * n_mem, d), norm_mem, w_mem, BF16,
                         _pick_tile(w_mem.shape[1], (1024, 512)))

    fox_kv = fox_c = None
    for i in range(depth):
        if i == n_gla:
            w_kv = cast_blocks(fox_w_kv, (d // tr,), (tr, 2 * fox_width), lambda r: (r, 0),
                               (d, 2 * fox_width), (tr, 2 * fox_width), lambda r: (r, 0))
            fox_kv = norm_matmul(xf, fox_kv_norm, w_kv, BF16,
                                 _pick_tile(2 * fox_width, (1024, 768, 512)))
            w_f = jnp.pad(fox_w_kv[:, 2 * fox_width:], ((0, 0), (0, LANES - FOX_HEADS))).astype(BF16)
            b_f = jnp.pad(fox_b_f, (0, LANES - FOX_HEADS)).reshape(1, LANES)
            fox_c = fox_gates(xf, fox_kv_norm, w_f, b_f, batch, seq_len)
        if i < n_gla:
            w = gla_w_in[i]
            o_q, o_k, o_v = 0, qk_width, 2 * qk_width
            o_glr = o_v + v_width
            o_og = o_glr + rank
            o_mq = o_og + v_width
            w_in = jnp.concatenate([
                w[:, o_v:o_glr], w[:, o_og:o_mq], w[:, o_q:o_k], w[:, o_k:o_v], w[:, o_mq:],
                jnp.pad(w[:, o_glr:o_og], ((0, 0), (0, LANES - rank)))], axis=1).astype(BF16)
            z = norm_matmul(xf, norm_mix[i], w_in, BF16, 8 * 2 * LANES)
            a = gla_attention(z, gla_w_gate_up[i].reshape(rank, GLA_HEADS, dk).transpose(1, 0, 2).astype(BF16),
                              gla_b_gate[i].reshape(GLA_HEADS, 1, dk), gla_norm[i].reshape(GLA_HEADS, 1, dv),
                              batch, seq_len, dk, dv, rank)
            mq_block = (2 * v_width + 2 * qk_width) // mem_width
            w_out, w_layer = gla_out, i
        else:
            j = i - n_gla
            z = norm_matmul(xf, norm_mix[i], fox_in, BF16, _pick_tile(fox_w_in.shape[2], (1024, 512)), layer=j)
            a = fox_attention(z, fox_kv, fox_c, batch, seq_len)
            mq_block = fox_width // mem_width
            w_out, w_layer = fox_out, j
        xf = mixer_out(a, z, mq_block, mem_kv, i, w_out, w_layer, xf, seq_len, n_mem)
        xf = conv_ffn(xf, norm_ffn[i], ffn_up, ffn_conv_w[i], ffn_conv_b[i], ffn_down, i, norm_final,
                      seq_len, final_norm=(i == depth - 1))
    return xf.reshape(batch, seq_len, d)
```

```python
import functools
import math

import jax
import jax.numpy as jnp
from jax import lax
from jax.experimental import pallas as pl
from jax.experimental.pallas import tpu as pltpu

GLA_HEADS = 4
GLA_GATE_TAU = 16.0
FOX_HEADS = 12
MEM_HEADS = 4
CONV_WIDTH = 3
EPS = 1e-6
LOG2_E = math.log2(math.e)

LANES = 128
SUBLANES_F32 = 8
SUBLANES_BF16 = 16
VMEM_LIMIT = 56 * 1024 * 1024

BF16 = jnp.bfloat16
F32 = jnp.float32

NT_DIMS = (((1,), (1,)), ((), ()))
TN_DIMS = (((0,), (0,)), ((), ()))


def _params(*sem):
    return pltpu.CompilerParams(dimension_semantics=sem, vmem_limit_bytes=VMEM_LIMIT)


def _rms(x, gain):
    return x * lax.rsqrt(jnp.mean(x * x, axis=-1, keepdims=True) + EPS) * gain


def _dot(a, b):
    return jnp.dot(a, b, preferred_element_type=F32)


def _dot_nt(a, b):
    return lax.dot_general(a, b, NT_DIMS, preferred_element_type=F32)


def _dot_tn(a, b):
    return lax.dot_general(a, b, TN_DIMS, preferred_element_type=F32)


def _split2(x):
    hi = x.astype(BF16)
    lo = (x - hi.astype(F32)).astype(BF16)
    return hi, lo


def _neg_abs(x):
    bits = lax.bitcast_convert_type(x, jnp.uint32) | jnp.uint32(0x80000000)
    return lax.bitcast_convert_type(bits, F32)


def _log_sigmoid(x):
    return jnp.minimum(x, 0.0) - jnp.log1p(jnp.exp(-jnp.abs(x)))


def _pick_tile(n, preferred):
    for t in preferred:
        if n % t == 0:
            return t
    return n


def _cast_kernel(x_ref, o_ref):
    o_ref[...] = x_ref[...].astype(o_ref.dtype)


def cast_blocks(w, grid, in_block, in_map, out_shape, out_block, out_map):
    return pl.pallas_call(
        _cast_kernel,
        grid=grid,
        in_specs=[pl.BlockSpec(in_block, in_map)],
        out_specs=pl.BlockSpec(out_block, out_map),
        out_shape=jax.ShapeDtypeStruct(out_shape, BF16),
        compiler_params=_params(*(("parallel",) * len(grid))),
        name="cast_blocks",
    )(w)


def cast_rows(w):
    layers, rows, cols = w.shape
    tr = _pick_tile(rows, (512, 256, 128))
    return cast_blocks(w, (layers, rows // tr), (None, tr, cols), lambda l, r: (l, r, 0),
                       w.shape, (None, tr, cols), lambda l, r: (l, r, 0))


def _norm_matmul_kernel(x_ref, g_ref, w_ref, o_ref, *, tn):
    h = _rms(x_ref[...], g_ref[...]).astype(BF16)
    n = o_ref.shape[1]
    for c in range(0, n, tn):
        cols = slice(c, min(c + tn, n))
        o_ref[:, cols] = _dot(h, w_ref[:, cols]).astype(o_ref.dtype)


def norm_matmul(x, gain, w, out_dtype, tn, layer=None):
    m, d = x.shape
    n = w.shape[-1]
    if layer is None:
        w_spec = pl.BlockSpec((d, n), lambda i: (0, 0), pipeline_mode=pl.Buffered(1))
    else:
        w_spec = pl.BlockSpec((None, d, n), lambda i: (layer, 0, 0), pipeline_mode=pl.Buffered(1))
    out_bytes = jnp.dtype(out_dtype).itemsize

    def vmem_need(tm):
        return 2 * tm * d * 4 + d * n * 2 + 2 * tm * n * out_bytes + tm * d * 2 + tm * tn * 4

    tm = next(t for t in (1024, 512, 256, 128) if m % t == 0 and vmem_need(t) <= 0.9 * VMEM_LIMIT)
    return pl.pallas_call(
        functools.partial(_norm_matmul_kernel, tn=tn),
        grid=(m // tm,),
        in_specs=[
            pl.BlockSpec((tm, d), lambda i: (i, 0)),
            pl.BlockSpec((1, d), lambda i: (0, 0)),
            w_spec,
        ],
        out_specs=pl.BlockSpec((tm, n), lambda i: (i, 0)),
        out_shape=jax.ShapeDtypeStruct((m, n), out_dtype),
        compiler_params=_params("parallel"),
        name="norm_matmul",
    )(x, gain.reshape(1, d), w)


def _mixer_out_kernel(a_ref, q_ref, mk_ref, mv_ref, w_ref, x_ref, o_ref, *, heads):
    ka = a_ref.shape[1]
    main = _dot(a_ref[...], w_ref[0:ka, :])
    dh = q_ref.shape[1] // heads
    scale = dh ** -0.5
    cols = [slice(hd * dh, (hd + 1) * dh) for hd in range(heads)]
    scores = [_dot_nt(q_ref[:, c], mk_ref[:, c]) for c in cols]
    probs, inv_sums = [], []
    for s in scores:
        s = s * scale
        p = jnp.exp(s - jnp.max(s, axis=-1, keepdims=True))
        probs.append(p.astype(BF16))
        inv_sums.append(1.0 / jnp.sum(p, axis=-1, keepdims=True))
    mem = [(_dot(p, mv_ref[:, c]) * r).astype(BF16) for p, r, c in zip(probs, inv_sums, cols)]
    o_ref[...] = x_ref[...] + main + _dot(jnp.concatenate(mem, axis=1), w_ref[ka:, :])


def mixer_out(a, z, q_col_block, mem_kv, layer, w, w_layer, x, seq_len, n_mem):
    m, d = x.shape
    ka = a.shape[1]
    mem_width = MEM_HEADS * LANES
    tm = _pick_tile(seq_len, (512, 256))
    tiles = seq_len // tm
    kern = functools.partial(_mixer_out_kernel, heads=MEM_HEADS)
    return pl.pallas_call(
        kern,
        grid=(m // tm,),
        in_specs=[
            pl.BlockSpec((tm, ka), lambda i: (i, 0)),
            pl.BlockSpec((tm, mem_width), lambda i: (i, q_col_block)),
            pl.BlockSpec((n_mem, mem_width), lambda i: (i // tiles, 2 * layer)),
            pl.BlockSpec((n_mem, mem_width), lambda i: (i // tiles, 2 * layer + 1)),
            pl.BlockSpec((None, ka + mem_width, d), lambda i: (w_layer, 0, 0), pipeline_mode=pl.Buffered(1)),
            pl.BlockSpec((tm, d), lambda i: (i, 0)),
        ],
        out_specs=pl.BlockSpec((tm, d), lambda i: (i, 0)),
        out_shape=jax.ShapeDtypeStruct((m, d), F32),
        compiler_params=_params("parallel"),
        name="mixer_out",
    )(a, z, mem_kv, mem_kv, w, x)


HALO = SUBLANES_BF16
FFN_CHUNK_PARTS = (1, 1)


def _ffn_kernel(x_ref, xh_ref, g_ref, w_ref, taps_ref, wd_ref, gf_ref, o_ref, h_ref, *, tiles_per_seq,
                final_norm):
    i = pl.program_id(0)
    f = pl.program_id(1)
    tm = x_ref.shape[0]
    tf = wd_ref.shape[0]
    hidden = taps_ref.shape[1] // 2

    @pl.when(f == 0)
    def _():
        x = x_ref[...]
        prev = _rms(xh_ref[...], g_ref[...])
        prev = jnp.where(i % tiles_per_seq == 0, 0.0, prev)
        h_ref[0:HALO, :] = prev.astype(BF16)
        h_ref[HALO:HALO + tm, :] = _rms(x, g_ref[...]).astype(BF16)
        o_ref[...] = x

    def conv(u, first_col):
        taps = taps_ref[:, pl.ds(pl.multiple_of(first_col, LANES), tf)]
        y = taps[2:3, :] * u + taps[1:2, :] * pltpu.roll(u, 1, axis=0) + taps[0:1, :] * pltpu.roll(u, 2, axis=0)
        return y[HALO:, :] + taps[3:4, :]

    def up(lo, rc):
        h = h_ref[lo:lo + rc + HALO, :]
        return _dot(h, w_ref[0]), _dot(h, w_ref[1])

    sizes = [c * tm // sum(FFN_CHUNK_PARTS) for c in FFN_CHUNK_PARTS]
    starts = [sum(sizes[:n]) for n in range(len(sizes))]
    u_next = up(starts[0], sizes[0])
    for n, (lo, rc) in enumerate(zip(starts, sizes)):
        ua, uv = u_next
        if n + 1 < len(starts):
            u_next = up(starts[n + 1], sizes[n + 1])
        ya = conv(ua, f * tf)
        yv = conv(uv, hidden + f * tf)
        act = (ya * jax.nn.sigmoid(ya) * yv).astype(BF16)
        o_ref[lo:lo + rc, :] += _dot(act, wd_ref[...])

    if final_norm:
        @pl.when(f == pl.num_programs(1) - 1)
        def _():
            o_ref[...] = _rms(o_ref[...], gf_ref[...])


def ffn_col_tile(hidden):
    return _pick_tile(hidden, (512, 256, 128))


def stage_ffn_up(w_up):
    layers, d, two_hidden = w_up.shape
    tf = ffn_col_tile(two_hidden // 2)
    nf = two_hidden // 2 // tf
    return cast_blocks(w_up, (layers, nf, 2), (None, d, tf), lambda l, f, g: (l, 0, g * nf + f),
                       (layers, nf, 2, d, tf), (None, None, None, d, tf), lambda l, f, g: (l, f, g, 0, 0))


def conv_ffn(x, gain, w_tiles, conv_w, conv_b, w_down, layer, final_gain, seq_len, final_norm):
    m, d = x.shape
    hidden = w_down.shape[1]
    tm = _pick_tile(seq_len, (1024, 512, 256))
    _, nf, _, _, tf = w_tiles.shape
    halo_blocks = tm // HALO
    assert tm % (sum(FFN_CHUNK_PARTS) * SUBLANES_BF16) == 0
    taps = jnp.concatenate([conv_w, conv_b.reshape(1, -1)], axis=0)
    kern = functools.partial(_ffn_kernel, tiles_per_seq=seq_len // tm, final_norm=final_norm)
    return pl.pallas_call(
        kern,
        grid=(m // tm, nf),
        in_specs=[
            pl.BlockSpec((tm, d), lambda i, f: (i, 0)),
            pl.BlockSpec((HALO, d), lambda i, f: (jnp.maximum(i * halo_blocks - 1, 0), 0)),
            pl.BlockSpec((1, d), lambda i, f: (0, 0)),
            pl.BlockSpec((None, None, 2, d, tf), lambda i, f: (layer, f, 0, 0, 0)),
            pl.BlockSpec(taps.shape, lambda i, f: (0, 0)),
            pl.BlockSpec((None, tf, d), lambda i, f: (layer, f, 0)),
            pl.BlockSpec((1, d), lambda i, f: (0, 0)),
        ],
        out_specs=pl.BlockSpec((tm, d), lambda i, f: (i, 0)),
        out_shape=jax.ShapeDtypeStruct((m, d), F32),
        scratch_shapes=[pltpu.VMEM((tm + HALO, d), BF16)],
        compiler_params=_params("parallel", "arbitrary"),
        name="conv_ffn",
    )(x, x, gain.reshape(1, d), w_tiles, taps, w_down, final_gain.reshape(1, d))


GLA_BLOCK = 256


def _ref_rows(bc, half):
    rows, dk = bc.shape
    if half >= SUBLANES_F32:
        nb = rows // (2 * half)
        b3 = bc.reshape(nb, 2 * half, dk)
        return jnp.broadcast_to(b3[:, half:half + 1, :], b3.shape).reshape(rows, dk)
    b3 = bc.reshape(rows // SUBLANES_F32, SUBLANES_F32, dk)
    sub = lax.broadcasted_iota(jnp.int32, b3.shape, 1)
    out = None
    for mid in range(half, SUBLANES_F32, 2 * half):
        cand = jnp.broadcast_to(b3[:, mid:mid + 1, :], b3.shape)
        out = cand if out is None else jnp.where(sub >= mid - half, cand, out)
    return out.reshape(rows, dk)


GLA_HEADS_PER_STEP = 4


def _gla_kernel(q_ref, k_ref, v_ref, glr_ref, og_ref, wg_ref, bg_ref, gn_ref, o_ref, state_ref, *, rank, heads):
    t = pl.program_id(2)
    rows = q_ref.shape[0]
    dk = q_ref.shape[1] // heads
    dv = v_ref.shape[1] // heads
    blk = GLA_BLOCK if rows % GLA_BLOCK == 0 else rows
    n_levels = int(math.log2(blk))

    @pl.when(t == 0)
    def _():
        state_ref[...] = jnp.zeros_like(state_ref)

    ti = lax.broadcasted_iota(jnp.int32, (blk, blk), 0)
    si = lax.broadcasted_iota(jnp.int32, (blk, blk), 1)
    tri = jnp.where(si <= ti, 1.0, 0.0).astype(BF16)
    x = ti ^ si
    top_bit = 31 - lax.clz(jnp.maximum(x, 1))
    pair_level = jnp.where(si < ti, top_bit, jnp.where(si == ti, -1, -2))
    qscale = dk ** -0.5

    def step(i, carry):
        rs = pl.ds(pl.multiple_of(i * blk, blk), blk)
        glr = glr_ref[rs, 0:rank].astype(BF16)
        hs = range(heads)
        kc = [slice(hd * dk, (hd + 1) * dk) for hd in hs]
        vc = [slice(hd * dv, (hd + 1) * dv) for hd in hs]
        q = [q_ref[rs, kc[hd]].astype(F32) * qscale for hd in hs]
        k = [k_ref[rs, kc[hd]].astype(F32) for hd in hs]
        v = [v_ref[rs, vc[hd]].astype(BF16) for hd in hs]
        g = [_log_sigmoid(_dot(glr, wg_ref[hd]) + bg_ref[hd]) * (LOG2_E / GLA_GATE_TAU) for hd in hs]
        gs = [_split2(g[hd]) for hd in hs]
        bc = [_dot(tri, gs[hd][0]) + _dot(tri, gs[hd][1]) for hd in hs]
        b_last = [bc[hd][blk - 1:blk, :] for hd in hs]
        state = [state_ref[hd] for hd in hs]

        o = [_dot_nt((q[hd] * jnp.exp2(bc[hd])).astype(BF16), state[hd].astype(BF16)) for hd in hs]

        att = [jnp.zeros((blk, blk), F32) for hd in hs]
        q_b = [q[hd].astype(BF16) for hd in hs]
        k_b = [k[hd].astype(BF16) for hd in hs]
        for p in range(n_levels):
            e = [jnp.exp2(_neg_abs(bc[hd] - _ref_rows(bc[hd], 1 << p)).astype(BF16)) for hd in hs]
            sp = [_dot_nt(q_b[hd] * e[hd], k_b[hd] * e[hd]) for hd in hs]
            att = [jnp.where(pair_level == p, sp[hd], att[hd]) for hd in hs]
        sd = [_dot_nt(q_b[hd], k_b[hd]) for hd in hs]
        att = [jnp.where(pair_level == -1, sd[hd], att[hd]) for hd in hs]
        o = [o[hd] + _dot(att[hd].astype(BF16), v[hd]) for hd in hs]

        kd = [(k[hd] * jnp.exp2(b_last[hd] - bc[hd])).astype(BF16) for hd in hs]
        upd = [_dot_tn(v[hd], kd[hd]) for hd in hs]
        for hd in hs:
            state_ref[hd] = state[hd] * jnp.exp2(b_last[hd]) + upd[hd]

        for hd in hs:
            on = o[hd] * lax.rsqrt(jnp.mean(o[hd] * o[hd], axis=-1, keepdims=True) + EPS) * gn_ref[hd]
            og = og_ref[rs, vc[hd]].astype(F32)
            o_ref[rs, vc[hd]] = (on * (og * jax.nn.sigmoid(og))).astype(o_ref.dtype)
        return carry

    lax.fori_loop(0, rows // blk, step, 0)


def gla_attention(z, w_gate, b_gate, gn_gain, batch, seq_len, dk, dv, rank):
    m = z.shape[0]
    heads = GLA_HEADS
    hp = GLA_HEADS_PER_STEP
    groups = heads // hp
    t = _pick_tile(seq_len, (1024, 512, 256))
    tiles = seq_len // t
    v0, og0 = 0, groups
    q0 = (2 * heads * dv) // (hp * dk)
    k0 = q0 + groups
    glr0 = (2 * heads * dv + 2 * heads * dk + MEM_HEADS * LANES) // LANES
    assert (2 * heads * dv) % (hp * dk) == 0
    rows = lambda b, h, s: b * tiles + s
    kern = functools.partial(_gla_kernel, rank=rank, heads=hp)
    return pl.pallas_call(
        kern,
        grid=(batch, groups, tiles),
        in_specs=[
            pl.BlockSpec((t, hp * dk), lambda b, h, s: (rows(b, h, s), q0 + h)),
            pl.BlockSpec((t, hp * dk), lambda b, h, s: (rows(b, h, s), k0 + h)),
            pl.BlockSpec((t, hp * dv), lambda b, h, s: (rows(b, h, s), v0 + h)),
            pl.BlockSpec((t, LANES), lambda b, h, s: (rows(b, h, s), glr0)),
            pl.BlockSpec((t, hp * dv), lambda b, h, s: (rows(b, h, s), og0 + h)),
            pl.BlockSpec((hp, rank, dk), lambda b, h, s: (h, 0, 0)),
            pl.BlockSpec((hp, 1, dk), lambda b, h, s: (h, 0, 0)),
            pl.BlockSpec((hp, 1, dv), lambda b, h, s: (h, 0, 0)),
        ],
        out_specs=pl.BlockSpec((t, hp * dv), lambda b, h, s: (rows(b, h, s), h)),
        out_shape=jax.ShapeDtypeStruct((m, heads * dv), BF16),
        scratch_shapes=[pltpu.VMEM((hp, dv, dk), F32)],
        compiler_params=_params("parallel", "parallel", "arbitrary"),
        name="gla_attention",
    )(z, z, z, z, z, w_gate, b_gate, gn_gain)


BIAS_PIECES = 3


def _split3(x):
    hi = x.astype(BF16)
    r1 = x - hi.astype(F32)
    mid = r1.astype(BF16)
    lo = (r1 - mid.astype(F32)).astype(BF16)
    return hi, mid, lo


def _fox_gate_kernel(x_ref, g_ref, w_ref, b_ref, route_ref, o_ref, carry_ref):
    s = pl.program_id(1)
    t = x_ref.shape[0]

    @pl.when(s == 0)
    def _():
        carry_ref[...] = jnp.zeros_like(carry_ref)

    h = _rms(x_ref[...], g_ref[...]).astype(BF16)
    half = h.shape[1] // 2
    logits = _dot(h[:, 0:half], w_ref[0:half, :]) + _dot(h[:, half:], w_ref[half:, :])
    log_f = _log_sigmoid(logits + b_ref[...])
    ti = lax.broadcasted_iota(jnp.int32, (t, t), 0)
    si = lax.broadcasted_iota(jnp.int32, (t, t), 1)
    tri = jnp.where(si <= ti, 1.0, 0.0).astype(BF16)
    sums = _dot(tri, jnp.concatenate(_split3(log_f), axis=1))
    c = carry_ref[...] + sums[:, 0:LANES] + sums[:, LANES:2 * LANES] + sums[:, 2 * LANES:3 * LANES]
    carry_ref[...] = c[t - 1:t, :]
    pieces = jnp.concatenate(_split3(c * (-LOG2_E)), axis=1)
    o_ref[...] = _dot(pieces, route_ref[...]).astype(BF16)


def fox_gates(x, gain, w_f, b_f, batch, seq_len):
    m, d = x.shape
    t = _pick_tile(seq_len, (512, 256))
    tiles = seq_len // t
    src = lax.broadcasted_iota(jnp.int32, (BIAS_PIECES * LANES, FOX_HEADS * LANES), 0)
    dst = lax.broadcasted_iota(jnp.int32, (BIAS_PIECES * LANES, FOX_HEADS * LANES), 1)
    route = ((src // LANES == dst % LANES) & (src % LANES == dst // LANES)).astype(BF16)
    return pl.pallas_call(
        _fox_gate_kernel,
        grid=(batch, tiles),
        in_specs=[
            pl.BlockSpec((t, d), lambda b, s: (b * tiles + s, 0)),
            pl.BlockSpec((1, d), lambda b, s: (0, 0)),
            pl.BlockSpec((d, LANES), lambda b, s: (0, 0)),
            pl.BlockSpec((1, LANES), lambda b, s: (0, 0)),
            pl.BlockSpec(route.shape, lambda b, s: (0, 0)),
        ],
        out_specs=pl.BlockSpec((t, FOX_HEADS * LANES), lambda b, s: (b * tiles + s, 0)),
        out_shape=jax.ShapeDtypeStruct((m, FOX_HEADS * LANES), BF16),
        scratch_shapes=[pltpu.VMEM((1, LANES), F32)],
        compiler_params=_params("parallel", "arbitrary"),
        name="fox_gates",
    )(x, gain.reshape(1, d), w_f, b_f, route)


FOX_HEADS_PER_STEP = 2


def _fox_attn_kernel(q_ref, k_ref, kb_ref, v_ref, o_ref, m_ref, acc_ref, s0_ref, s1_ref, *, heads):
    qi = pl.program_id(2)
    tq = q_ref.shape[0]
    tk = tq // 2
    dh = q_ref.shape[1] // heads
    cols = [slice(hd * dh, (hd + 1) * dh) for hd in range(heads)]
    lane = lax.broadcasted_iota(jnp.int32, (tq, dh), 1)
    bias_taps = jnp.where(lane < BIAS_PIECES, 1.0, 0.0).astype(BF16)
    ones_col = jnp.where(lax.broadcasted_iota(jnp.int32, (tk, dh), 1) == 0, 1.0, 0.0).astype(BF16)
    qs = [jnp.concatenate([(q_ref[:, cols[hd]].astype(F32) * (dh ** -0.5 * LOG2_E)).astype(BF16), bias_taps],
                          axis=1) for hd in range(heads)]

    def key_rows(j):
        return pl.ds(pl.multiple_of(j * tk, tk), tk)

    def issue_scores(s_ref, j, rows):
        ks = key_rows(j)
        for hd in range(heads):
            keys = jnp.concatenate([k_ref[ks, cols[hd]], kb_ref[ks, cols[hd]]], axis=1)
            s_ref[hd, rows, :] = _dot_nt(qs[hd][rows], keys)

    def update(hd, s, j, rows):
        m_prev = m_ref[hd, rows, :]
        m_new = jnp.maximum(m_prev, jnp.max(s, axis=-1, keepdims=True))
        alpha = jnp.exp2(m_prev - m_new)
        p = jnp.concatenate([jnp.exp2((s[:, c:c + LANES] - m_new).astype(BF16))
                             for c in range(0, tk, LANES)], axis=1)
        pv = _dot(p, jnp.concatenate([v_ref[key_rows(j), cols[hd]], ones_col], axis=1))
        acc = acc_ref[hd, rows, :]
        acc_ref[hd, rows, :] = jnp.concatenate([alpha * acc[:, c:c + LANES] for c in range(0, 2 * dh, LANES)],
                                               axis=1) + pv
        m_ref[hd, rows, :] = m_new

    every = slice(0, tq)
    lower = slice(tk, tq)

    m_ref[...] = jnp.full(m_ref.shape, -jnp.inf, F32)
    acc_ref[...] = jnp.zeros_like(acc_ref)
    issue_scores(s0_ref, 0, every)

    def pair(jj, carry):
        j = 2 * jj
        issue_scores(s1_ref, j + 1, every)
        for hd in range(heads):
            update(hd, s0_ref[hd], j, every)
        issue_scores(s0_ref, j + 2, every)
        for hd in range(heads):
            update(hd, s1_ref[hd], j + 1, every)
        return carry

    lax.fori_loop(0, qi, pair, 0)

    issue_scores(s1_ref, 2 * qi + 1, lower)
    ti = lax.broadcasted_iota(jnp.int32, (tq, tk), 0)
    si = lax.broadcasted_iota(jnp.int32, (tq, tk), 1)
    for hd in range(heads):
        update(hd, jnp.where(si <= ti, s0_ref[hd], -jnp.inf), 2 * qi, every)
    tl = lax.broadcasted_iota(jnp.int32, (tk, tk), 0)
    sl = lax.broadcasted_iota(jnp.int32, (tk, tk), 1)
    for hd in range(heads):
        update(hd, jnp.where(sl <= tl, s1_ref[hd, lower, :], -jnp.inf), 2 * qi + 1, lower)
    for hd in range(heads):
        acc = acc_ref[hd]
        o_ref[:, cols[hd]] = (acc[:, 0:dh] / acc[:, dh:dh + 1]).astype(o_ref.dtype)


def fox_attention(zq, kv, kbias, batch, seq_len):
    m = zq.shape[0]
    hp = FOX_HEADS_PER_STEP
    width = hp * LANES
    groups = FOX_HEADS // hp
    tq = _pick_tile(seq_len, (1024, 512, 256))
    tiles = seq_len // tq
    kern = functools.partial(_fox_attn_kernel, heads=hp)
    return pl.pallas_call(
        kern,
        grid=(batch, groups, tiles),
        in_specs=[
            pl.BlockSpec((tq, width), lambda b, h, s: (b * tiles + s, h)),
            pl.BlockSpec((seq_len, width), lambda b, h, s: (b, h)),
            pl.BlockSpec((seq_len, width), lambda b, h, s: (b, h)),
            pl.BlockSpec((seq_len, width), lambda b, h, s: (b, groups + h)),
        ],
        out_specs=pl.BlockSpec((tq, width), lambda b, h, s: (b * tiles + s, h)),
        out_shape=jax.ShapeDtypeStruct((m, FOX_HEADS * LANES), BF16),
        scratch_shapes=[pltpu.VMEM((hp, tq, LANES), F32), pltpu.VMEM((hp, tq, 2 * LANES), F32),
                        pltpu.VMEM((hp, tq, tq // 2), F32), pltpu.VMEM((hp, tq, tq // 2), F32)],
        compiler_params=_params("parallel", "parallel", "arbitrary"),
        name="fox_attention",
    )(zq, kv, kbias, kv)


def kernel(x, mem, norm_mix, norm_ffn, norm_mem, norm_final, mem_w_kv, gla_w_in, gla_w_gate_up,
           gla_b_gate, gla_norm, gla_w_out, fox_kv_norm, fox_w_kv, fox_b_f, fox_w_in, fox_w_out,
           ffn_w_up, ffn_conv_w, ffn_conv_b, ffn_w_down):
    batch, seq_len, d = x.shape
    n_mem = mem.shape[1]
    depth = norm_mix.shape[0]
    n_gla = gla_w_in.shape[0]
    rank = gla_w_gate_up.shape[1]
    qk_width = gla_w_gate_up.shape[2]
    dk = qk_width // GLA_HEADS
    v_width = gla_norm.shape[1]
    dv = v_width // GLA_HEADS
    mem_width = mem_w_kv.shape[2] // 2
    fox_width = (fox_w_kv.shape[1] - FOX_HEADS) // 2
    assert mem_width == MEM_HEADS * LANES and fox_width == FOX_HEADS * LANES
    assert (2 * v_width) % dk == 0 and rank <= LANES

    xf = x.reshape(batch * seq_len, d)

    tr = _pick_tile(d, (512, 256, 128))
    w_mem = cast_blocks(mem_w_kv, (depth, d // tr), (None, tr, 2 * mem_width), lambda l, r: (l, r, 0),
                        (d, depth * 2 * mem_width), (tr, 2 * mem_width), lambda l, r: (r, l))
    ffn_up = stage_ffn_up(ffn_w_up)
    ffn_down = cast_rows(ffn_w_down)
    gla_out, fox_in, fox_out = cast_rows(gla_w_out), cast_rows(fox_w_in), cast_rows(fox_w_out)
    mem_kv = norm_matmul(mem.reshape(batch * n_mem, d), norm_mem, w_mem, BF16,
                         _pick_tile(w_mem.shape[1], (1024, 512)))

    fox_kv = fox_c = None
    for i in range(depth):
        if i == n_gla:
            w_kv = cast_blocks(fox_w_kv, (d // tr,), (tr, 2 * fox_width), lambda r: (r, 0),
                               (d, 2 * fox_width), (tr, 2 * fox_width), lambda r: (r, 0))
            fox_kv = norm_matmul(xf, fox_kv_norm, w_kv, BF16,
                                 _pick_tile(2 * fox_width, (1024, 768, 512)))
            w_f = jnp.pad(fox_w_kv[:, 2 * fox_width:], ((0, 0), (0, LANES - FOX_HEADS))).astype(BF16)
            b_f = jnp.pad(fox_b_f, (0, LANES - FOX_HEADS)).reshape(1, LANES)
            fox_c = fox_gates(xf, fox_kv_norm, w_f, b_f, batch, seq_len)
        if i < n_gla:
            w = gla_w_in[i]
            o_q, o_k, o_v = 0, qk_width, 2 * qk_width
            o_glr = o_v + v_width
            o_og = o_glr + rank
            o_mq = o_og + v_width
            w_in = jnp.concatenate([
                w[:, o_v:o_glr], w[:, o_og:o_mq], w[:, o_q:o_k], w[:, o_k:o_v], w[:, o_mq:],
                jnp.pad(w[:, o_glr:o_og], ((0, 0), (0, LANES - rank)))], axis=1).astype(BF16)
            z = norm_matmul(xf, norm_mix[i], w_in, BF16, 8 * 2 * LANES)
            a = gla_attention(z, gla_w_gate_up[i].reshape(rank, GLA_HEADS, dk).transpose(1, 0, 2).astype(BF16),
                              gla_b_gate[i].reshape(GLA_HEADS, 1, dk), gla_norm[i].reshape(GLA_HEADS, 1, dv),
                              batch, seq_len, dk, dv, rank)
            mq_block = (2 * v_width + 2 * qk_width) // mem_width
            w_out, w_layer = gla_out, i
        else:
            j = i - n_gla
            z = norm_matmul(xf, norm_mix[i], fox_in, BF16, _pick_tile(fox_w_in.shape[2], (1024, 512)), layer=j)
            a = fox_attention(z, fox_kv, fox_c, batch, seq_len)
            mq_block = fox_width // mem_width
            w_out, w_layer = fox_out, j
        xf = mixer_out(a, z, mq_block, mem_kv, i, w_out, w_layer, xf, seq_len, n_mem)
        xf = conv_ffn(xf, norm_ffn[i], ffn_up, ffn_conv_w[i], ffn_conv_b[i], ffn_down, i, norm_final,
                      seq_len, final_norm=(i == depth - 1))
    return xf.reshape(batch, seq_len, d)
```

```python
import functools
import math

import jax
import jax.numpy as jnp
from jax import lax
from jax.experimental import pallas as pl
from jax.experimental.pallas import tpu as pltpu

GLA_HEADS = 4
GLA_GATE_TAU = 16.0
FOX_HEADS = 12
MEM_HEADS = 4
CONV_WIDTH = 3
EPS = 1e-6
LOG2_E = math.log2(math.e)

LANES = 128
SUBLANES_F32 = 8
SUBLANES_BF16 = 16
MXU_TILE = 256
VMEM_LIMIT = 56 * 1024 * 1024
VMEM_TILE_BUDGET = 0.9 * VMEM_LIMIT

BF16 = jnp.bfloat16
F32 = jnp.float32

NT_DIMS = (((1,), (1,)), ((), ()))
TN_DIMS = (((0,), (0,)), ((), ()))


def _params(*sem):
    return pltpu.CompilerParams(dimension_semantics=sem, vmem_limit_bytes=VMEM_LIMIT)


def _rms(x, gain):
    return x * lax.rsqrt(jnp.mean(x * x, axis=-1, keepdims=True) + EPS) * gain


def _dot(a, b):
    return jnp.dot(a, b, preferred_element_type=F32)


def _dot_nt(a, b):
    return lax.dot_general(a, b, NT_DIMS, preferred_element_type=F32)


def _dot_tn(a, b):
    return lax.dot_general(a, b, TN_DIMS, preferred_element_type=F32)


def _split2(x):
    hi = x.astype(BF16)
    lo = (x - hi.astype(F32)).astype(BF16)
    return hi, lo


def _neg_abs(x):
    bits = lax.bitcast_convert_type(x, jnp.uint32) | jnp.uint32(0x80000000)
    return lax.bitcast_convert_type(bits, F32)


def _log_sigmoid(x):
    return jnp.minimum(x, 0.0) - jnp.log1p(jnp.exp(-jnp.abs(x)))


def _pick_tile(n, preferred):
    for t in preferred:
        if n % t == 0:
            return t
    return n


def _cast_kernel(x_ref, o_ref):
    o_ref[...] = x_ref[...].astype(o_ref.dtype)


def cast_blocks(w, grid, in_block, in_map, out_shape, out_block, out_map):
    return pl.pallas_call(
        _cast_kernel,
        grid=grid,
        in_specs=[pl.BlockSpec(in_block, in_map)],
        out_specs=pl.BlockSpec(out_block, out_map),
        out_shape=jax.ShapeDtypeStruct(out_shape, BF16),
        compiler_params=_params(*(("parallel",) * len(grid))),
        name="cast_blocks",
    )(w)


def cast_rows(w):
    layers, rows, cols = w.shape
    tr = _pick_tile(rows, (512, 256, 128))
    return cast_blocks(w, (layers, rows // tr), (None, tr, cols), lambda l, r: (l, r, 0),
                       w.shape, (None, tr, cols), lambda l, r: (l, r, 0))


def _norm_matmul_kernel(x_ref, g_ref, w_ref, o_ref, *, tn):
    h = _rms(x_ref[...], g_ref[...]).astype(BF16)
    n = o_ref.shape[1]
    for c in range(0, n, tn):
        cols = slice(c, min(c + tn, n))
        o_ref[:, cols] = _dot(h, w_ref[:, cols]).astype(o_ref.dtype)


def norm_matmul(x, gain, w, out_dtype, tn, layer=None):
    m, d = x.shape
    n = w.shape[-1]
    if layer is None:
        w_spec = pl.BlockSpec((d, n), lambda i: (0, 0), pipeline_mode=pl.Buffered(1))
    else:
        w_spec = pl.BlockSpec((None, d, n), lambda i: (layer, 0, 0), pipeline_mode=pl.Buffered(1))
    out_bytes = jnp.dtype(out_dtype).itemsize

    def vmem_need(tm):
        return 2 * tm * d * 4 + d * n * 2 + 2 * tm * n * out_bytes + tm * d * 2 + tm * tn * 4

    tm = next(t for t in (1024, 512, 256, 128) if m % t == 0 and vmem_need(t) <= VMEM_TILE_BUDGET)
    return pl.pallas_call(
        functools.partial(_norm_matmul_kernel, tn=tn),
        grid=(m // tm,),
        in_specs=[
            pl.BlockSpec((tm, d), lambda i: (i, 0)),
            pl.BlockSpec((1, d), lambda i: (0, 0)),
            w_spec,
        ],
        out_specs=pl.BlockSpec((tm, n), lambda i: (i, 0)),
        out_shape=jax.ShapeDtypeStruct((m, n), out_dtype),
        compiler_params=_params("parallel"),
        name="norm_matmul",
    )(x, gain.reshape(1, d), w)


def _mixer_out_kernel(a_ref, q_ref, mk_ref, mv_ref, w_ref, x_ref, o_ref, *, heads):
    ka = a_ref.shape[1]
    main = _dot(a_ref[...], w_ref[0:ka, :])
    dh = q_ref.shape[1] // heads
    scale = dh ** -0.5
    cols = [slice(hd * dh, (hd + 1) * dh) for hd in range(heads)]
    scores = [_dot_nt(q_ref[:, c], mk_ref[:, c]) for c in cols]
    probs, inv_sums = [], []
    for s in scores:
        s = s * scale
        p = jnp.exp(s - jnp.max(s, axis=-1, keepdims=True))
        probs.append(p.astype(BF16))
        inv_sums.append(1.0 / jnp.sum(p, axis=-1, keepdims=True))
    mem = [(_dot(p, mv_ref[:, c]) * r).astype(BF16) for p, r, c in zip(probs, inv_sums, cols)]
    o_ref[...] = x_ref[...] + main + _dot(jnp.concatenate(mem, axis=1), w_ref[ka:, :])


def mixer_out(a, z, q_col_block, mem_kv, layer, w, w_layer, x, seq_len, n_mem):
    m, d = x.shape
    ka = a.shape[1]
    mem_width = MEM_HEADS * LANES
    tm = _pick_tile(seq_len, (512, 256))
    tiles = seq_len // tm
    kern = functools.partial(_mixer_out_kernel, heads=MEM_HEADS)
    return pl.pallas_call(
        kern,
        grid=(m // tm,),
        in_specs=[
            pl.BlockSpec((tm, ka), lambda i: (i, 0)),
            pl.BlockSpec((tm, mem_width), lambda i: (i, q_col_block)),
            pl.BlockSpec((n_mem, mem_width), lambda i: (i // tiles, 2 * layer)),
            pl.BlockSpec((n_mem, mem_width), lambda i: (i // tiles, 2 * layer + 1)),
            pl.BlockSpec((None, ka + mem_width, d), lambda i: (w_layer, 0, 0), pipeline_mode=pl.Buffered(1)),
            pl.BlockSpec((tm, d), lambda i: (i, 0)),
        ],
        out_specs=pl.BlockSpec((tm, d), lambda i: (i, 0)),
        out_shape=jax.ShapeDtypeStruct((m, d), F32),
        compiler_params=_params("parallel"),
        name="mixer_out",
    )(a, z, mem_kv, mem_kv, w, x)


HALO = SUBLANES_BF16
FFN_CHUNK_PARTS = (1, 1)


def _ffn_kernel(x_ref, xh_ref, g_ref, w_ref, taps_ref, wd_ref, gf_ref, o_ref, h_ref, *, tiles_per_seq,
                final_norm):
    i = pl.program_id(0)
    f = pl.program_id(1)
    tm = x_ref.shape[0]
    tf = wd_ref.shape[0]
    hidden = taps_ref.shape[1] // 2

    @pl.when(f == 0)
    def _():
        x = x_ref[...]
        prev = _rms(xh_ref[...], g_ref[...])
        prev = jnp.where(i % tiles_per_seq == 0, 0.0, prev)
        h_ref[0:HALO, :] = prev.astype(BF16)
        h_ref[HALO:HALO + tm, :] = _rms(x, g_ref[...]).astype(BF16)
        o_ref[...] = x

    def conv(u, first_col):
        taps = taps_ref[:, pl.ds(pl.multiple_of(first_col, LANES), tf)]
        y = taps[2:3, :] * u + taps[1:2, :] * pltpu.roll(u, 1, axis=0) + taps[0:1, :] * pltpu.roll(u, 2, axis=0)
        return y[HALO:, :] + taps[3:4, :]

    def up(lo, rc):
        h = h_ref[lo:lo + rc + HALO, :]
        return _dot(h, w_ref[0]), _dot(h, w_ref[1])

    sizes = [c * tm // sum(FFN_CHUNK_PARTS) for c in FFN_CHUNK_PARTS]
    starts = [sum(sizes[:n]) for n in range(len(sizes))]
    u_next = up(starts[0], sizes[0])
    for n, (lo, rc) in enumerate(zip(starts, sizes)):
        ua, uv = u_next
        if n + 1 < len(starts):
            u_next = up(starts[n + 1], sizes[n + 1])
        ya = conv(ua, f * tf)
        yv = conv(uv, hidden + f * tf)
        act = (ya * jax.nn.sigmoid(ya) * yv).astype(BF16)
        o_ref[lo:lo + rc, :] += _dot(act, wd_ref[...])

    if final_norm:
        @pl.when(f == pl.num_programs(1) - 1)
        def _():
            o_ref[...] = _rms(o_ref[...], gf_ref[...])


def ffn_col_tile(hidden):
    return _pick_tile(hidden, (512, 256, 128))


def stage_ffn_up(w_up):
    layers, d, two_hidden = w_up.shape
    tf = ffn_col_tile(two_hidden // 2)
    nf = two_hidden // 2 // tf
    return cast_blocks(w_up, (layers, nf, 2), (None, d, tf), lambda l, f, g: (l, 0, g * nf + f),
                       (layers, nf, 2, d, tf), (None, None, None, d, tf), lambda l, f, g: (l, f, g, 0, 0))


def conv_ffn(x, gain, w_tiles, conv_w, conv_b, w_down, layer, final_gain, seq_len, final_norm):
    m, d = x.shape
    hidden = w_down.shape[1]
    tm = _pick_tile(seq_len, (1024, 512, 256))
    _, nf, _, _, tf = w_tiles.shape
    halo_blocks = tm // HALO
    assert tm % (sum(FFN_CHUNK_PARTS) * SUBLANES_BF16) == 0
    taps = jnp.concatenate([conv_w, conv_b.reshape(1, -1)], axis=0)
    kern = functools.partial(_ffn_kernel, tiles_per_seq=seq_len // tm, final_norm=final_norm)
    return pl.pallas_call(
        kern,
        grid=(m // tm, nf),
        in_specs=[
            pl.BlockSpec((tm, d), lambda i, f: (i, 0)),
            pl.BlockSpec((HALO, d), lambda i, f: (jnp.maximum(i * halo_blocks - 1, 0), 0)),
            pl.BlockSpec((1, d), lambda i, f: (0, 0)),
            pl.BlockSpec((None, None, 2, d, tf), lambda i, f: (layer, f, 0, 0, 0)),
            pl.BlockSpec(taps.shape, lambda i, f: (0, 0)),
            pl.BlockSpec((None, tf, d), lambda i, f: (layer, f, 0)),
            pl.BlockSpec((1, d), lambda i, f: (0, 0)),
        ],
        out_specs=pl.BlockSpec((tm, d), lambda i, f: (i, 0)),
        out_shape=jax.ShapeDtypeStruct((m, d), F32),
        scratch_shapes=[pltpu.VMEM((tm + HALO, d), BF16)],
        compiler_params=_params("parallel", "arbitrary"),
        name="conv_ffn",
    )(x, x, gain.reshape(1, d), w_tiles, taps, w_down, final_gain.reshape(1, d))


GLA_BLOCK = 256


def _ref_rows(bc, half):
    rows, dk = bc.shape
    if half >= SUBLANES_F32:
        nb = rows // (2 * half)
        b3 = bc.reshape(nb, 2 * half, dk)
        return jnp.broadcast_to(b3[:, half:half + 1, :], b3.shape).reshape(rows, dk)
    b3 = bc.reshape(rows // SUBLANES_F32, SUBLANES_F32, dk)
    sub = lax.broadcasted_iota(jnp.int32, b3.shape, 1)
    out = None
    for mid in range(half, SUBLANES_F32, 2 * half):
        cand = jnp.broadcast_to(b3[:, mid:mid + 1, :], b3.shape)
        out = cand if out is None else jnp.where(sub >= mid - half, cand, out)
    return out.reshape(rows, dk)


GLA_HEADS_PER_STEP = 4


def _gla_kernel(q_ref, k_ref, v_ref, glr_ref, og_ref, wg_ref, bg_ref, gn_ref, o_ref, state_ref, *, rank, heads):
    t = pl.program_id(2)
    rows = q_ref.shape[0]
    dk = q_ref.shape[1] // heads
    dv = v_ref.shape[1] // heads
    blk = GLA_BLOCK if rows % GLA_BLOCK == 0 else rows
    n_levels = int(math.log2(blk))

    @pl.when(t == 0)
    def _():
        state_ref[...] = jnp.zeros_like(state_ref)

    ti = lax.broadcasted_iota(jnp.int32, (blk, blk), 0)
    si = lax.broadcasted_iota(jnp.int32, (blk, blk), 1)
    tri = jnp.where(si <= ti, 1.0, 0.0).astype(BF16)
    x = ti ^ si
    top_bit = 31 - lax.clz(jnp.maximum(x, 1))
    pair_level = jnp.where(si < ti, top_bit, jnp.where(si == ti, -1, -2))
    qscale = dk ** -0.5

    def step(i, carry):
        rs = pl.ds(pl.multiple_of(i * blk, blk), blk)
        glr = glr_ref[rs, 0:rank].astype(BF16)
        hs = range(heads)
        kc = [slice(hd * dk, (hd + 1) * dk) for hd in hs]
        vc = [slice(hd * dv, (hd + 1) * dv) for hd in hs]
        q = [q_ref[rs, kc[hd]].astype(F32) * qscale for hd in hs]
        k = [k_ref[rs, kc[hd]].astype(F32) for hd in hs]
        v = [v_ref[rs, vc[hd]].astype(BF16) for hd in hs]
        g = [_log_sigmoid(_dot(glr, wg_ref[hd]) + bg_ref[hd]) * (LOG2_E / GLA_GATE_TAU) for hd in hs]
        gs = [_split2(g[hd]) for hd in hs]
        bc = [_dot(tri, gs[hd][0]) + _dot(tri, gs[hd][1]) for hd in hs]
        b_last = [bc[hd][blk - 1:blk, :] for hd in hs]
        state = [state_ref[hd] for hd in hs]

        o = [_dot_nt((q[hd] * jnp.exp2(bc[hd])).astype(BF16), state[hd].astype(BF16)) for hd in hs]

        att = [jnp.zeros((blk, blk), F32) for hd in hs]
        q_b = [q[hd].astype(BF16) for hd in hs]
        k_b = [k[hd].astype(BF16) for hd in hs]
        for p in range(n_levels):
            e = [jnp.exp2(_neg_abs(bc[hd] - _ref_rows(bc[hd], 1 << p)).astype(BF16)) for hd in hs]
            sp = [_dot_nt(q_b[hd] * e[hd], k_b[hd] * e[hd]) for hd in hs]
            att = [jnp.where(pair_level == p, sp[hd], att[hd]) for hd in hs]
        sd = [_dot_nt(q_b[hd], k_b[hd]) for hd in hs]
        att = [jnp.where(pair_level == -1, sd[hd], att[hd]) for hd in hs]
        o = [o[hd] + _dot(att[hd].astype(BF16), v[hd]) for hd in hs]

        kd = [(k[hd] * jnp.exp2(b_last[hd] - bc[hd])).astype(BF16) for hd in hs]
        upd = [_dot_tn(v[hd], kd[hd]) for hd in hs]
        for hd in hs:
            state_ref[hd] = state[hd] * jnp.exp2(b_last[hd]) + upd[hd]

        for hd in hs:
            on = o[hd] * lax.rsqrt(jnp.mean(o[hd] * o[hd], axis=-1, keepdims=True) + EPS) * gn_ref[hd]
            og = og_ref[rs, vc[hd]].astype(F32)
            o_ref[rs, vc[hd]] = (on * (og * jax.nn.sigmoid(og))).astype(o_ref.dtype)
        return carry

    lax.fori_loop(0, rows // blk, step, 0)


def gla_attention(z, w_gate, b_gate, gn_gain, batch, seq_len, dk, dv, rank):
    m = z.shape[0]
    heads = GLA_HEADS
    hp = GLA_HEADS_PER_STEP
    groups = heads // hp
    t = _pick_tile(seq_len, (1024, 512, 256))
    tiles = seq_len // t
    v0, og0 = 0, groups
    q0 = (2 * heads * dv) // (hp * dk)
    k0 = q0 + groups
    glr0 = (2 * heads * dv + 2 * heads * dk + MEM_HEADS * LANES) // LANES
    assert (2 * heads * dv) % (hp * dk) == 0
    rows = lambda b, h, s: b * tiles + s
    kern = functools.partial(_gla_kernel, rank=rank, heads=hp)
    return pl.pallas_call(
        kern,
        grid=(batch, groups, tiles),
        in_specs=[
            pl.BlockSpec((t, hp * dk), lambda b, h, s: (rows(b, h, s), q0 + h)),
            pl.BlockSpec((t, hp * dk), lambda b, h, s: (rows(b, h, s), k0 + h)),
            pl.BlockSpec((t, hp * dv), lambda b, h, s: (rows(b, h, s), v0 + h)),
            pl.BlockSpec((t, LANES), lambda b, h, s: (rows(b, h, s), glr0)),
            pl.BlockSpec((t, hp * dv), lambda b, h, s: (rows(b, h, s), og0 + h)),
            pl.BlockSpec((hp, rank, dk), lambda b, h, s: (h, 0, 0)),
            pl.BlockSpec((hp, 1, dk), lambda b, h, s: (h, 0, 0)),
            pl.BlockSpec((hp, 1, dv), lambda b, h, s: (h, 0, 0)),
        ],
        out_specs=pl.BlockSpec((t, hp * dv), lambda b, h, s: (rows(b, h, s), h)),
        out_shape=jax.ShapeDtypeStruct((m, heads * dv), BF16),
        scratch_shapes=[pltpu.VMEM((hp, dv, dk), F32)],
        compiler_params=_params("parallel", "parallel", "arbitrary"),
        name="gla_attention",
    )(z, z, z, z, z, w_gate, b_gate, gn_gain)


BIAS_PIECES = 3


def _split3(x):
    hi = x.astype(BF16)
    r1 = x - hi.astype(F32)
    mid = r1.astype(BF16)
    lo = (r1 - mid.astype(F32)).astype(BF16)
    return hi, mid, lo


def _fox_gate_kernel(x_ref, g_ref, w_ref, b_ref, route_ref, o_ref, carry_ref):
    s = pl.program_id(1)
    t = x_ref.shape[0]

    @pl.when(s == 0)
    def _():
        carry_ref[...] = jnp.zeros_like(carry_ref)

    h = _rms(x_ref[...], g_ref[...]).astype(BF16)
    half = h.shape[1] // 2
    logits = _dot(h[:, 0:half], w_ref[0:half, :]) + _dot(h[:, half:], w_ref[half:, :])
    log_f = _log_sigmoid(logits + b_ref[...])
    ti = lax.broadcasted_iota(jnp.int32, (t, t), 0)
    si = lax.broadcasted_iota(jnp.int32, (t, t), 1)
    tri = jnp.where(si <= ti, 1.0, 0.0).astype(BF16)
    sums = _dot(tri, jnp.concatenate(_split3(log_f), axis=1))
    c = carry_ref[...] + sums[:, 0:LANES] + sums[:, LANES:2 * LANES] + sums[:, 2 * LANES:3 * LANES]
    carry_ref[...] = c[t - 1:t, :]
    pieces = jnp.concatenate(_split3(c * (-LOG2_E)), axis=1)
    o_ref[...] = _dot(pieces, route_ref[...]).astype(BF16)


def fox_gates(x, gain, w_f, b_f, batch, seq_len):
    m, d = x.shape
    t = _pick_tile(seq_len, (512, 256))
    tiles = seq_len // t
    src = lax.broadcasted_iota(jnp.int32, (BIAS_PIECES * LANES, FOX_HEADS * LANES), 0)
    dst = lax.broadcasted_iota(jnp.int32, (BIAS_PIECES * LANES, FOX_HEADS * LANES), 1)
    route = ((src // LANES == dst % LANES) & (src % LANES == dst // LANES)).astype(BF16)
    return pl.pallas_call(
        _fox_gate_kernel,
        grid=(batch, tiles),
        in_specs=[
            pl.BlockSpec((t, d), lambda b, s: (b * tiles + s, 0)),
            pl.BlockSpec((1, d), lambda b, s: (0, 0)),
            pl.BlockSpec((d, LANES), lambda b, s: (0, 0)),
            pl.BlockSpec((1, LANES), lambda b, s: (0, 0)),
            pl.BlockSpec(route.shape, lambda b, s: (0, 0)),
        ],
        out_specs=pl.BlockSpec((t, FOX_HEADS * LANES), lambda b, s: (b * tiles + s, 0)),
        out_shape=jax.ShapeDtypeStruct((m, FOX_HEADS * LANES), BF16),
        scratch_shapes=[pltpu.VMEM((1, LANES), F32)],
        compiler_params=_params("parallel", "arbitrary"),
        name="fox_gates",
    )(x, gain.reshape(1, d), w_f, b_f, route)


FOX_HEADS_PER_STEP = 2


def _fox_attn_kernel(q_ref, k_ref, kb_ref, v_ref, o_ref, m_ref, acc_ref, s0_ref, s1_ref, *, heads):
    qi = pl.program_id(2)
    tq = q_ref.shape[0]
    tk = tq // 2
    dh = q_ref.shape[1] // heads
    cols = [slice(hd * dh, (hd + 1) * dh) for hd in range(heads)]
    lane = lax.broadcasted_iota(jnp.int32, (tq, dh), 1)
    bias_taps = jnp.where(lane < BIAS_PIECES, 1.0, 0.0).astype(BF16)
    ones_col = jnp.where(lax.broadcasted_iota(jnp.int32, (tk, dh), 1) == 0, 1.0, 0.0).astype(BF16)
    qs = [jnp.concatenate([(q_ref[:, cols[hd]].astype(F32) * (dh ** -0.5 * LOG2_E)).astype(BF16), bias_taps],
                          axis=1) for hd in range(heads)]

    def key_rows(j):
        return pl.ds(pl.multiple_of(j * tk, tk), tk)

    def issue_scores(s_ref, j, rows):
        ks = key_rows(j)
        for hd in range(heads):
            keys = jnp.concatenate([k_ref[ks, cols[hd]], kb_ref[ks, cols[hd]]], axis=1)
            s_ref[hd, rows, :] = _dot_nt(qs[hd][rows], keys)

    def update(hd, s, j, rows):
        m_prev = m_ref[hd, rows, :]
        m_new = jnp.maximum(m_prev, jnp.max(s, axis=-1, keepdims=True))
        alpha = jnp.exp2(m_prev - m_new)
        p = jnp.concatenate([jnp.exp2((s[:, c:c + LANES] - m_new).astype(BF16))
                             for c in range(0, tk, LANES)], axis=1)
        pv = _dot(p, jnp.concatenate([v_ref[key_rows(j), cols[hd]], ones_col], axis=1))
        acc = acc_ref[hd, rows, :]
        acc_ref[hd, rows, :] = jnp.concatenate([alpha * acc[:, c:c + LANES] for c in range(0, 2 * dh, LANES)],
                                               axis=1) + pv
        m_ref[hd, rows, :] = m_new

    every = slice(0, tq)
    lower = slice(tk, tq)

    m_ref[...] = jnp.full(m_ref.shape, -jnp.inf, F32)
    acc_ref[...] = jnp.zeros_like(acc_ref)
    issue_scores(s0_ref, 0, every)

    def pair(jj, carry):
        j = 2 * jj
        issue_scores(s1_ref, j + 1, every)
        for hd in range(heads):
            update(hd, s0_ref[hd], j, every)
        issue_scores(s0_ref, j + 2, every)
        for hd in range(heads):
            update(hd, s1_ref[hd], j + 1, every)
        return carry

    lax.fori_loop(0, qi, pair, 0)

    issue_scores(s1_ref, 2 * qi + 1, lower)
    ti = lax.broadcasted_iota(jnp.int32, (tq, tk), 0)
    si = lax.broadcasted_iota(jnp.int32, (tq, tk), 1)
    for hd in range(heads):
        update(hd, jnp.where(si <= ti, s0_ref[hd], -jnp.inf), 2 * qi, every)
    tl = lax.broadcasted_iota(jnp.int32, (tk, tk), 0)
    sl = lax.broadcasted_iota(jnp.int32, (tk, tk), 1)
    for hd in range(heads):
        update(hd, jnp.where(sl <= tl, s1_ref[hd, lower, :], -jnp.inf), 2 * qi + 1, lower)
    for hd in range(heads):
        acc = acc_ref[hd]
        o_ref[:, cols[hd]] = (acc[:, 0:dh] / acc[:, dh:dh + 1]).astype(o_ref.dtype)


def fox_attention(zq, kv, kbias, batch, seq_len):
    m = zq.shape[0]
    hp = FOX_HEADS_PER_STEP
    width = hp * LANES
    groups = FOX_HEADS // hp
    tq = _pick_tile(seq_len, (1024, 512, 256))
    tiles = seq_len // tq
    kern = functools.partial(_fox_attn_kernel, heads=hp)
    return pl.pallas_call(
        kern,
        grid=(batch, groups, tiles),
        in_specs=[
            pl.BlockSpec((tq, width), lambda b, h, s: (b * tiles + s, h)),
            pl.BlockSpec((seq_len, width), lambda b, h, s: (b, h)),
            pl.BlockSpec((seq_len, width), lambda b, h, s: (b, h)),
            pl.BlockSpec((seq_len, width), lambda b, h, s: (b, groups + h)),
        ],
        out_specs=pl.BlockSpec((tq, width), lambda b, h, s: (b * tiles + s, h)),
        out_shape=jax.ShapeDtypeStruct((m, FOX_HEADS * LANES), BF16),
        scratch_shapes=[pltpu.VMEM((hp, tq, LANES), F32), pltpu.VMEM((hp, tq, 2 * LANES), F32),
                        pltpu.VMEM((hp, tq, tq // 2), F32), pltpu.VMEM((hp, tq, tq // 2), F32)],
        compiler_params=_params("parallel", "parallel", "arbitrary"),
        name="fox_attention",
    )(zq, kv, kbias, kv)


def kernel(x, mem, norm_mix, norm_ffn, norm_mem, norm_final, mem_w_kv, gla_w_in, gla_w_gate_up,
           gla_b_gate, gla_norm, gla_w_out, fox_kv_norm, fox_w_kv, fox_b_f, fox_w_in, fox_w_out,
           ffn_w_up, ffn_conv_w, ffn_conv_b, ffn_w_down):
    batch, seq_len, d = x.shape
    n_mem = mem.shape[1]
    depth = norm_mix.shape[0]
    n_gla = gla_w_in.shape[0]
    rank = gla_w_gate_up.shape[1]
    qk_width = gla_w_gate_up.shape[2]
    dk = qk_width // GLA_HEADS
    v_width = gla_norm.shape[1]
    dv = v_width // GLA_HEADS
    mem_width = mem_w_kv.shape[2] // 2
    fox_width = (fox_w_kv.shape[1] - FOX_HEADS) // 2
    assert mem_width == MEM_HEADS * LANES and fox_width == FOX_HEADS * LANES
    assert (2 * v_width) % dk == 0 and rank <= LANES

    xf = x.reshape(batch * seq_len, d)

    tr = _pick_tile(d, (512, 256, 128))
    w_mem = cast_blocks(mem_w_kv, (depth, d // tr), (None, tr, 2 * mem_width), lambda l, r: (l, r, 0),
                        (d, depth * 2 * mem_width), (tr, 2 * mem_width), lambda l, r: (r, l))
    ffn_up = stage_ffn_up(ffn_w_up)
    ffn_down = cast_rows(ffn_w_down)
    gla_out, fox_in, fox_out = cast_rows(gla_w_out), cast_rows(fox_w_in), cast_rows(fox_w_out)
    mem_kv = norm_matmul(mem.reshape(batch * n_mem, d), norm_mem, w_mem, BF16,
                         _pick_tile(w_mem.shape[1], (1024, 512)))

    fox_kv = fox_c = None
    for i in range(depth):
        if i == n_gla:
            w_kv = cast_blocks(fox_w_kv, (d // tr,), (tr, 2 * fox_width), lambda r: (r, 0),
                               (d, 2 * fox_width), (tr, 2 * fox_width), lambda r: (r, 0))
            fox_kv = norm_matmul(xf, fox_kv_norm, w_kv, BF16,
                                 _pick_tile(2 * fox_width, (1024, 768, 512)))
            w_f = jnp.pad(fox_w_kv[:, 2 * fox_width:], ((0, 0), (0, LANES - FOX_HEADS))).astype(BF16)
            b_f = jnp.pad(fox_b_f, (0, LANES - FOX_HEADS)).reshape(1, LANES)
            fox_c = fox_gates(xf, fox_kv_norm, w_f, b_f, batch, seq_len)
        if i < n_gla:
            w = gla_w_in[i]
            o_q, o_k, o_v = 0, qk_width, 2 * qk_width
            o_glr = o_v + v_width
            o_og = o_glr + rank
            o_mq = o_og + v_width
            w_in = jnp.concatenate([
                w[:, o_v:o_glr], w[:, o_og:o_mq], w[:, o_q:o_k], w[:, o_k:o_v], w[:, o_mq:],
                jnp.pad(w[:, o_glr:o_og], ((0, 0), (0, LANES - rank)))], axis=1).astype(BF16)
            z = norm_matmul(xf, norm_mix[i], w_in, BF16, 8 * MXU_TILE)
            a = gla_attention(z, gla_w_gate_up[i].reshape(rank, GLA_HEADS, dk).transpose(1, 0, 2).astype(BF16),
                              gla_b_gate[i].reshape(GLA_HEADS, 1, dk), gla_norm[i].reshape(GLA_HEADS, 1, dv),
                              batch, seq_len, dk, dv, rank)
            mq_block = (2 * v_width + 2 * qk_width) // mem_width
            w_out, w_layer = gla_out, i
        else:
            j = i - n_gla
            z = norm_matmul(xf, norm_mix[i], fox_in, BF16, _pick_tile(fox_w_in.shape[2], (1024, 512)), layer=j)
            a = fox_attention(z, fox_kv, fox_c, batch, seq_len)
            mq_block = fox_width // mem_width
            w_out, w_layer = fox_out, j
        xf = mixer_out(a, z, mq_block, mem_kv, i, w_out, w_layer, xf, seq_len, n_mem)
        xf = conv_ffn(xf, norm_ffn[i], ffn_up, ffn_conv_w[i], ffn_conv_b[i], ffn_down, i, norm_final,
                      seq_len, final_norm=(i == depth - 1))
    return xf.reshape(batch, seq_len, d)
```

```python
import functools
import math

import jax
import jax.numpy as jnp
from jax import lax
from jax.experimental import pallas as pl
from jax.experimental.pallas import tpu as pltpu

GLA_HEADS = 4
GLA_GATE_TAU = 16.0
FOX_HEADS = 12
MEM_HEADS = 4
CONV_WIDTH = 3
EPS = 1e-6
LOG2_E = math.log2(math.e)

LANES = 128
SUBLANES_F32 = 8
SUBLANES_BF16 = 16
MXU_TILE = 256
VMEM_LIMIT = 56 * 1024 * 1024
VMEM_TILE_BUDGET = 0.9 * VMEM_LIMIT

BF16 = jnp.bfloat16
F32 = jnp.float32

NT_DIMS = (((1,), (1,)), ((), ()))
TN_DIMS = (((0,), (0,)), ((), ()))


def _params(*sem):
    return pltpu.CompilerParams(dimension_semantics=sem, vmem_limit_bytes=VMEM_LIMIT)


def _rms(x, gain):
    return x * lax.rsqrt(jnp.mean(x * x, axis=-1, keepdims=True) + EPS) * gain


def _dot(a, b):
    return jnp.dot(a, b, preferred_element_type=F32)


def _dot_nt(a, b):
    return lax.dot_general(a, b, NT_DIMS, preferred_element_type=F32)


def _dot_tn(a, b):
    return lax.dot_general(a, b, TN_DIMS, preferred_element_type=F32)


def _split2(x):
    hi = x.astype(BF16)
    lo = (x - hi.astype(F32)).astype(BF16)
    return hi, lo


def _neg_abs(x):
    bits = lax.bitcast_convert_type(x, jnp.uint32) | jnp.uint32(0x80000000)
    return lax.bitcast_convert_type(bits, F32)


def _log_sigmoid(x):
    return jnp.minimum(x, 0.0) - jnp.log1p(jnp.exp(-jnp.abs(x)))


def _pick_tile(n, preferred):
    for t in preferred:
        if n % t == 0:
            return t
    return n


def _cast_kernel(x_ref, o_ref):
    o_ref[...] = x_ref[...].astype(o_ref.dtype)


def cast_blocks(w, grid, in_block, in_map, out_shape, out_block, out_map):
    return pl.pallas_call(
        _cast_kernel,
        grid=grid,
        in_specs=[pl.BlockSpec(in_block, in_map)],
        out_specs=pl.BlockSpec(out_block, out_map),
        out_shape=jax.ShapeDtypeStruct(out_shape, BF16),
        compiler_params=_params(*(("parallel",) * len(grid))),
        name="cast_blocks",
    )(w)


def cast_rows(w):
    layers, rows, cols = w.shape
    tr = _pick_tile(rows, (512, 256, 128))
    return cast_blocks(w, (layers, rows // tr), (None, tr, cols), lambda l, r: (l, r, 0),
                       w.shape, (None, tr, cols), lambda l, r: (l, r, 0))


def _norm_matmul_kernel(x_ref, g_ref, w_ref, o_ref, *, tn):
    h = _rms(x_ref[...], g_ref[...]).astype(BF16)
    n = o_ref.shape[1]
    for c in range(0, n, tn):
        cols = slice(c, min(c + tn, n))
        o_ref[:, cols] = _dot(h, w_ref[:, cols]).astype(o_ref.dtype)


def norm_matmul(x, gain, w, out_dtype, tn, layer=None):
    m, d = x.shape
    n = w.shape[-1]
    if layer is None:
        w_spec = pl.BlockSpec((d, n), lambda i: (0, 0), pipeline_mode=pl.Buffered(1))
    else:
        w_spec = pl.BlockSpec((None, d, n), lambda i: (layer, 0, 0), pipeline_mode=pl.Buffered(1))
    out_bytes = jnp.dtype(out_dtype).itemsize

    def vmem_need(tm):
        return 2 * tm * d * 4 + d * n * 2 + 2 * tm * n * out_bytes + tm * d * 2 + tm * tn * 4

    tm = next(t for t in (1024, 512, 256, 128) if m % t == 0 and vmem_need(t) <= VMEM_TILE_BUDGET)
    return pl.pallas_call(
        functools.partial(_norm_matmul_kernel, tn=tn),
        grid=(m // tm,),
        in_specs=[
            pl.BlockSpec((tm, d), lambda i: (i, 0)),
            pl.BlockSpec((1, d), lambda i: (0, 0)),
            w_spec,
        ],
        out_specs=pl.BlockSpec((tm, n), lambda i: (i, 0)),
        out_shape=jax.ShapeDtypeStruct((m, n), out_dtype),
        compiler_params=_params("parallel"),
        name="norm_matmul",
    )(x, gain.reshape(1, d), w)


def _mixer_out_kernel(a_ref, q_ref, mk_ref, mv_ref, w_ref, x_ref, o_ref, *, heads):
    ka = a_ref.shape[1]
    main = _dot(a_ref[...], w_ref[0:ka, :])
    dh = q_ref.shape[1] // heads
    scale = dh ** -0.5
    cols = [slice(hd * dh, (hd + 1) * dh) for hd in range(heads)]
    scores = [_dot_nt(q_ref[:, c], mk_ref[:, c]) for c in cols]
    probs, inv_sums = [], []
    for s in scores:
        s = s * scale
        p = jnp.exp(s - jnp.max(s, axis=-1, keepdims=True))
        probs.append(p.astype(BF16))
        inv_sums.append(1.0 / jnp.sum(p, axis=-1, keepdims=True))
    mem = [(_dot(p, mv_ref[:, c]) * r).astype(BF16) for p, r, c in zip(probs, inv_sums, cols)]
    o_ref[...] = x_ref[...] + main + _dot(jnp.concatenate(mem, axis=1), w_ref[ka:, :])


def mixer_out(a, z, q_col_block, mem_kv, layer, w, w_layer, x, seq_len, n_mem):
    m, d = x.shape
    ka = a.shape[1]
    mem_width = MEM_HEADS * LANES
    tm = _pick_tile(seq_len, (512, 256))
    tiles = seq_len // tm
    kern = functools.partial(_mixer_out_kernel, heads=MEM_HEADS)
    return pl.pallas_call(
        kern,
        grid=(m // tm,),
        in_specs=[
            pl.BlockSpec((tm, ka), lambda i: (i, 0)),
            pl.BlockSpec((tm, mem_width), lambda i: (i, q_col_block)),
            pl.BlockSpec((n_mem, mem_width), lambda i: (i // tiles, 2 * layer)),
            pl.BlockSpec((n_mem, mem_width), lambda i: (i // tiles, 2 * layer + 1)),
            pl.BlockSpec((None, ka + mem_width, d), lambda i: (w_layer, 0, 0), pipeline_mode=pl.Buffered(1)),
            pl.BlockSpec((tm, d), lambda i: (i, 0)),
        ],
        out_specs=pl.BlockSpec((tm, d), lambda i: (i, 0)),
        out_shape=jax.ShapeDtypeStruct((m, d), F32),
        compiler_params=_params("parallel"),
        name="mixer_out",
    )(a, z, mem_kv, mem_kv, w, x)


HALO = SUBLANES_BF16
FFN_CHUNK_PARTS = (1, 1)


def _ffn_kernel(x_ref, xh_ref, g_ref, w_ref, taps_ref, wd_ref, gf_ref, o_ref, h_ref, *, tiles_per_seq,
                final_norm):
    i = pl.program_id(0)
    f = pl.program_id(1)
    tm = x_ref.shape[0]
    tf = wd_ref.shape[0]
    hidden = taps_ref.shape[1] // 2

    @pl.when(f == 0)
    def _():
        x = x_ref[...]
        prev = _rms(xh_ref[...], g_ref[...])
        prev = jnp.where(i % tiles_per_seq == 0, 0.0, prev)
        h_ref[0:HALO, :] = prev.astype(BF16)
        h_ref[HALO:HALO + tm, :] = _rms(x, g_ref[...]).astype(BF16)
        o_ref[...] = x

    def conv(u, first_col):
        taps = taps_ref[:, pl.ds(pl.multiple_of(first_col, LANES), tf)]
        y = taps[2:3, :] * u + taps[1:2, :] * pltpu.roll(u, 1, axis=0) + taps[0:1, :] * pltpu.roll(u, 2, axis=0)
        return y[HALO:, :] + taps[3:4, :]

    def up(lo, rc):
        h = h_ref[lo:lo + rc + HALO, :]
        return _dot(h, w_ref[0]), _dot(h, w_ref[1])

    sizes = [c * tm // sum(FFN_CHUNK_PARTS) for c in FFN_CHUNK_PARTS]
    starts = [sum(sizes[:n]) for n in range(len(sizes))]
    u_next = up(starts[0], sizes[0])
    for n, (lo, rc) in enumerate(zip(starts, sizes)):
        ua, uv = u_next
        if n + 1 < len(starts):
            u_next = up(starts[n + 1], sizes[n + 1])
        ya = conv(ua, f * tf)
        yv = conv(uv, hidden + f * tf)
        act = (ya * jax.nn.sigmoid(ya) * yv).astype(BF16)
        o_ref[lo:lo + rc, :] += _dot(act, wd_ref[...])

    if final_norm:
        @pl.when(f == pl.num_programs(1) - 1)
        def _():
            o_ref[...] = _rms(o_ref[...], gf_ref[...])


def ffn_col_tile(hidden):
    return _pick_tile(hidden, (512, 256, 128))


def stage_ffn_up(w_up):
    layers, d, two_hidden = w_up.shape
    tf = ffn_col_tile(two_hidden // 2)
    nf = two_hidden // 2 // tf
    return cast_blocks(w_up, (layers, nf, 2), (None, d, tf), lambda l, f, g: (l, 0, g * nf + f),
                       (layers, nf, 2, d, tf), (None, None, None, d, tf), lambda l, f, g: (l, f, g, 0, 0))


def conv_ffn(x, gain, w_tiles, conv_w, conv_b, w_down, layer, final_gain, seq_len, final_norm):
    m, d = x.shape
    hidden = w_down.shape[1]
    tm = _pick_tile(seq_len, (1024, 512, 256))
    _, nf, _, _, tf = w_tiles.shape
    halo_blocks = tm // HALO
    assert tm % (sum(FFN_CHUNK_PARTS) * SUBLANES_BF16) == 0
    taps = jnp.concatenate([conv_w, conv_b.reshape(1, -1)], axis=0)
    kern = functools.partial(_ffn_kernel, tiles_per_seq=seq_len // tm, final_norm=final_norm)
    return pl.pallas_call(
        kern,
        grid=(m // tm, nf),
        in_specs=[
            pl.BlockSpec((tm, d), lambda i, f: (i, 0)),
            pl.BlockSpec((HALO, d), lambda i, f: (jnp.maximum(i * halo_blocks - 1, 0), 0)),
            pl.BlockSpec((1, d), lambda i, f: (0, 0)),
            pl.BlockSpec((None, None, 2, d, tf), lambda i, f: (layer, f, 0, 0, 0)),
            pl.BlockSpec(taps.shape, lambda i, f: (0, 0)),
            pl.BlockSpec((None, tf, d), lambda i, f: (layer, f, 0)),
            pl.BlockSpec((1, d), lambda i, f: (0, 0)),
        ],
        out_specs=pl.BlockSpec((tm, d), lambda i, f: (i, 0)),
        out_shape=jax.ShapeDtypeStruct((m, d), F32),
        scratch_shapes=[pltpu.VMEM((tm + HALO, d), BF16)],
        compiler_params=_params("parallel", "arbitrary"),
        name="conv_ffn",
    )(x, x, gain.reshape(1, d), w_tiles, taps, w_down, final_gain.reshape(1, d))


GLA_BLOCK = 128


def _ref_rows(bc, half):
    rows, dk = bc.shape
    if half >= SUBLANES_F32:
        nb = rows // (2 * half)
        b3 = bc.reshape(nb, 2 * half, dk)
        return jnp.broadcast_to(b3[:, half:half + 1, :], b3.shape).reshape(rows, dk)
    b3 = bc.reshape(rows // SUBLANES_F32, SUBLANES_F32, dk)
    sub = lax.broadcasted_iota(jnp.int32, b3.shape, 1)
    out = None
    for mid in range(half, SUBLANES_F32, 2 * half):
        cand = jnp.broadcast_to(b3[:, mid:mid + 1, :], b3.shape)
        out = cand if out is None else jnp.where(sub >= mid - half, cand, out)
    return out.reshape(rows, dk)


GLA_HEADS_PER_STEP = 4


def _gla_kernel(q_ref, k_ref, v_ref, glr_ref, og_ref, wg_ref, bg_ref, gn_ref, o_ref, state_ref, *, rank, heads):
    t = pl.program_id(2)
    rows = q_ref.shape[0]
    dk = q_ref.shape[1] // heads
    dv = v_ref.shape[1] // heads
    blk = GLA_BLOCK if rows % GLA_BLOCK == 0 else rows
    n_levels = int(math.log2(blk))

    @pl.when(t == 0)
    def _():
        state_ref[...] = jnp.zeros_like(state_ref)

    ti = lax.broadcasted_iota(jnp.int32, (blk, blk), 0)
    si = lax.broadcasted_iota(jnp.int32, (blk, blk), 1)
    tri = jnp.where(si <= ti, 1.0, 0.0).astype(BF16)
    x = ti ^ si
    top_bit = 31 - lax.clz(jnp.maximum(x, 1))
    pair_level = jnp.where(si < ti, top_bit, jnp.where(si == ti, -1, -2))
    qscale = dk ** -0.5

    def step(i, carry):
        rs = pl.ds(pl.multiple_of(i * blk, blk), blk)
        glr = glr_ref[rs, 0:rank].astype(BF16)
        hs = range(heads)
        kc = [slice(hd * dk, (hd + 1) * dk) for hd in hs]
        vc = [slice(hd * dv, (hd + 1) * dv) for hd in hs]
        q = [q_ref[rs, kc[hd]].astype(F32) * qscale for hd in hs]
        k = [k_ref[rs, kc[hd]].astype(F32) for hd in hs]
        v = [v_ref[rs, vc[hd]].astype(BF16) for hd in hs]
        g = [_log_sigmoid(_dot(glr, wg_ref[hd]) + bg_ref[hd]) * (LOG2_E / GLA_GATE_TAU) for hd in hs]
        gs = [_split2(g[hd]) for hd in hs]
        bc = [_dot(tri, gs[hd][0]) + _dot(tri, gs[hd][1]) for hd in hs]
        b_last = [bc[hd][blk - 1:blk, :] for hd in hs]
        state = [state_ref[hd] for hd in hs]

        o = [_dot_nt((q[hd] * jnp.exp2(bc[hd])).astype(BF16), state[hd].astype(BF16)) for hd in hs]

        att = [jnp.zeros((blk, blk), F32) for hd in hs]
        q_b = [q[hd].astype(BF16) for hd in hs]
        k_b = [k[hd].astype(BF16) for hd in hs]
        for p in range(n_levels):
            e = [jnp.exp2(_neg_abs(bc[hd] - _ref_rows(bc[hd], 1 << p)).astype(BF16)) for hd in hs]
            sp = [_dot_nt(q_b[hd] * e[hd], k_b[hd] * e[hd]) for hd in hs]
            att = [jnp.where(pair_level == p, sp[hd], att[hd]) for hd in hs]
        sd = [_dot_nt(q_b[hd], k_b[hd]) for hd in hs]
        att = [jnp.where(pair_level == -1, sd[hd], att[hd]) for hd in hs]
        o = [o[hd] + _dot(att[hd].astype(BF16), v[hd]) for hd in hs]

        kd = [(k[hd] * jnp.exp2(b_last[hd] - bc[hd])).astype(BF16) for hd in hs]
        upd = [_dot_tn(v[hd], kd[hd]) for hd in hs]
        for hd in hs:
            state_ref[hd] = state[hd] * jnp.exp2(b_last[hd]) + upd[hd]

        for hd in hs:
            on = o[hd] * lax.rsqrt(jnp.mean(o[hd] * o[hd], axis=-1, keepdims=True) + EPS) * gn_ref[hd]
            og = og_ref[rs, vc[hd]].astype(F32)
            o_ref[rs, vc[hd]] = (on * (og * jax.nn.sigmoid(og))).astype(o_ref.dtype)
        return carry

    lax.fori_loop(0, rows // blk, step, 0)


def gla_attention(z, w_gate, b_gate, gn_gain, batch, seq_len, dk, dv, rank):
    m = z.shape[0]
    heads = GLA_HEADS
    hp = GLA_HEADS_PER_STEP
    groups = heads // hp
    t = _pick_tile(seq_len, (1024, 512, 256))
    tiles = seq_len // t
    v0, og0 = 0, groups
    q0 = (2 * heads * dv) // (hp * dk)
    k0 = q0 + groups
    glr0 = (2 * heads * dv + 2 * heads * dk + MEM_HEADS * LANES) // LANES
    assert (2 * heads * dv) % (hp * dk) == 0
    rows = lambda b, h, s: b * tiles + s
    kern = functools.partial(_gla_kernel, rank=rank, heads=hp)
    return pl.pallas_call(
        kern,
        grid=(batch, groups, tiles),
        in_specs=[
            pl.BlockSpec((t, hp * dk), lambda b, h, s: (rows(b, h, s), q0 + h)),
            pl.BlockSpec((t, hp * dk), lambda b, h, s: (rows(b, h, s), k0 + h)),
            pl.BlockSpec((t, hp * dv), lambda b, h, s: (rows(b, h, s), v0 + h)),
            pl.BlockSpec((t, LANES), lambda b, h, s: (rows(b, h, s), glr0)),
            pl.BlockSpec((t, hp * dv), lambda b, h, s: (rows(b, h, s), og0 + h)),
            pl.BlockSpec((hp, rank, dk), lambda b, h, s: (h, 0, 0)),
            pl.BlockSpec((hp, 1, dk), lambda b, h, s: (h, 0, 0)),
            pl.BlockSpec((hp, 1, dv), lambda b, h, s: (h, 0, 0)),
        ],
        out_specs=pl.BlockSpec((t, hp * dv), lambda b, h, s: (rows(b, h, s), h)),
        out_shape=jax.ShapeDtypeStruct((m, heads * dv), BF16),
        scratch_shapes=[pltpu.VMEM((hp, dv, dk), F32)],
        compiler_params=_params("parallel", "parallel", "arbitrary"),
        name="gla_attention",
    )(z, z, z, z, z, w_gate, b_gate, gn_gain)


BIAS_PIECES = 3


def _split3(x):
    hi = x.astype(BF16)
    r1 = x - hi.astype(F32)
    mid = r1.astype(BF16)
    lo = (r1 - mid.astype(F32)).astype(BF16)
    return hi, mid, lo


def _fox_gate_kernel(x_ref, g_ref, w_ref, b_ref, route_ref, o_ref, carry_ref):
    s = pl.program_id(1)
    t = x_ref.shape[0]

    @pl.when(s == 0)
    def _():
        carry_ref[...] = jnp.zeros_like(carry_ref)

    h = _rms(x_ref[...], g_ref[...]).astype(BF16)
    half = h.shape[1] // 2
    logits = _dot(h[:, 0:half], w_ref[0:half, :]) + _dot(h[:, half:], w_ref[half:, :])
    log_f = _log_sigmoid(logits + b_ref[...])
    ti = lax.broadcasted_iota(jnp.int32, (t, t), 0)
    si = lax.broadcasted_iota(jnp.int32, (t, t), 1)
    tri = jnp.where(si <= ti, 1.0, 0.0).astype(BF16)
    sums = _dot(tri, jnp.concatenate(_split3(log_f), axis=1))
    c = carry_ref[...] + sums[:, 0:LANES] + sums[:, LANES:2 * LANES] + sums[:, 2 * LANES:3 * LANES]
    carry_ref[...] = c[t - 1:t, :]
    pieces = jnp.concatenate(_split3(c * (-LOG2_E)), axis=1)
    o_ref[...] = _dot(pieces, route_ref[...]).astype(BF16)


def fox_gates(x, gain, w_f, b_f, batch, seq_len):
    m, d = x.shape
    t = _pick_tile(seq_len, (512, 256))
    tiles = seq_len // t
    src = lax.broadcasted_iota(jnp.int32, (BIAS_PIECES * LANES, FOX_HEADS * LANES), 0)
    dst = lax.broadcasted_iota(jnp.int32, (BIAS_PIECES * LANES, FOX_HEADS * LANES), 1)
    route = ((src // LANES == dst % LANES) & (src % LANES == dst // LANES)).astype(BF16)
    return pl.pallas_call(
        _fox_gate_kernel,
        grid=(batch, tiles),
        in_specs=[
            pl.BlockSpec((t, d), lambda b, s: (b * tiles + s, 0)),
            pl.BlockSpec((1, d), lambda b, s: (0, 0)),
            pl.BlockSpec((d, LANES), lambda b, s: (0, 0)),
            pl.BlockSpec((1, LANES), lambda b, s: (0, 0)),
            pl.BlockSpec(route.shape, lambda b, s: (0, 0)),
        ],
        out_specs=pl.BlockSpec((t, FOX_HEADS * LANES), lambda b, s: (b * tiles + s, 0)),
        out_shape=jax.ShapeDtypeStruct((m, FOX_HEADS * LANES), BF16),
        scratch_shapes=[pltpu.VMEM((1, LANES), F32)],
        compiler_params=_params("parallel", "arbitrary"),
        name="fox_gates",
    )(x, gain.reshape(1, d), w_f, b_f, route)


FOX_HEADS_PER_STEP = 2


def _fox_attn_kernel(q_ref, k_ref, kb_ref, v_ref, o_ref, m_ref, acc_ref, s0_ref, s1_ref, *, heads):
    qi = pl.program_id(2)
    tq = q_ref.shape[0]
    tk = tq // 2
    dh = q_ref.shape[1] // heads
    cols = [slice(hd * dh, (hd + 1) * dh) for hd in range(heads)]
    lane = lax.broadcasted_iota(jnp.int32, (tq, dh), 1)
    bias_taps = jnp.where(lane < BIAS_PIECES, 1.0, 0.0).astype(BF16)
    ones_col = jnp.where(lax.broadcasted_iota(jnp.int32, (tk, dh), 1) == 0, 1.0, 0.0).astype(BF16)
    qs = [jnp.concatenate([(q_ref[:, cols[hd]].astype(F32) * (dh ** -0.5 * LOG2_E)).astype(BF16), bias_taps],
                          axis=1) for hd in range(heads)]

    def key_rows(j):
        return pl.ds(pl.multiple_of(j * tk, tk), tk)

    def issue_scores(s_ref, j, rows):
        ks = key_rows(j)
        for hd in range(heads):
            keys = jnp.concatenate([k_ref[ks, cols[hd]], kb_ref[ks, cols[hd]]], axis=1)
            s_ref[hd, rows, :] = _dot_nt(qs[hd][rows], keys)

    def update(hd, s, j, rows):
        m_prev = m_ref[hd, rows, :]
        m_new = jnp.maximum(m_prev, jnp.max(s, axis=-1, keepdims=True))
        alpha = jnp.exp2(m_prev - m_new)
        p = jnp.concatenate([jnp.exp2((s[:, c:c + LANES] - m_new).astype(BF16))
                             for c in range(0, tk, LANES)], axis=1)
        pv = _dot(p, jnp.concatenate([v_ref[key_rows(j), cols[hd]], ones_col], axis=1))
        acc = acc_ref[hd, rows, :]
        acc_ref[hd, rows, :] = jnp.concatenate([alpha * acc[:, c:c + LANES] for c in range(0, 2 * dh, LANES)],
                                               axis=1) + pv
        m_ref[hd, rows, :] = m_new

    every = slice(0, tq)
    lower = slice(tk, tq)

    m_ref[...] = jnp.full(m_ref.shape, -jnp.inf, F32)
    acc_ref[...] = jnp.zeros_like(acc_ref)
    issue_scores(s0_ref, 0, every)

    def pair(jj, carry):
        j = 2 * jj
        issue_scores(s1_ref, j + 1, every)
        for hd in range(heads):
            update(hd, s0_ref[hd], j, every)
        issue_scores(s0_ref, j + 2, every)
        for hd in range(heads):
            update(hd, s1_ref[hd], j + 1, every)
        return carry

    lax.fori_loop(0, qi, pair, 0)

    issue_scores(s1_ref, 2 * qi + 1, lower)
    ti = lax.broadcasted_iota(jnp.int32, (tq, tk), 0)
    si = lax.broadcasted_iota(jnp.int32, (tq, tk), 1)
    for hd in range(heads):
        update(hd, jnp.where(si <= ti, s0_ref[hd], -jnp.inf), 2 * qi, every)
    tl = lax.broadcasted_iota(jnp.int32, (tk, tk), 0)
    sl = lax.broadcasted_iota(jnp.int32, (tk, tk), 1)
    for hd in range(heads):
        update(hd, jnp.where(sl <= tl, s1_ref[hd, lower, :], -jnp.inf), 2 * qi + 1, lower)
    for hd in range(heads):
        acc = acc_ref[hd]
        o_ref[:, cols[hd]] = (acc[:, 0:dh] / acc[:, dh:dh + 1]).astype(o_ref.dtype)


def fox_attention(zq, kv, kbias, batch, seq_len):
    m = zq.shape[0]
    hp = FOX_HEADS_PER_STEP
    width = hp * LANES
    groups = FOX_HEADS // hp
    tq = _pick_tile(seq_len, (1024, 512, 256))
    tiles = seq_len // tq
    kern = functools.partial(_fox_attn_kernel, heads=hp)
    return pl.pallas_call(
        kern,
        grid=(batch, groups, tiles),
        in_specs=[
            pl.BlockSpec((tq, width), lambda b, h, s: (b * tiles + s, h)),
            pl.BlockSpec((seq_len, width), lambda b, h, s: (b, h)),
            pl.BlockSpec((seq_len, width), lambda b, h, s: (b, h)),
            pl.BlockSpec((seq_len, width), lambda b, h, s: (b, groups + h)),
        ],
        out_specs=pl.BlockSpec((tq, width), lambda b, h, s: (b * tiles + s, h)),
        out_shape=jax.ShapeDtypeStruct((m, FOX_HEADS * LANES), BF16),
        scratch_shapes=[pltpu.VMEM((hp, tq, LANES), F32), pltpu.VMEM((hp, tq, 2 * LANES), F32),
                        pltpu.VMEM((hp, tq, tq // 2), F32), pltpu.VMEM((hp, tq, tq // 2), F32)],
        compiler_params=_params("parallel", "parallel", "arbitrary"),
        name="fox_attention",
    )(zq, kv, kbias, kv)


def kernel(x, mem, norm_mix, norm_ffn, norm_mem, norm_final, mem_w_kv, gla_w_in, gla_w_gate_up,
           gla_b_gate, gla_norm, gla_w_out, fox_kv_norm, fox_w_kv, fox_b_f, fox_w_in, fox_w_out,
           ffn_w_up, ffn_conv_w, ffn_conv_b, ffn_w_down):
    batch, seq_len, d = x.shape
    n_mem = mem.shape[1]
    depth = norm_mix.shape[0]
    n_gla = gla_w_in.shape[0]
    rank = gla_w_gate_up.shape[1]
    qk_width = gla_w_gate_up.shape[2]
    dk = qk_width // GLA_HEADS
    v_width = gla_norm.shape[1]
    dv = v_width // GLA_HEADS
    mem_width = mem_w_kv.shape[2] // 2
    fox_width = (fox_w_kv.shape[1] - FOX_HEADS) // 2
    assert mem_width == MEM_HEADS * LANES and fox_width == FOX_HEADS * LANES
    assert (2 * v_width) % dk == 0 and rank <= LANES

    xf = x.reshape(batch * seq_len, d)

    tr = _pick_tile(d, (512, 256, 128))
    w_mem = cast_blocks(mem_w_kv, (depth, d // tr), (None, tr, 2 * mem_width), lambda l, r: (l, r, 0),
                        (d, depth * 2 * mem_width), (tr, 2 * mem_width), lambda l, r: (r, l))
    ffn_up = stage_ffn_up(ffn_w_up)
    ffn_down = cast_rows(ffn_w_down)
    gla_out, fox_in, fox_out = cast_rows(gla_w_out), cast_rows(fox_w_in), cast_rows(fox_w_out)
    mem_kv = norm_matmul(mem.reshape(batch * n_mem, d), norm_mem, w_mem, BF16,
                         _pick_tile(w_mem.shape[1], (1024, 512)))

    fox_kv = fox_c = None
    for i in range(depth):
        if i == n_gla:
            w_kv = cast_blocks(fox_w_kv, (d // tr,), (tr, 2 * fox_width), lambda r: (r, 0),
                               (d, 2 * fox_width), (tr, 2 * fox_width), lambda r: (r, 0))
            fox_kv = norm_matmul(xf, fox_kv_norm, w_kv, BF16,
                                 _pick_tile(2 * fox_width, (1024, 768, 512)))
            w_f = jnp.pad(fox_w_kv[:, 2 * fox_width:], ((0, 0), (0, LANES - FOX_HEADS))).astype(BF16)
            b_f = jnp.pad(fox_b_f, (0, LANES - FOX_HEADS)).reshape(1, LANES)
            fox_c = fox_gates(xf, fox_kv_norm, w_f, b_f, batch, seq_len)
        if i < n_gla:
            w = gla_w_in[i]
            o_q, o_k, o_v = 0, qk_width, 2 * qk_width
            o_glr = o_v + v_width
            o_og = o_glr + rank
            o_mq = o_og + v_width
            w_in = jnp.concatenate([
                w[:, o_v:o_glr], w[:, o_og:o_mq], w[:, o_q:o_k], w[:, o_k:o_v], w[:, o_mq:],
                jnp.pad(w[:, o_glr:o_og], ((0, 0), (0, LANES - rank)))], axis=1).astype(BF16)
            z = norm_matmul(xf, norm_mix[i], w_in, BF16, 8 * MXU_TILE)
            a = gla_attention(z, gla_w_gate_up[i].reshape(rank, GLA_HEADS, dk).transpose(1, 0, 2).astype(BF16),
                              gla_b_gate[i].reshape(GLA_HEADS, 1, dk), gla_norm[i].reshape(GLA_HEADS, 1, dv),
                              batch, seq_len, dk, dv, rank)
            mq_block = (2 * v_width + 2 * qk_width) // mem_width
            w_out, w_layer = gla_out, i
        else:
            j = i - n_gla
            z = norm_matmul(xf, norm_mix[i], fox_in, BF16, _pick_tile(fox_w_in.shape[2], (1024, 512)), layer=j)
            a = fox_attention(z, fox_kv, fox_c, batch, seq_len)
            mq_block = fox_width // mem_width
            w_out, w_layer = fox_out, j
        xf = mixer_out(a, z, mq_block, mem_kv, i, w_out, w_layer, xf, seq_len, n_mem)
        xf = conv_ffn(xf, norm_ffn[i], ffn_up, ffn_conv_w[i], ffn_conv_b[i], ffn_down, i, norm_final,
                      seq_len, final_norm=(i == depth - 1))
    return xf.reshape(batch, seq_len, d)
```
